```python
import math
import jax, jax.numpy as jnp
from jax import lax
import numpy as np

D_MODEL = 1024
BATCH = 32
SEQ = 2048
DEPTH = 1

N_HEADS_ATTN = 8
N_KV_GROUPS = 2
HEADS_PER_GROUP = N_HEADS_ATTN // N_KV_GROUPS
HEAD_DIM = 64
ATTN_WIDTH = N_HEADS_ATTN * HEAD_DIM
KV_WIDTH = N_KV_GROUPS * HEAD_DIM
CMP_BLOCK = 32
CMP_STRIDE = 16
CMP_HIDDEN = 2 * HEAD_DIM
SEL_BLOCK = 64
N_SEL = 16
WINDOW = 512
SEL_QCHUNK = 32
WIN_QBLOCK = 128
ROPE_THETA = 10000.0
FORCE_SCORE = 1.0e4
MASK_VALUE = -1.0e30

SSM_WIDTH = 256
SSM_GROUP = 16
SSM_GROUPS = SSM_WIDTH // SSM_GROUP
SSM_STATE = 64
DT_MIN = 0.001
DT_MAX = 0.1

NORM_EPS = 1e-6
N_BRANCHES = 2
IN_SPLITS = (ATTN_WIDTH, 6 * KV_WIDTH, 3 * N_HEADS_ATTN, ATTN_WIDTH,
             SSM_WIDTH, SSM_WIDTH, N_BRANCHES * D_MODEL)
IN_WIDTH = sum(IN_SPLITS)
IN_OFFSETS = tuple(int(v) for v in np.cumsum(IN_SPLITS)[:-1])

kernel_name = "hybrid_nsa_s5_gated_block"


def rms_norm(x, gain):
    xf = x.astype(jnp.float32)
    y = xf * lax.rsqrt(jnp.mean(xf * xf, axis=-1, keepdims=True) + NORM_EPS)
    return (y * gain.astype(jnp.float32)).astype(x.dtype)


def rope_tables(seq, dtype):
    half = HEAD_DIM // 2
    inv_freq = ROPE_THETA ** (-jnp.arange(half, dtype=jnp.float32) / half)
    ang = jnp.arange(seq, dtype=jnp.float32)[:, None] * inv_freq[None, :]
    return jnp.cos(ang).astype(dtype), jnp.sin(ang).astype(dtype)


def apply_rope(t, cos, sin):
    t1, t2 = jnp.split(t, 2, axis=-1)
    c = cos[:, None, :]
    s = sin[:, None, :]
    return jnp.concatenate([t1 * c - t2 * s, t2 * c + t1 * s], axis=-1)


def compress_blocks(t, pe, w1, b1, w2):
    b, s, g, dh = t.shape
    nc = (s - CMP_BLOCK) // CMP_STRIDE + 1
    idx = jnp.arange(nc)[:, None] * CMP_STRIDE + jnp.arange(CMP_BLOCK)[None, :]
    blk = t[:, idx] + pe[None, None, :, None, :]
    blk = jnp.moveaxis(blk, 3, 2).reshape(b, nc, g, CMP_BLOCK * dh)
    hid = jax.nn.gelu(blk @ w1 + b1)
    return hid @ w2


def compressed_attention(q, kc, vc, scale):
    s_len = q.shape[1]
    nc = kc.shape[1]
    sc = jnp.einsum('bsghd,bngd->bghsn', q, kc).astype(jnp.float32) * scale
    blk_end = jnp.arange(nc) * CMP_STRIDE + CMP_BLOCK - 1
    valid = blk_end[None, :] <= jnp.arange(s_len)[:, None]
    p = jax.nn.softmax(jnp.where(valid, sc, MASK_VALUE), axis=-1) * valid
    o = jnp.einsum('bghsn,bngd->bsghd', p.astype(vc.dtype), vc)
    return o, p


def select_blocks(p_cmp, s_len):
    nc = p_cmp.shape[-1]
    ns = s_len // SEL_BLOCK
    cs = jnp.arange(nc) * CMP_STRIDE
    ss = jnp.arange(ns) * SEL_BLOCK
    overlap = (jnp.minimum(cs[:, None] + CMP_BLOCK, ss[None, :] + SEL_BLOCK)
               > jnp.maximum(cs[:, None], ss[None, :])).astype(jnp.float32)
    imp = jnp.einsum('bghsn,nj->bgsj', p_cmp, overlap)
    t = jnp.arange(s_len)[:, None]
    blk = jnp.arange(ns)[None, :]
    future = ss[None, :] > t
    forced = (blk == 0) | (blk == t // SEL_BLOCK)
    imp = jnp.where(forced, FORCE_SCORE, jnp.where(future, -1.0, imp))
    _, idx = lax.top_k(imp, min(N_SEL, ns))
    return idx


def selected_attention(q, ks, vs, idx, scale):
    b, s_len, g, hg, dh = q.shape
    ns = s_len // SEL_BLOCK
    k_sel = idx.shape[-1]
    nq = s_len // SEL_QCHUNK
    kb = ks.reshape(b, ns, SEL_BLOCK, g, dh).transpose(0, 3, 1, 2, 4)
    vb = vs.reshape(b, ns, SEL_BLOCK, g, dh).transpose(0, 3, 1, 2, 4)
    qc = q.reshape(b, nq, SEL_QCHUNK, g, hg, dh).transpose(1, 0, 3, 4, 2, 5)
    ic = idx.reshape(b, g, nq, SEL_QCHUNK, k_sel).transpose(2, 0, 1, 3, 4)
    bi = jnp.arange(b)[:, None, None, None]
    gi = jnp.arange(g)[None, :, None, None]

    def chunk(args):
        qi, ii, ci = args
        kg = kb[bi, gi, ii]
        vg = vb[bi, gi, ii]
        sc = jnp.einsum('bghtd,bgtkld->bghtkl', qi, kg).astype(jnp.float32) * scale
        kpos = ii[..., None] * SEL_BLOCK + jnp.arange(SEL_BLOCK)
        qpos = ci * SEL_QCHUNK + jnp.arange(SEL_QCHUNK)
        valid = kpos <= qpos[None, None, :, None, None]
        sc = jnp.where(valid[:, :, None], sc, MASK_VALUE).reshape(b, g, hg, SEL_QCHUNK, k_sel * SEL_BLOCK)
        p = jax.nn.softmax(sc, axis=-1).reshape(b, g, hg, SEL_QCHUNK, k_sel, SEL_BLOCK)
        return jnp.einsum('bghtkl,bgtkld->bghtd', p.astype(vg.dtype), vg)

    o = lax.map(chunk, (qc, ic, jnp.arange(nq)))
    return o.transpose(1, 0, 4, 2, 3, 5).reshape(b, s_len, g, hg, dh)


def window_attention(q, kw, vw, scale):
    b, s_len, g, hg, dh = q.shape
    nb = s_len // WIN_QBLOCK
    span = WINDOW + WIN_QBLOCK
    kp = jnp.pad(kw, ((0, 0), (WINDOW, 0), (0, 0), (0, 0)))
    vp = jnp.pad(vw, ((0, 0), (WINDOW, 0), (0, 0), (0, 0)))
    qb = q.reshape(b, nb, WIN_QBLOCK, g, hg, dh).transpose(1, 0, 3, 4, 2, 5)

    def block(args):
        qi, blk_id = args
        start = blk_id * WIN_QBLOCK
        kk = lax.dynamic_slice_in_dim(kp, start, span, axis=1)
        vv = lax.dynamic_slice_in_dim(vp, start, span, axis=1)
        sc = jnp.einsum('bghtd,bsgd->bghts', qi, kk).astype(jnp.float32) * scale
        qpos = start + jnp.arange(WIN_QBLOCK)
        kpos = start - WINDOW + jnp.arange(span)
        valid = ((kpos[None, :] <= qpos[:, None]) & (kpos[None, :] > qpos[:, None] - WINDOW)
                 & (kpos[None, :] >= 0))
        p = jax.nn.softmax(jnp.where(valid, sc, MASK_VALUE), axis=-1)
        return jnp.einsum('bghts,bsgd->bghtd', p.astype(vv.dtype), vv)

    o = lax.map(block, (qb, jnp.arange(nb)))
    return o.transpose(1, 0, 4, 2, 3, 5).reshape(b, s_len, g, hg, dh)


def _complex_linear_combine(left, right):
    ar1, ai1, br1, bi1 = left
    ar2, ai2, br2, bi2 = right
    ar = ar2 * ar1 - ai2 * ai1
    ai = ar2 * ai1 + ai2 * ar1
    xr = ar2 * br1 - ai2 * bi1 + br2
    xi = ar2 * bi1 + ai2 * br1 + bi2
    return ar, ai, xr, xi


def s5_ssm(u, lam_re, lam_im, log_dt, b_re, b_im, c_re, c_im, d_skip):
    b, s_len, _ = u.shape
    uf = u.astype(jnp.float32).reshape(b, s_len, SSM_GROUPS, SSM_GROUP)
    dt = jnp.exp(log_dt.astype(jnp.float32))[:, None]
    lr = lam_re.astype(jnp.float32)
    li = lam_im.astype(jnp.float32)
    mag = jnp.exp(lr * dt)
    ab_re = mag * jnp.cos(li * dt)
    ab_im = mag * jnp.sin(li * dt)
    nr = ab_re - 1.0
    ni = ab_im
    den = lr * lr + li * li
    cr = ((nr * lr + ni * li) / den)[..., None]
    ci = ((ni * lr - nr * li) / den)[..., None]
    br = b_re.astype(jnp.float32)
    bim = b_im.astype(jnp.float32)
    bb_re = cr * br - ci * bim
    bb_im = cr * bim + ci * br
    bu_re = jnp.einsum('bsgc,gpc->sbgp', uf, bb_re)
    bu_im = jnp.einsum('bsgc,gpc->sbgp', uf, bb_im)
    a_re = jnp.broadcast_to(ab_re, (s_len, 1) + ab_re.shape)
    a_im = jnp.broadcast_to(ab_im, (s_len, 1) + ab_im.shape)
    _, _, x_re, x_im = lax.associative_scan(_complex_linear_combine, (a_re, a_im, bu_re, bu_im), axis=0)
    y = (jnp.einsum('sbgp,gcp->bsgc', x_re, c_re.astype(jnp.float32))
         - jnp.einsum('sbgp,gcp->bsgc', x_im, c_im.astype(jnp.float32)))
    y = y.reshape(b, s_len, SSM_WIDTH) + d_skip.astype(jnp.float32) * uf.reshape(b, s_len, SSM_WIDTH)
    return y.astype(u.dtype)


def setup_inputs(seed: int = 0) -> dict:
    key = jax.random.key(seed)
    ks = jax.random.split(key, 32)
    L = DEPTH
    nrm = jax.random.normal
    f32 = jnp.float32
    lam_im_base = math.pi * jnp.arange(SSM_STATE, dtype=f32)
    return {
        "x": nrm(ks[0], (BATCH, SEQ, D_MODEL), f32),
        "c": nrm(ks[1], (BATCH, D_MODEL), f32),
        "w_ada": nrm(ks[2], (L, D_MODEL, 3 * D_MODEL), f32) * D_MODEL ** -0.5,
        "b_ada": nrm(ks[3], (L, 3 * D_MODEL), f32) * 0.02,
        "norm_gain": 1.0 + 0.05 * nrm(ks[4], (L, D_MODEL), f32),
        "w_in": nrm(ks[5], (L, D_MODEL, IN_WIDTH), f32) * D_MODEL ** -0.5,
        "pe_cmp_k": nrm(ks[6], (L, CMP_BLOCK, HEAD_DIM), f32) * 0.1,
        "w_cmp_k1": nrm(ks[7], (L, CMP_BLOCK * HEAD_DIM, CMP_HIDDEN), f32) * (CMP_BLOCK * HEAD_DIM) ** -0.5,
        "b_cmp_k1": nrm(ks[8], (L, CMP_HIDDEN), f32) * 0.02,
        "w_cmp_k2": nrm(ks[9], (L, CMP_HIDDEN, HEAD_DIM), f32) * CMP_HIDDEN ** -0.5,
        "pe_cmp_v": nrm(ks[10], (L, CMP_BLOCK, HEAD_DIM), f32) * 0.1,
        "w_cmp_v1": nrm(ks[11], (L, CMP_BLOCK * HEAD_DIM, CMP_HIDDEN), f32) * (CMP_BLOCK * HEAD_DIM) ** -0.5,
        "b_cmp_v1": nrm(ks[12], (L, CMP_HIDDEN), f32) * 0.02,
        "w_cmp_v2": nrm(ks[13], (L, CMP_HIDDEN, HEAD_DIM), f32) * CMP_HIDDEN ** -0.5,
        "lam_re": -0.5 + 0.01 * nrm(ks[14], (L, SSM_GROUPS, SSM_STATE), f32),
        "lam_im": lam_im_base + 0.01 * nrm(ks[15], (L, SSM_GROUPS, SSM_STATE), f32),
        "log_dt": jax.random.uniform(ks[16], (L, SSM_GROUPS), f32, math.log(DT_MIN), math.log(DT_MAX)),
        "b_re": nrm(ks[17], (L, SSM_GROUPS, SSM_STATE, SSM_GROUP), f32) * (2 * SSM_GROUP) ** -0.5,
        "b_im": nrm(ks[18], (L, SSM_GROUPS, SSM_STATE, SSM_GROUP), f32) * (2 * SSM_GROUP) ** -0.5,
        "c_re": nrm(ks[19], (L, SSM_GROUPS, SSM_GROUP, SSM_STATE), f32) * (2 * SSM_STATE) ** -0.5,
        "c_im": nrm(ks[20], (L, SSM_GROUPS, SSM_GROUP, SSM_STATE), f32) * (2 * SSM_STATE) ** -0.5,
        "d_skip": nrm(ks[21], (L, SSM_WIDTH), f32) * 0.5,
        "w_glu": nrm(ks[22], (L, SSM_WIDTH, 2 * SSM_WIDTH), f32) * SSM_WIDTH ** -0.5,
        "w_proj_attn": nrm(ks[23], (L, ATTN_WIDTH, D_MODEL), f32) * ATTN_WIDTH ** -0.5,
        "w_proj_ssm": nrm(ks[24], (L, SSM_WIDTH, D_MODEL), f32) * SSM_WIDTH ** -0.5,
        "w_out": nrm(ks[25], (L, D_MODEL, D_MODEL), f32) * D_MODEL ** -0.5,
        "final_gain": 1.0 + 0.05 * nrm(ks[26], (D_MODEL,), f32),
    }


def reference(x, c, w_ada, b_ada, norm_gain, w_in, pe_cmp_k, w_cmp_k1, b_cmp_k1, w_cmp_k2,
              pe_cmp_v, w_cmp_v1, b_cmp_v1, w_cmp_v2, lam_re, lam_im, log_dt, b_re, b_im,
              c_re, c_im, d_skip, w_glu, w_proj_attn, w_proj_ssm, w_out, final_gain):
    b, s_len, _ = x.shape
    scale = HEAD_DIM ** -0.5
    cos, sin = rope_tables(s_len, x.dtype)
    for l in range(DEPTH):
        mod = jax.nn.silu(c) @ w_ada[l] + b_ada[l]
        shift, mscale, gate = jnp.split(mod[:, None, :], 3, axis=-1)
        h = rms_norm(x, norm_gain[l]) * (1.0 + mscale) + shift

        proj = h @ w_in[l]
        q, kv, nsa_gl, z_attn, u, z_ssm, merge_gl = jnp.split(proj, IN_OFFSETS, axis=-1)

        q = apply_rope(q.reshape(b, s_len, N_HEADS_ATTN, HEAD_DIM), cos, sin)
        q = q.reshape(b, s_len, N_KV_GROUPS, HEADS_PER_GROUP, HEAD_DIM)
        k_c, v_c, k_s, v_s, k_w, v_w = [t.reshape(b, s_len, N_KV_GROUPS, HEAD_DIM)
                                        for t in jnp.split(kv, 6, axis=-1)]
        k_c = apply_rope(k_c, cos, sin)
        k_s = apply_rope(k_s, cos, sin)
        k_w = apply_rope(k_w, cos, sin)
        kc = compress_blocks(k_c, pe_cmp_k[l], w_cmp_k1[l], b_cmp_k1[l], w_cmp_k2[l])
        vc = compress_blocks(v_c, pe_cmp_v[l], w_cmp_v1[l], b_cmp_v1[l], w_cmp_v2[l])
        o_cmp, p_cmp = compressed_attention(q, kc, vc, scale)
        sel_idx = select_blocks(p_cmp, s_len)
        o_slc = selected_attention(q, k_s, v_s, sel_idx, scale)
        o_win = window_attention(q, k_w, v_w, scale)
        g3 = jax.nn.sigmoid(nsa_gl.reshape(b, s_len, N_KV_GROUPS, HEADS_PER_GROUP, 3))
        o_attn = (g3[..., 0:1] * o_cmp + g3[..., 1:2] * o_slc + g3[..., 2:3] * o_win)
        o_attn = o_attn.reshape(b, s_len, ATTN_WIDTH) * jax.nn.silu(z_attn)

        y = jax.nn.gelu(s5_ssm(u, lam_re[l], lam_im[l], log_dt[l], b_re[l], b_im[l],
                               c_re[l], c_im[l], d_skip[l]))
        ya, yb = jnp.split(y @ w_glu[l], 2, axis=-1)
        o_ssm = ya * jax.nn.sigmoid(yb) * jax.nn.silu(z_ssm)

        m_a, m_s = jnp.split(merge_gl, 2, axis=-1)
        merged = (jax.nn.sigmoid(m_a) * (o_attn @ w_proj_attn[l])
                  + jax.nn.sigmoid(m_s) * (o_ssm @ w_proj_ssm[l]))
        x = x + gate * (merged @ w_out[l])
    return rms_norm(x, final_gain)
```

```python
import functools
import math

import jax
import jax.numpy as jnp
import numpy as np
from jax import lax
from jax.experimental import pallas as pl
from jax.experimental.pallas import tpu as pltpu

F32 = jnp.float32
BF16 = jnp.bfloat16

N_HEADS = 8
N_GROUPS = 2
HEADS_PER_GROUP = N_HEADS // N_GROUPS
HEAD_DIM = 64
ATTN_WIDTH = N_HEADS * HEAD_DIM
KV_WIDTH = N_GROUPS * HEAD_DIM
CMP_BLOCK = 32
CMP_STRIDE = 16
CMP_HIDDEN = 2 * HEAD_DIM
SEL_BLOCK = 64
N_SEL = 16
WINDOW = 512
ROPE_THETA = 10000.0
FORCE_SCORE = 1.0e4
MASK_VALUE = -1.0e30
SSM_WIDTH = 256
SSM_GROUP = 16
SSM_GROUPS = SSM_WIDTH // SSM_GROUP
SSM_STATE = 64
NORM_EPS = 1e-6

LANES = 128
VMEM_LIMIT = 56 * 1024 * 1024

PROJ_ROWS = 512
ATTN_Q = 128
ATTN_K = 512
SSM_CHUNK = 8
SSM_HALVES = 2
SSM_BT = 8


def _dot(a, b):
    return jnp.dot(a, b, preferred_element_type=F32)


def _dot_nt(a, b):
    return lax.dot_general(a, b, (((1,), (1,)), ((), ())), preferred_element_type=F32)


def _dot_hilo(a, b_bf16):
    hi = a.astype(BF16)
    lo = (a - hi.astype(F32)).astype(BF16)
    return _dot(hi, b_bf16) + _dot(lo, b_bf16)


def _sigmoid(x):
    return 1.0 / (1.0 + jnp.exp(-x))


def _silu(x):
    return x * _sigmoid(x)


def _gelu(x):
    return 0.5 * x * (1.0 + jnp.tanh(math.sqrt(2.0 / math.pi) * (x + 0.044715 * (x * x * x))))


def _ada_kernel(c_ref, w_ref, b_ref, o_ref):
    c = c_ref[...]
    o_ref[...] = _dot(_silu(c).astype(BF16), w_ref[...].astype(BF16)) + b_ref[...]


def _ada(c, w, b):
    bsz, d = c.shape
    n = w.shape[1]
    return pl.pallas_call(
        _ada_kernel,
        grid=(n // d,),
        in_specs=[pl.BlockSpec((bsz, d), lambda j: (0, 0)),
                  pl.BlockSpec((d, d), lambda j: (0, j)),
                  pl.BlockSpec((1, d), lambda j: (0, j))],
        out_specs=pl.BlockSpec((bsz, d), lambda j: (0, j)),
        out_shape=jax.ShapeDtypeStruct((bsz, n), F32),
        compiler_params=pltpu.CompilerParams(vmem_limit_bytes=VMEM_LIMIT),
        name="ada",
    )(c, w, b.reshape(1, n))


_C_Q = 0
_C_KC = 512
_C_KS = 640
_C_KW = 768
_C_VC = 896
_C_VS = 1024
_C_VW = 1152
_C_G3 = 1280
_C_Z = 1408
_C_U = 1920
_C_ZS = 2176
_C_MA = 2432
_C_MS = 3456
_C_END = 4480


def _proj_kernel(x_ref, gain_ref, scale_ref, shift_ref, cos_ref, sa_ref, sb_ref, w_ref,
                 q_ref, kc_ref, ks_ref, kw_ref, vc_ref, vs_ref, vw_ref, g3_ref, sz_ref,
                 u_ref, szs_ref, sma_ref, sms_ref):
    x = x_ref[...]
    ms = jnp.mean(x * x, axis=-1, keepdims=True)
    h = x * lax.rsqrt(ms + NORM_EPS) * gain_ref[...]
    h = h * (1.0 + scale_ref[0]) + shift_ref[0]
    hb = h.astype(BF16)

    def proj(a, b):
        return _dot(hb, w_ref[:, a:b])

    cos = cos_ref[...]
    sa = sa_ref[...]
    sb = sb_ref[...]

    def rope(t):
        return t * cos + pltpu.roll(t, LANES - 32, 1) * sa + pltpu.roll(t, 32, 1) * sb

    for v in range(ATTN_WIDTH // LANES):
        q_ref[:, v * LANES:(v + 1) * LANES] = rope(proj(_C_Q + v * LANES, _C_Q + (v + 1) * LANES)).astype(BF16)
    kc_ref[...] = rope(proj(_C_KC, _C_KC + LANES))
    ks_ref[...] = rope(proj(_C_KS, _C_KS + LANES)).astype(BF16)
    kw_ref[...] = rope(proj(_C_KW, _C_KW + LANES)).astype(BF16)
    vc_ref[...] = proj(_C_VC, _C_VC + LANES)
    vs_ref[...] = proj(_C_VS, _C_VS + LANES).astype(BF16)
    vw_ref[...] = proj(_C_VW, _C_VW + LANES).astype(BF16)
    g3_ref[...] = _sigmoid(proj(_C_G3, _C_G3 + LANES))
    sz_ref[...] = _silu(proj(_C_Z, _C_U)).astype(BF16)
    u = proj(_C_U, _C_ZS).astype(BF16)
    u_ref[0] = u[:, :LANES]
    u_ref[1] = u[:, LANES:]
    szs_ref[...] = _silu(proj(_C_ZS, _C_MA)).astype(BF16)
    sma_ref[...] = _sigmoid(proj(_C_MA, _C_MS)).astype(BF16)
    sms_ref[...] = _sigmoid(proj(_C_MS, _C_END)).astype(BF16)


def _proj(x2, gain, scale, shift, cos_t, sa_t, sb_t, w, seq):
    t, d = x2.shape
    tm = PROJ_ROWS
    per_seq = seq // tm
    row = lambda n: pl.BlockSpec((tm, n), lambda i: (i, 0))
    mod = pl.BlockSpec((1, 1, d), lambda i: (i // per_seq, 0, 0))
    tab = pl.BlockSpec((tm, LANES), lambda i: (i % per_seq, 0))
    out_shape = [
        jax.ShapeDtypeStruct((t, ATTN_WIDTH), BF16),
        jax.ShapeDtypeStruct((t, LANES), F32),
        jax.ShapeDtypeStruct((t, LANES), BF16),
        jax.ShapeDtypeStruct((t, LANES), BF16),
        jax.ShapeDtypeStruct((t, LANES), F32),
        jax.ShapeDtypeStruct((t, LANES), BF16),
        jax.ShapeDtypeStruct((t, LANES), BF16),
        jax.ShapeDtypeStruct((t, LANES), F32),
        jax.ShapeDtypeStruct((t, ATTN_WIDTH), BF16),
        jax.ShapeDtypeStruct((SSM_HALVES, t, LANES), BF16),
        jax.ShapeDtypeStruct((t, SSM_WIDTH), BF16),
        jax.ShapeDtypeStruct((t, d), BF16),
        jax.ShapeDtypeStruct((t, d), BF16),
    ]
    out_specs = [row(ATTN_WIDTH), row(LANES), row(LANES), row(LANES), row(LANES), row(LANES), row(LANES),
                 row(LANES), row(ATTN_WIDTH),
                 pl.BlockSpec((SSM_HALVES, tm, LANES), lambda i: (0, i, 0)),
                 row(SSM_WIDTH), row(d), row(d)]
    return pl.pallas_call(
        _proj_kernel,
        grid=(t // tm,),
        in_specs=[row(d), pl.BlockSpec((1, d), lambda i: (0, 0)), mod, mod, tab, tab, tab,
                  pl.BlockSpec((d, _C_END), lambda i: (0, 0))],
        out_specs=out_specs,
        out_shape=out_shape,
        compiler_params=pltpu.CompilerParams(vmem_limit_bytes=VMEM_LIMIT),
        name="proj",
    )(x2, gain, scale, shift, cos_t, sa_t, sb_t, w)


def _compress_kernel(rk_ref, rv_ref, pea_k, peb_k, wa_k, wb_k, b1_k, w2_k,
                     pea_v, peb_v, wa_v, wb_v, b1_v, w2_v, kc_ref, vc_ref):
    def one(r_ref, pea, peb, wa, wb, b1, w2):
        r = r_ref[0]
        first = _dot((r + pea[...]).astype(BF16), wa[...])
        second = _dot((r + peb[...]).astype(BF16), wb[...])
        nxt = pltpu.roll(second, second.shape[0] - 1, 0)
        hid = _gelu(first + nxt + b1[...])
        return _dot(hid.astype(BF16), w2[...])

    kc_ref[0] = one(rk_ref, pea_k, peb_k, wa_k, wb_k, b1_k, w2_k)
    vc_ref[0] = one(rv_ref, pea_v, peb_v, wa_v, wb_v, b1_v, w2_v)


def _compress(rk, rv, pk, pv):
    bsz, nch, width = rk.shape
    full = lambda a: pl.BlockSpec(a.shape, lambda b: (0,) * a.ndim)
    rspec = pl.BlockSpec((1, nch, width), lambda b: (b, 0, 0))
    ospec = pl.BlockSpec((1, nch, LANES), lambda b: (b, 0, 0))
    return pl.pallas_call(
        _compress_kernel,
        grid=(bsz,),
        in_specs=[rspec, rspec] + [full(a) for a in pk] + [full(a) for a in pv],
        out_specs=[ospec, ospec],
        out_shape=[jax.ShapeDtypeStruct((bsz, nch, LANES), F32)] * 2,
        compiler_params=pltpu.CompilerParams(vmem_limit_bytes=VMEM_LIMIT),
        name="compress",
    )(rk, rv, *pk, *pv)


def _compress_params(pe, w1, b1, w2):
    half = CMP_BLOCK // 2
    eye = jnp.eye(N_GROUPS, dtype=F32)
    w1r = w1.reshape(CMP_BLOCK, HEAD_DIM, CMP_HIDDEN)

    def expand(wl):
        return jnp.einsum('ldj,gh->lgdhj', wl, eye).reshape(half * KV_WIDTH, N_GROUPS * CMP_HIDDEN)

    def pe_lanes(p):
        return jnp.broadcast_to(p[:, None, :], (half, N_GROUPS, HEAD_DIM)).reshape(1, half * KV_WIDTH)

    w2e = jnp.einsum('jd,gh->gjhd', w2, eye).reshape(N_GROUPS * CMP_HIDDEN, KV_WIDTH)
    return (pe_lanes(pe[:half]), pe_lanes(pe[half:]),
            expand(w1r[:half]).astype(BF16), expand(w1r[half:]).astype(BF16),
            jnp.tile(b1, N_GROUPS).reshape(1, N_GROUPS * CMP_HIDDEN), w2e.astype(BF16))


def _attn_kernel(q_ref, kc_ref, vc_ref, ks_ref, vs_ref, kw_ref, vw_ref, g3_ref, sz_ref,
                 ovl_ref, esel_ref, eg_ref, o_ref, bias_ref, m_ref, l_ref, acc_ref, *, seq):
    tq, tk = ATTN_Q, ATTN_K
    hg = HEADS_PER_GROUP
    q0 = pl.program_id(1) * tq
    lane = lax.broadcasted_iota(jnp.int32, (tq, LANES), 1)
    t_row = q0 + lax.broadcasted_iota(jnp.int32, (tq, LANES), 0)
    t4 = jnp.concatenate([t_row] * hg, axis=0)
    lane4 = jnp.concatenate([lane] * hg, axis=0)

    gexp = _dot_hilo(g3_ref[...], eg_ref[...])
    width = ATTN_WIDTH

    combined = []
    for g in range(N_GROUPS):
        in_group = (lane >= HEAD_DIM * g) & (lane < HEAD_DIM * (g + 1))
        zero = jnp.zeros((tq, LANES), BF16)
        qg = jnp.concatenate(
            [jnp.where(in_group, q_ref[:, h * LANES:(h + 1) * LANES], zero) for h in range(hg)], axis=0)

        sc = _dot_nt(qg, kc_ref[0].astype(BF16))
        valid = (lane4 * CMP_STRIDE + (CMP_BLOCK - 1)) <= t4
        s = jnp.where(valid, sc, MASK_VALUE)
        m = jnp.max(s, axis=-1, keepdims=True)
        e = jnp.where(valid, jnp.exp(s - m), 0.0)
        den = jnp.sum(e, axis=-1, keepdims=True)
        p = e / jnp.where(den > 0.0, den, 1.0)
        o_cmp = _dot(p.astype(BF16), vc_ref[0].astype(BF16))

        psum = p[0:tq]
        for h in range(1, hg):
            psum = psum + p[h * tq:(h + 1) * tq]
        imp = _dot_hilo(psum, ovl_ref[...])
        n_sel_blocks = seq // SEL_BLOCK
        forced = (lane == 0) | (lane == (t_row >> int(math.log2(SEL_BLOCK))))
        future = lane * SEL_BLOCK > t_row
        imp = jnp.where(forced, FORCE_SCORE, jnp.where(future, -1.0, imp))
        cnt = jnp.zeros((tq, LANES), jnp.int32)
        for i in range(n_sel_blocks):
            col = imp[:, i:i + 1]
            ahead = (col > imp) | ((col == imp) & (lane > i))
            cnt = cnt + ahead.astype(jnp.int32)
        sel = (cnt < min(N_SEL, n_sel_blocks)) & (lane < n_sel_blocks)
        selexp = _dot(jnp.where(sel, 1.0, 0.0).astype(BF16), esel_ref[...])
        kpos = lax.broadcasted_iota(jnp.int32, (tq, seq), 1)
        tq_pos = q0 + lax.broadcasted_iota(jnp.int32, (tq, seq), 0)
        bias_ref[...] = jnp.where((selexp > 0.5) & (kpos <= tq_pos), 0.0, MASK_VALUE)

        m_ref[...] = jnp.full(m_ref.shape, MASK_VALUE, F32)
        l_ref[...] = jnp.zeros(l_ref.shape, F32)
        acc_ref[...] = jnp.zeros(acc_ref.shape, F32)

        def body(kt, carry):
            ko = pl.multiple_of(kt * tk, tk)
            s2 = _dot_nt(qg, ks_ref[pl.ds(ko, tk), :])
            b = bias_ref[:, pl.ds(ko, tk)]
            s2 = (s2.reshape(hg, tq, tk) + b[None]).reshape(hg * tq, tk)
            m_old = m_ref[...]
            m_new = jnp.maximum(m_old, jnp.max(s2, axis=-1, keepdims=True))
            alpha = jnp.exp(m_old - m_new)
            p2 = jnp.exp(s2 - m_new)
            l_ref[...] = alpha * l_ref[...] + jnp.sum(p2, axis=-1, keepdims=True)
            acc_ref[...] = alpha * acc_ref[...] + _dot(p2.astype(BF16), vs_ref[pl.ds(ko, tk), :])
            m_ref[...] = m_new
            return carry

        lax.fori_loop(0, (q0 + tq + tk - 1) // tk, body, 0)
        o_slc = acc_ref[...] / l_ref[...]

        span = WINDOW + tq
        start = pl.multiple_of(jnp.maximum(q0 - WINDOW, 0), tq)
        s3 = _dot_nt(qg, kw_ref[pl.ds(start, span), :])
        kp = start + lax.broadcasted_iota(jnp.int32, (hg * tq, span), 1)
        tt = q0 + (lax.broadcasted_iota(jnp.int32, (hg * tq, span), 0) & (tq - 1))
        ok = (kp <= tt) & (kp > tt - WINDOW)
        s3 = jnp.where(ok, s3, MASK_VALUE)
        m3 = jnp.max(s3, axis=-1, keepdims=True)
        e3 = jnp.exp(s3 - m3)
        p3 = e3 / jnp.sum(e3, axis=-1, keepdims=True)
        o_win = _dot(p3.astype(BF16), vw_ref[pl.ds(start, span), :])

        combined.append((o_cmp, o_slc, o_win))

    first_half = lane < HEAD_DIM
    for h in range(hg):
        rows = slice(h * tq, (h + 1) * tq)
        cols = slice(h * LANES, (h + 1) * LANES)
        acc = jnp.zeros((tq, LANES), F32)
        for j in range(3):
            both = jnp.where(first_half, combined[0][j][rows], combined[1][j][rows])
            acc = acc + gexp[:, j * width + h * LANES: j * width + (h + 1) * LANES] * both
        o_ref[:, cols] = (acc * sz_ref[:, cols].astype(F32)).astype(BF16)


def _attn(q, kc, vc, ks, vs, kw, vw, g3, sz, ovl, esel, eg, bsz, seq):
    tq = ATTN_Q
    nq = seq // tq
    nch = kc.shape[1]
    qrow = lambda n: pl.BlockSpec((tq, n), lambda b, i: (b * nq + i, 0))
    per_seq = pl.BlockSpec((seq, LANES), lambda b, i: (b, 0))
    cmp_spec = pl.BlockSpec((1, nch, LANES), lambda b, i: (b, 0, 0))
    full = lambda a: pl.BlockSpec(a.shape, lambda b, i: (0,) * a.ndim)
    rows = HEADS_PER_GROUP * tq
    return pl.pallas_call(
        functools.partial(_attn_kernel, seq=seq),
        grid=(bsz, nq),
        in_specs=[qrow(ATTN_WIDTH), cmp_spec, cmp_spec, per_seq, per_seq, per_seq, per_seq,
                  qrow(LANES), qrow(ATTN_WIDTH), full(ovl), full(esel), full(eg)],
        out_specs=qrow(ATTN_WIDTH),
        out_shape=jax.ShapeDtypeStruct((bsz * seq, ATTN_WIDTH), BF16),
        scratch_shapes=[pltpu.VMEM((tq, seq), F32),
                        pltpu.VMEM((rows, 1), F32),
                        pltpu.VMEM((rows, 1), F32),
                        pltpu.VMEM((rows, LANES), F32)],
        compiler_params=pltpu.CompilerParams(vmem_limit_bytes=VMEM_LIMIT),
        name="attn",
    )(q, kc, vc, ks, vs, kw, vw, g3, sz, ovl, esel, eg)


def _ssm_kernel(u_ref, mi_ref, ws_ref, wo_ref, al_ref, y_ref, sx_ref, *, n_chunks):
    u = u_ref[0]
    n_tiles = sx_ref.shape[0]
    nt = n_tiles // 2
    tile = lambda j: slice(j * LANES, (j + 1) * LANES)
    for j in range(n_tiles):
        sx_ref[j] = _dot(u, ws_ref[0, :, tile(j)])
    a_re = [al_ref[0, 0:1, tile(j)] for j in range(nt)]
    a_im = [al_ref[0, 1:2, tile(j)] for j in range(nt)]
    bt = SSM_BT

    def body(k, carry):
        rows = pl.ds(k, bt, stride=n_chunks)
        new = []
        for j in range(nt):
            x_re, x_im = carry[j], carry[nt + j]
            s_re = sx_ref[j, rows, :]
            s_im = sx_ref[nt + j, rows, :]
            sx_ref[j, rows, :] = x_re
            sx_ref[nt + j, rows, :] = x_im
            new.append((a_re[j] * x_re - a_im[j] * x_im + s_re, a_re[j] * x_im + a_im[j] * x_re + s_im))
        return tuple([n[0] for n in new] + [n[1] for n in new])

    zero = jnp.zeros((bt, LANES), F32)
    lax.fori_loop(0, n_chunks, body, (zero,) * n_tiles)
    y = _dot(u, mi_ref[0])
    for j in range(n_tiles):
        y = y + _dot(sx_ref[j].astype(BF16), wo_ref[0, tile(j), :])
    y_ref[0] = y


def _ssm(u3, mi, ws, wo, al, n_chunks):
    halves, rows, width = u3.shape
    r = SSM_BT * n_chunks
    wspec = lambda a: pl.BlockSpec((1,) + a.shape[1:], lambda hf, i: (hf, 0, 0))
    return pl.pallas_call(
        functools.partial(_ssm_kernel, n_chunks=n_chunks),
        grid=(halves, rows // r),
        in_specs=[pl.BlockSpec((1, r, width), lambda hf, i: (hf, i, 0)),
                  wspec(mi), wspec(ws), wspec(wo), wspec(al)],
        out_specs=pl.BlockSpec((1, r, width), lambda hf, i: (hf, i, 0)),
        out_shape=jax.ShapeDtypeStruct((halves, rows, width), F32),
        scratch_shapes=[pltpu.VMEM((ws.shape[2] // LANES, r, LANES), F32)],
        compiler_params=pltpu.CompilerParams(vmem_limit_bytes=VMEM_LIMIT),
        name="ssm",
    )(u3, mi, ws, wo, al)


def _ssm_params(lam_re, lam_im, log_dt, b_re, b_im, c_re, c_im, d_skip):
    L = SSM_CHUNK
    G, P, C = SSM_GROUPS, SSM_STATE, SSM_GROUP
    gh = G // SSM_HALVES
    dt = jnp.exp(log_dt)[:, None]
    lr, li = lam_re, lam_im
    mag = jnp.exp(lr * dt)
    ab_re = mag * jnp.cos(li * dt)
    ab_im = mag * jnp.sin(li * dt)
    nr, ni = ab_re - 1.0, ab_im
    den = lr * lr + li * li
    cr = ((nr * lr + ni * li) / den)[..., None]
    ci = ((ni * lr - nr * li) / den)[..., None]
    bb_re = cr * b_re - ci * b_im
    bb_im = cr * b_im + ci * b_re
    pr, pi = [jnp.ones_like(ab_re)], [jnp.zeros_like(ab_im)]
    for _ in range(L):
        pr.append(pr[-1] * ab_re - pi[-1] * ab_im)
        pi.append(pr[-2] * ab_im + pi[-1] * ab_re)
    pw_re = jnp.stack(pr)
    pw_im = jnp.stack(pi)
    ca_re = jnp.einsum('gcp,dgp->dgcp', c_re, pw_re[:L]) - jnp.einsum('gcp,dgp->dgcp', c_im, pw_im[:L])
    ca_im = jnp.einsum('gcp,dgp->dgcp', c_re, pw_im[:L]) + jnp.einsum('gcp,dgp->dgcp', c_im, pw_re[:L])
    kern = jnp.einsum('dgop,gpi->dgoi', ca_re, bb_re) - jnp.einsum('dgop,gpi->dgoi', ca_im, bb_im)
    lag = jnp.arange(L)[None, :] - jnp.arange(L)[:, None]
    kt = jnp.where((lag >= 0)[:, :, None, None, None], kern[jnp.clip(lag, 0, L - 1)], 0.0)
    eye_g = jnp.eye(G, dtype=F32)
    m_intra = jnp.einsum('stgoi,gh->sgitho', kt, eye_g)
    skip = jnp.einsum('st,gh,io,go->sgitho', jnp.eye(L, dtype=F32), eye_g, jnp.eye(C, dtype=F32),
                      d_skip.reshape(G, C))
    m_intra = m_intra + skip
    e_re, e_im = pw_re[:L][::-1], pw_im[:L][::-1]
    ws_re = jnp.einsum('sgp,gpi->sgip', e_re, bb_re) - jnp.einsum('sgp,gpi->sgip', e_im, bb_im)
    ws_im = jnp.einsum('sgp,gpi->sgip', e_re, bb_im) + jnp.einsum('sgp,gpi->sgip', e_im, bb_re)
    o_re, o_im = pw_re[1:L + 1], pw_im[1:L + 1]
    wo_re = jnp.einsum('gop,tgp->gpto', c_re, o_re) - jnp.einsum('gop,tgp->gpto', c_im, o_im)
    wo_im = -(jnp.einsum('gop,tgp->gpto', c_re, o_im) + jnp.einsum('gop,tgp->gpto', c_im, o_re))

    mi_h, ws_h, wo_h, al_h = [], [], [], []
    for hf in range(SSM_HALVES):
        gs = slice(hf * gh, (hf + 1) * gh)
        mi_h.append(m_intra[:, gs, :, :, gs, :].reshape(L * gh * C, L * gh * C))
        eye_h = jnp.eye(gh, dtype=F32)
        wsr = jnp.einsum('sgip,gh->sgihp', ws_re[:, gs], eye_h).reshape(L * gh * C, gh * P)
        wsi = jnp.einsum('sgip,gh->sgihp', ws_im[:, gs], eye_h).reshape(L * gh * C, gh * P)
        ws_h.append(jnp.concatenate([wsr, wsi], axis=1))
        wor = jnp.einsum('gpto,gh->gptho', wo_re[gs], eye_h).reshape(gh * P, L * gh * C)
        woi = jnp.einsum('gpto,gh->gptho', wo_im[gs], eye_h).reshape(gh * P, L * gh * C)
        wo_h.append(jnp.concatenate([wor, woi], axis=0))
        al_h.append(jnp.stack([pw_re[L, gs].reshape(gh * P), pw_im[L, gs].reshape(gh * P)]))
    return (jnp.stack(mi_h).astype(BF16), jnp.stack(ws_h).astype(BF16), jnp.stack(wo_h).astype(BF16),
            jnp.stack(al_h))


def _tail_kernel(x_ref, oa_ref, y_ref, szs_ref, sma_ref, sms_ref, gate_ref, wglu_ref, wpa_ref, wps_ref,
                 wout_ref, fg_ref, o_ref):
    y = jnp.concatenate([y_ref[0], y_ref[1]], axis=-1)
    yy = _dot(_gelu(y).astype(BF16), wglu_ref[...])
    o_ssm = yy[:, :SSM_WIDTH] * _sigmoid(yy[:, SSM_WIDTH:]) * szs_ref[...].astype(F32)
    merged = (sma_ref[...].astype(F32) * _dot(oa_ref[...], wpa_ref[...])
              + sms_ref[...].astype(F32) * _dot(o_ssm.astype(BF16), wps_ref[...]))
    xo = x_ref[...] + gate_ref[0] * _dot(merged.astype(BF16), wout_ref[...])
    ms = jnp.mean(xo * xo, axis=-1, keepdims=True)
    o_ref[...] = xo * lax.rsqrt(ms + NORM_EPS) * fg_ref[...]


def _tail(x2, oa, y3, szs, sma, sms, gate, wglu, wpa, wps, wout, fgain, seq):
    t, d = x2.shape
    tm = PROJ_ROWS
    per_seq = seq // tm
    row = lambda n: pl.BlockSpec((tm, n), lambda i: (i, 0))
    full = lambda a: pl.BlockSpec(a.shape, lambda i: (0,) * a.ndim)
    return pl.pallas_call(
        _tail_kernel,
        grid=(t // tm,),
        in_specs=[row(d), row(ATTN_WIDTH), pl.BlockSpec((SSM_HALVES, tm, LANES), lambda i: (0, i, 0)),
                  row(SSM_WIDTH), row(d), row(d),
                  pl.BlockSpec((1, 1, d), lambda i: (i // per_seq, 0, 0)),
                  full(wglu), full(wpa), full(wps), full(wout), full(fgain)],
        out_specs=row(d),
        out_shape=jax.ShapeDtypeStruct((t, d), F32),
        compiler_params=pltpu.CompilerParams(vmem_limit_bytes=VMEM_LIMIT),
        name="tail",
    )(x2, oa, y3, szs, sma, sms, gate, wglu, wpa, wps, wout, fgain)


def _head_perm():
    idx = np.arange(ATTN_WIDTH)
    h, rem = idx // LANES, idx % LANES
    g, dd = rem // HEAD_DIM, rem % HEAD_DIM
    return (g * HEADS_PER_GROUP + h) * HEAD_DIM + dd


def _rope_tables(seq):
    half = HEAD_DIM // 2
    inv_freq = ROPE_THETA ** (-jnp.arange(half, dtype=F32) / half)
    ang = jnp.arange(seq, dtype=F32)[:, None] * inv_freq[None, :]
    cos, sin = jnp.cos(ang), jnp.sin(ang)
    zero = jnp.zeros_like(sin)
    reps = LANES // HEAD_DIM
    cos_t = jnp.tile(jnp.concatenate([cos, cos], axis=1), (1, reps))
    sa_t = jnp.tile(jnp.concatenate([-sin, zero], axis=1), (1, reps))
    sb_t = jnp.tile(jnp.concatenate([zero, sin], axis=1), (1, reps))
    return cos_t, sa_t, sb_t


def _proj_weight(w_in):
    d = w_in.shape[0]
    o_q, o_kv, o_g, o_z, o_u, o_zs, o_m = 0, 512, 1280, 1304, 1816, 2072, 2328
    perm = _head_perm()
    kv = lambda j: w_in[:, o_kv + j * KV_WIDTH: o_kv + (j + 1) * KV_WIDTH]
    gates = jnp.pad(w_in[:, o_g:o_z], ((0, 0), (0, LANES - 3 * N_HEADS)))
    cols = [w_in[:, o_q:o_kv][:, perm] * (HEAD_DIM ** -0.5),
            kv(0), kv(2), kv(4), kv(1), kv(3), kv(5), gates,
            w_in[:, o_z:o_u][:, perm], w_in[:, o_u:o_zs], w_in[:, o_zs:o_m], w_in[:, o_m:]]
    w = jnp.concatenate(cols, axis=1)
    assert w.shape == (d, _C_END)
    return w.astype(BF16)


def _attn_constants(seq):
    n_cmp = LANES
    cs = np.arange(n_cmp) * CMP_STRIDE
    ss = np.arange(LANES) * SEL_BLOCK
    ovl = (np.minimum(cs[:, None] + CMP_BLOCK, ss[None, :] + SEL_BLOCK) > np.maximum(cs[:, None], ss[None, :]))
    ovl = ovl & (np.arange(LANES)[None, :] < seq // SEL_BLOCK) & (cs[:, None] + CMP_BLOCK <= seq)
    esel = (np.arange(seq)[None, :] // SEL_BLOCK) == np.arange(LANES)[:, None]
    eg = np.zeros((LANES, 3 * ATTN_WIDTH), np.float32)
    for g in range(N_GROUPS):
        for h in range(HEADS_PER_GROUP):
            for j in range(3):
                base = j * ATTN_WIDTH + h * LANES + g * HEAD_DIM
                eg[(g * HEADS_PER_GROUP + h) * 3 + j, base:base + HEAD_DIM] = 1.0
    return (jnp.asarray(ovl, BF16), jnp.asarray(esel, BF16), jnp.asarray(eg, BF16))


def kernel(x, c, w_ada, b_ada, norm_gain, w_in, pe_cmp_k, w_cmp_k1, b_cmp_k1, w_cmp_k2, pe_cmp_v, w_cmp_v1,
           b_cmp_v1, w_cmp_v2, lam_re, lam_im, log_dt, b_re, b_im, c_re, c_im, d_skip, w_glu, w_proj_attn,
           w_proj_ssm, w_out, final_gain):
    bsz, seq, d = x.shape
    depth = w_in.shape[0]
    assert depth == 1, "the tail kernel fuses the final norm into the (single) layer"
    assert bsz % SSM_BT == 0 and seq % SSM_CHUNK == 0
    assert seq % (ATTN_K) == 0 and seq % PROJ_ROWS == 0 and seq >= WINDOW + ATTN_Q
    assert (seq - CMP_BLOCK) // CMP_STRIDE + 1 <= LANES and seq // CMP_STRIDE == LANES
    t = bsz * seq
    cos_t, sa_t, sb_t = _rope_tables(seq)
    ovl, esel, eg = _attn_constants(seq)
    perm = _head_perm()
    n_chunks = seq // SSM_CHUNK

    x2 = x.reshape(t, d)
    for l in range(depth):
        mod = _ada(c, w_ada[l], b_ada[l])
        shift, mscale, gate = [m.reshape(bsz, 1, d) for m in jnp.split(mod, 3, axis=-1)]
        (q, kc_r, ks, kw, vc_r, vs, vw, g3, sz, u, szs, sma, sms) = _proj(
            x2, norm_gain[l].reshape(1, d), mscale, shift, cos_t, sa_t, sb_t, _proj_weight(w_in[l]), seq)

        nch = seq // CMP_STRIDE
        kc, vc = _compress(
            kc_r.reshape(bsz, nch, CMP_STRIDE * KV_WIDTH), vc_r.reshape(bsz, nch, CMP_STRIDE * KV_WIDTH),
            _compress_params(pe_cmp_k[l], w_cmp_k1[l], b_cmp_k1[l], w_cmp_k2[l]),
            _compress_params(pe_cmp_v[l], w_cmp_v1[l], b_cmp_v1[l], w_cmp_v2[l]))
        o_attn = _attn(q, kc, vc, ks, vs, kw, vw, g3, sz, ovl, esel, eg, bsz, seq)

        mi, ws, wo, al = _ssm_params(lam_re[l], lam_im[l], log_dt[l], b_re[l], b_im[l], c_re[l], c_im[l],
                                     d_skip[l])
        y3 = _ssm(u.reshape(SSM_HALVES, t // SSM_CHUNK, SSM_CHUNK * LANES), mi, ws, wo, al, n_chunks)
        y3 = y3.reshape(SSM_HALVES, t, LANES)

        x2 = _tail(x2, o_attn, y3, szs, sma, sms, gate, w_glu[l].astype(BF16),
                   w_proj_attn[l][perm].astype(BF16), w_proj_ssm[l].astype(BF16), w_out[l].astype(BF16),
                   final_gain.reshape(1, d) if l == depth - 1 else jnp.ones((1, d), F32), seq)
    return x2.reshape(bsz, seq, d)
```

```python
import functools
import math

import jax
import jax.numpy as jnp
import numpy as np
from jax import lax
from jax.experimental import pallas as pl
from jax.experimental.pallas import tpu as pltpu

F32 = jnp.float32
BF16 = jnp.bfloat16

N_HEADS = 8
N_GROUPS = 2
HEADS_PER_GROUP = N_HEADS // N_GROUPS
HEAD_DIM = 64
ATTN_WIDTH = N_HEADS * HEAD_DIM
KV_WIDTH = N_GROUPS * HEAD_DIM
CMP_BLOCK = 32
CMP_STRIDE = 16
CMP_HIDDEN = 2 * HEAD_DIM
SEL_BLOCK = 64
N_SEL = 16
WINDOW = 512
ROPE_THETA = 10000.0
FORCE_SCORE = 1.0e4
MASK_VALUE = -1.0e30
SSM_WIDTH = 256
SSM_GROUP = 16
SSM_GROUPS = SSM_WIDTH // SSM_GROUP
SSM_STATE = 64
NORM_EPS = 1e-6

LANES = 128
VMEM_LIMIT = 56 * 1024 * 1024

PROJ_ROWS = 512
ATTN_Q = 128
ATTN_K = 512
SSM_CHUNK = 8
SSM_HALVES = 2
SSM_BT = 8


def _dot(a, b):
    return jnp.dot(a, b, preferred_element_type=F32)


def _dot_nt(a, b):
    return lax.dot_general(a, b, (((1,), (1,)), ((), ())), preferred_element_type=F32)


def _dot_hilo(a, b_bf16):
    hi = a.astype(BF16)
    lo = (a - hi.astype(F32)).astype(BF16)
    return _dot(hi, b_bf16) + _dot(lo, b_bf16)


def _sigmoid(x):
    return 1.0 / (1.0 + jnp.exp(-x))


def _silu(x):
    return x * _sigmoid(x)


def _gelu(x):
    return 0.5 * x * (1.0 + jnp.tanh(math.sqrt(2.0 / math.pi) * (x + 0.044715 * (x * x * x))))


def _ada_kernel(c_ref, w_ref, b_ref, o_ref):
    c = c_ref[...]
    o_ref[...] = _dot(_silu(c).astype(BF16), w_ref[...].astype(BF16)) + b_ref[...]


def _ada(c, w, b):
    bsz, d = c.shape
    n = w.shape[1]
    return pl.pallas_call(
        _ada_kernel,
        grid=(n // d,),
        in_specs=[pl.BlockSpec((bsz, d), lambda j: (0, 0)),
                  pl.BlockSpec((d, d), lambda j: (0, j)),
                  pl.BlockSpec((1, d), lambda j: (0, j))],
        out_specs=pl.BlockSpec((bsz, d), lambda j: (0, j)),
        out_shape=jax.ShapeDtypeStruct((bsz, n), F32),
        compiler_params=pltpu.CompilerParams(vmem_limit_bytes=VMEM_LIMIT),
        name="ada",
    )(c, w, b.reshape(1, n))


_C_Q = 0
_C_KC = 512
_C_KS = 640
_C_KW = 768
_C_VC = 896
_C_VS = 1024
_C_VW = 1152
_C_G3 = 1280
_C_Z = 1408
_C_U = 1920
_C_ZS = 2176
_C_MA = 2432
_C_MS = 3456
_C_END = 4480


def _proj_kernel(x_ref, gain_ref, scale_ref, shift_ref, cos_ref, sa_ref, sb_ref, w_ref,
                 q_ref, kc_ref, ks0_ref, ks1_ref, kw_ref, vc_ref, vs0_ref, vs1_ref, vw0_ref, vw1_ref,
                 g3_ref, sz_ref, u_ref, szs_ref, sma_ref, sms_ref, *, per_seq):
    x = x_ref[...]
    ms = jnp.mean(x * x, axis=-1, keepdims=True)
    h = x * lax.rsqrt(ms + NORM_EPS) * gain_ref[...]
    h = h * (1.0 + scale_ref[0]) + shift_ref[0]
    hb = h.astype(BF16)

    def proj(a, b):
        return _dot(hb, w_ref[:, a:b])

    cos = cos_ref[...]
    sa = sa_ref[...]
    sb = sb_ref[...]

    def rope(t):
        return t * cos + pltpu.roll(t, LANES - 32, 1) * sa + pltpu.roll(t, 32, 1) * sb

    for v in range(ATTN_WIDTH // LANES):
        q_ref[:, v * LANES:(v + 1) * LANES] = rope(proj(_C_Q + v * LANES, _C_Q + (v + 1) * LANES)).astype(BF16)
    kc_ref[...] = rope(proj(_C_KC, _C_KC + LANES))
    kw_ref[...] = rope(proj(_C_KW, _C_KW + LANES)).astype(BF16)
    vc_ref[...] = proj(_C_VC, _C_VC + LANES)
    tm = x.shape[0]
    lane = lax.broadcasted_iota(jnp.int32, (tm, LANES), 1)
    pos = (pl.program_id(0) % per_seq) * tm + lax.broadcasted_iota(jnp.int32, (tm, LANES), 0)
    blk = pos >> int(math.log2(SEL_BLOCK))
    first = lane < HEAD_DIM
    ks = rope(proj(_C_KS, _C_KS + LANES))
    ks0_ref[...] = jnp.where(first, ks, jnp.where(lane - HEAD_DIM == blk, 1.0, 0.0)).astype(BF16)
    ks1_ref[...] = jnp.where(first, jnp.where(lane == blk, 1.0, 0.0), ks).astype(BF16)
    vs = proj(_C_VS, _C_VS + LANES)
    vs0_ref[...] = jnp.where(first, vs, 1.0).astype(BF16)
    vs1_ref[...] = jnp.where(first, 1.0, vs).astype(BF16)
    vw = proj(_C_VW, _C_VW + LANES)
    vw0_ref[...] = jnp.where(first, vw, 1.0).astype(BF16)
    vw1_ref[...] = jnp.where(first, 1.0, vw).astype(BF16)
    g3_ref[...] = _sigmoid(proj(_C_G3, _C_G3 + LANES))
    sz_ref[...] = _silu(proj(_C_Z, _C_U)).astype(BF16)
    u = proj(_C_U, _C_ZS).astype(BF16)
    u_ref[0] = u[:, :LANES]
    u_ref[1] = u[:, LANES:]
    szs_ref[...] = _silu(proj(_C_ZS, _C_MA)).astype(BF16)
    sma_ref[...] = _sigmoid(proj(_C_MA, _C_MS)).astype(BF16)
    sms_ref[...] = _sigmoid(proj(_C_MS, _C_END)).astype(BF16)


def _proj(x2, gain, scale, shift, cos_t, sa_t, sb_t, w, seq):
    t, d = x2.shape
    tm = PROJ_ROWS
    per_seq = seq // tm
    row = lambda n: pl.BlockSpec((tm, n), lambda i: (i, 0))
    mod = pl.BlockSpec((1, 1, d), lambda i: (i // per_seq, 0, 0))
    tab = pl.BlockSpec((tm, LANES), lambda i: (i % per_seq, 0))
    out_shape = [
        jax.ShapeDtypeStruct((t, ATTN_WIDTH), BF16),
        jax.ShapeDtypeStruct((t, LANES), F32),
        jax.ShapeDtypeStruct((t, LANES), BF16),
        jax.ShapeDtypeStruct((t, LANES), BF16),
        jax.ShapeDtypeStruct((t, LANES), BF16),
        jax.ShapeDtypeStruct((t, LANES), F32),
        jax.ShapeDtypeStruct((t, LANES), BF16),
        jax.ShapeDtypeStruct((t, LANES), BF16),
        jax.ShapeDtypeStruct((t, LANES), BF16),
        jax.ShapeDtypeStruct((t, LANES), BF16),
        jax.ShapeDtypeStruct((t, LANES), F32),
        jax.ShapeDtypeStruct((t, ATTN_WIDTH), BF16),
        jax.ShapeDtypeStruct((SSM_HALVES, t, LANES), BF16),
        jax.ShapeDtypeStruct((t, SSM_WIDTH), BF16),
        jax.ShapeDtypeStruct((t, d), BF16),
        jax.ShapeDtypeStruct((t, d), BF16),
    ]
    out_specs = [row(ATTN_WIDTH)] + [row(LANES)] * 10 + [row(ATTN_WIDTH),
                 pl.BlockSpec((SSM_HALVES, tm, LANES), lambda i: (0, i, 0)),
                 row(SSM_WIDTH), row(d), row(d)]
    return pl.pallas_call(
        functools.partial(_proj_kernel, per_seq=per_seq),
        grid=(t // tm,),
        in_specs=[row(d), pl.BlockSpec((1, d), lambda i: (0, 0)), mod, mod, tab, tab, tab,
                  pl.BlockSpec((d, _C_END), lambda i: (0, 0))],
        out_specs=out_specs,
        out_shape=out_shape,
        compiler_params=pltpu.CompilerParams(vmem_limit_bytes=VMEM_LIMIT),
        name="proj",
    )(x2, gain, scale, shift, cos_t, sa_t, sb_t, w)


def _compress_kernel(rk_ref, rv_ref, pea_k, peb_k, wa_k, wb_k, b1_k, w2_k,
                     pea_v, peb_v, wa_v, wb_v, b1_v, w2_v, kc_ref, vc_ref):
    def one(r_ref, pea, peb, wa, wb, b1, w2):
        r = r_ref[0]
        first = _dot((r + pea[...]).astype(BF16), wa[...])
        second = _dot((r + peb[...]).astype(BF16), wb[...])
        nxt = pltpu.roll(second, second.shape[0] - 1, 0)
        hid = _gelu(first + nxt + b1[...])
        return _dot(hid.astype(BF16), w2[...])

    kc_ref[0] = one(rk_ref, pea_k, peb_k, wa_k, wb_k, b1_k, w2_k)
    vc_ref[0] = one(rv_ref, pea_v, peb_v, wa_v, wb_v, b1_v, w2_v)


def _compress(rk, rv, pk, pv):
    bsz, nch, width = rk.shape
    full = lambda a: pl.BlockSpec(a.shape, lambda b: (0,) * a.ndim)
    rspec = pl.BlockSpec((1, nch, width), lambda b: (b, 0, 0))
    ospec = pl.BlockSpec((1, nch, LANES), lambda b: (b, 0, 0))
    return pl.pallas_call(
        _compress_kernel,
        grid=(bsz,),
        in_specs=[rspec, rspec] + [full(a) for a in pk] + [full(a) for a in pv],
        out_specs=[ospec, ospec],
        out_shape=[jax.ShapeDtypeStruct((bsz, nch, LANES), F32)] * 2,
        compiler_params=pltpu.CompilerParams(vmem_limit_bytes=VMEM_LIMIT),
        name="compress",
    )(rk, rv, *pk, *pv)


def _compress_params(pe, w1, b1, w2):
    half = CMP_BLOCK // 2
    eye = jnp.eye(N_GROUPS, dtype=F32)
    w1r = w1.reshape(CMP_BLOCK, HEAD_DIM, CMP_HIDDEN)

    def expand(wl):
        return jnp.einsum('ldj,gh->lgdhj', wl, eye).reshape(half * KV_WIDTH, N_GROUPS * CMP_HIDDEN)

    def pe_lanes(p):
        return jnp.broadcast_to(p[:, None, :], (half, N_GROUPS, HEAD_DIM)).reshape(1, half * KV_WIDTH)

    w2e = jnp.einsum('jd,gh->gjhd', w2, eye).reshape(N_GROUPS * CMP_HIDDEN, KV_WIDTH)
    return (pe_lanes(pe[:half]), pe_lanes(pe[half:]),
            expand(w1r[:half]).astype(BF16), expand(w1r[half:]).astype(BF16),
            jnp.tile(b1, N_GROUPS).reshape(1, N_GROUPS * CMP_HIDDEN), w2e.astype(BF16))


def _attn_kernel(q_ref, kc_ref, vc_ref, ks0_ref, ks1_ref, vs0_ref, vs1_ref, kw_ref, vw0_ref, vw1_ref,
                 g3_ref, sz_ref, ovlt_ref, eg_ref, dbias_ref, o_ref, slc_ref, *, seq):
    tq, tk = ATTN_Q, ATTN_K
    hg = HEADS_PER_GROUP
    rows = hg * tq
    n_sel_blocks = seq // SEL_BLOCK
    sel_shift = int(math.log2(SEL_BLOCK))
    q0 = pl.program_id(1) * tq
    lane = lax.broadcasted_iota(jnp.int32, (tq, LANES), 1)
    t_row = q0 + lax.broadcasted_iota(jnp.int32, (tq, LANES), 0)
    cvalid = (lane * CMP_STRIDE + (CMP_BLOCK - 1)) <= t_row

    span = WINDOW + tq
    start = pl.multiple_of(jnp.maximum(q0 - WINDOW, 0), tq)
    kp = start + lax.broadcasted_iota(jnp.int32, (tq, span), 1)
    tt = q0 + lax.broadcasted_iota(jnp.int32, (tq, span), 0)
    wbias = jnp.where((kp <= tt) & (kp > tt - WINDOW), 0.0, MASK_VALUE)

    jidx = lax.broadcasted_iota(jnp.int32, (n_sel_blocks, tq), 0)
    t_lane = q0 + lax.broadcasted_iota(jnp.int32, (n_sel_blocks, tq), 1)
    forced = (jidx == 0) | (jidx == (t_lane >> sel_shift))
    future = jidx * SEL_BLOCK > t_lane

    gexp = _dot_hilo(g3_ref[...], eg_ref[...])
    width = ATTN_WIDTH
    ks_refs, vs_refs, vw_refs = (ks0_ref, ks1_ref), (vs0_ref, vs1_ref), (vw0_ref, vw1_ref)

    combined = []
    for g in range(N_GROUPS):
        in_group = (lane >= HEAD_DIM * g) & (lane < HEAD_DIM * (g + 1))
        zero = jnp.zeros((tq, LANES), BF16)
        q_heads = [q_ref[:, h * LANES:(h + 1) * LANES] for h in range(hg)]
        qg = jnp.concatenate([jnp.where(in_group, qh, zero) for qh in q_heads], axis=0)

        sc = _dot_nt(qg, kc_ref[0].astype(BF16)).reshape(hg, tq, LANES)
        s = jnp.where(cvalid[None], sc, MASK_VALUE)
        m = jnp.max(s, axis=-1, keepdims=True)
        e = jnp.where(cvalid[None], jnp.exp(s - m), 0.0)
        den = jnp.sum(e, axis=-1, keepdims=True)
        p = e * (1.0 / jnp.where(den > 0.0, den, 1.0))
        o_cmp = _dot(p.reshape(rows, LANES).astype(BF16), vc_ref[0].astype(BF16))

        psum = p[0]
        for h in range(1, hg):
            psum = psum + p[h]
        p_hi = psum.astype(BF16)
        p_lo = (psum - p_hi.astype(F32)).astype(BF16)
        imp = (_dot_nt(ovlt_ref[...], p_hi) + _dot_nt(ovlt_ref[...], p_lo))[0:n_sel_blocks]
        imp = jnp.where(forced, FORCE_SCORE, jnp.where(future, -1.0, imp))
        cnt = jnp.zeros((n_sel_blocks, tq), jnp.int32)
        for i in range(n_sel_blocks):
            other = imp[i:i + 1, :]
            ahead = (other > imp) | ((other == imp) & (jidx > i))
            cnt = cnt + ahead.astype(jnp.int32)
        mask_t = jnp.where(cnt < min(N_SEL, n_sel_blocks), 0.0, MASK_VALUE)
        lo_rows = HEAD_DIM * (1 - g)
        parts = [mask_t, jnp.zeros((LANES - n_sel_blocks - lo_rows, tq), F32)]
        if lo_rows:
            parts = [jnp.zeros((lo_rows, tq), F32)] + parts
        maskq = jnp.concatenate(parts, axis=0).T.astype(BF16)
        qsel = jnp.concatenate([jnp.where(in_group, qh, maskq) for qh in q_heads], axis=0)

        n_tiles = q0 // tk + 1
        for n in range(1, seq // tk + 1):
            @pl.when(n_tiles == n)
            def _(n=n, ks_ref=ks_refs[g], vs_ref=vs_refs[g], qsel=qsel):
                klen = n * tk
                s2 = _dot_nt(qsel, ks_ref[0:klen, :])
                last = (s2[:, klen - tk:].reshape(hg, tq, tk) + dbias_ref[0][None]).reshape(rows, tk)
                s2 = last if n == 1 else jnp.concatenate([s2[:, :klen - tk], last], axis=1)
                m2 = jnp.max(s2, axis=-1, keepdims=True)
                p2 = jnp.exp(s2 - m2).astype(BF16)
                slc_ref[...] = _dot(p2, vs_ref[0:klen, :])
        ov = slc_ref[...]
        o_slc = ov * (1.0 / pltpu.roll(ov, HEAD_DIM, 1))

        s3 = _dot_nt(qg, kw_ref[pl.ds(start, span), :]).reshape(hg, tq, span) + wbias[None]
        m3 = jnp.max(s3, axis=-1, keepdims=True)
        p3 = jnp.exp(s3 - m3).astype(BF16).reshape(rows, span)
        ow = _dot(p3, vw_refs[g][pl.ds(start, span), :])
        o_win = ow * (1.0 / pltpu.roll(ow, HEAD_DIM, 1))

        combined.append((o_cmp, o_slc, o_win))

    first_half = lane < HEAD_DIM
    for h in range(hg):
        rws = slice(h * tq, (h + 1) * tq)
        cols = slice(h * LANES, (h + 1) * LANES)
        acc = jnp.zeros((tq, LANES), F32)
        for j in range(3):
            both = jnp.where(first_half, combined[0][j][rws], combined[1][j][rws])
            acc = acc + gexp[:, j * width + h * LANES: j * width + (h + 1) * LANES] * both
        o_ref[:, cols] = (acc * sz_ref[:, cols].astype(F32)).astype(BF16)


def _attn(q, kc, vc, ks0, ks1, vs0, vs1, kw, vw0, vw1, g3, sz, ovlt, eg, dbias, bsz, seq):
    tq, tk = ATTN_Q, ATTN_K
    nq = seq // tq
    nch = kc.shape[1]
    qrow = lambda n: pl.BlockSpec((tq, n), lambda b, i: (b * nq + i, 0))
    per_seq = pl.BlockSpec((seq, LANES), lambda b, i: (b, 0))
    cmp_spec = pl.BlockSpec((1, nch, LANES), lambda b, i: (b, 0, 0))
    full = lambda a: pl.BlockSpec(a.shape, lambda b, i: (0,) * a.ndim)
    diag = pl.BlockSpec((1, tq, tk), lambda b, i: (i % (tk // tq), 0, 0))
    return pl.pallas_call(
        functools.partial(_attn_kernel, seq=seq),
        grid=(bsz, nq),
        in_specs=[qrow(ATTN_WIDTH), cmp_spec, cmp_spec] + [per_seq] * 7
                 + [qrow(LANES), qrow(ATTN_WIDTH), full(ovlt), full(eg), diag],
        out_specs=qrow(ATTN_WIDTH),
        out_shape=jax.ShapeDtypeStruct((bsz * seq, ATTN_WIDTH), BF16),
        scratch_shapes=[pltpu.VMEM((HEADS_PER_GROUP * tq, LANES), F32)],
        compiler_params=pltpu.CompilerParams(vmem_limit_bytes=VMEM_LIMIT),
        name="attn",
    )(q, kc, vc, ks0, ks1, vs0, vs1, kw, vw0, vw1, g3, sz, ovlt, eg, dbias)


def _ssm_kernel(u_ref, mi_ref, ws_ref, wo_ref, al_ref, y_ref, sx_ref, *, n_chunks):
    u = u_ref[0]
    n_tiles = sx_ref.shape[0]
    nt = n_tiles // 2
    tile = lambda j: slice(j * LANES, (j + 1) * LANES)
    for j in range(n_tiles):
        sx_ref[j] = _dot(u, ws_ref[0, :, tile(j)])
    a_re = [al_ref[0, 0:1, tile(j)] for j in range(nt)]
    a_im = [al_ref[0, 1:2, tile(j)] for j in range(nt)]
    bt = SSM_BT

    def body(k, carry):
        rows = pl.ds(k, bt, stride=n_chunks)
        new = []
        for j in range(nt):
            x_re, x_im = carry[j], carry[nt + j]
            s_re = sx_ref[j, rows, :]
            s_im = sx_ref[nt + j, rows, :]
            sx_ref[j, rows, :] = x_re
            sx_ref[nt + j, rows, :] = x_im
            new.append((a_re[j] * x_re - a_im[j] * x_im + s_re, a_re[j] * x_im + a_im[j] * x_re + s_im))
        return tuple([n[0] for n in new] + [n[1] for n in new])

    zero = jnp.zeros((bt, LANES), F32)
    lax.fori_loop(0, n_chunks, body, (zero,) * n_tiles)
    y = _dot(u, mi_ref[0])
    for j in range(n_tiles):
        y = y + _dot(sx_ref[j].astype(BF16), wo_ref[0, tile(j), :])
    y_ref[0] = y


def _ssm(u3, mi, ws, wo, al, n_chunks):
    halves, rows, width = u3.shape
    r = SSM_BT * n_chunks
    wspec = lambda a: pl.BlockSpec((1,) + a.shape[1:], lambda hf, i: (hf, 0, 0))
    return pl.pallas_call(
        functools.partial(_ssm_kernel, n_chunks=n_chunks),
        grid=(halves, rows // r),
        in_specs=[pl.BlockSpec((1, r, width), lambda hf, i: (hf, i, 0)),
                  wspec(mi), wspec(ws), wspec(wo), wspec(al)],
        out_specs=pl.BlockSpec((1, r, width), lambda hf, i: (hf, i, 0)),
        out_shape=jax.ShapeDtypeStruct((halves, rows, width), F32),
        scratch_shapes=[pltpu.VMEM((ws.shape[2] // LANES, r, LANES), F32)],
        compiler_params=pltpu.CompilerParams(vmem_limit_bytes=VMEM_LIMIT),
        name="ssm",
    )(u3, mi, ws, wo, al)


def _ssm_params(lam_re, lam_im, log_dt, b_re, b_im, c_re, c_im, d_skip):
    L = SSM_CHUNK
    G, P, C = SSM_GROUPS, SSM_STATE, SSM_GROUP
    gh = G // SSM_HALVES
    dt = jnp.exp(log_dt)[:, None]
    lr, li = lam_re, lam_im
    mag = jnp.exp(lr * dt)
    ab_re = mag * jnp.cos(li * dt)
    ab_im = mag * jnp.sin(li * dt)
    nr, ni = ab_re - 1.0, ab_im
    den = lr * lr + li * li
    cr = ((nr * lr + ni * li) / den)[..., None]
    ci = ((ni * lr - nr * li) / den)[..., None]
    bb_re = cr * b_re - ci * b_im
    bb_im = cr * b_im + ci * b_re
    pr, pi = [jnp.ones_like(ab_re)], [jnp.zeros_like(ab_im)]
    for _ in range(L):
        pr.append(pr[-1] * ab_re - pi[-1] * ab_im)
        pi.append(pr[-2] * ab_im + pi[-1] * ab_re)
    pw_re = jnp.stack(pr)
    pw_im = jnp.stack(pi)
    ca_re = jnp.einsum('gcp,dgp->dgcp', c_re, pw_re[:L]) - jnp.einsum('gcp,dgp->dgcp', c_im, pw_im[:L])
    ca_im = jnp.einsum('gcp,dgp->dgcp', c_re, pw_im[:L]) + jnp.einsum('gcp,dgp->dgcp', c_im, pw_re[:L])
    kern = jnp.einsum('dgop,gpi->dgoi', ca_re, bb_re) - jnp.einsum('dgop,gpi->dgoi', ca_im, bb_im)
    lag = jnp.arange(L)[None, :] - jnp.arange(L)[:, None]
    kt = jnp.where((lag >= 0)[:, :, None, None, None], kern[jnp.clip(lag, 0, L - 1)], 0.0)
    eye_g = jnp.eye(G, dtype=F32)
    m_intra = jnp.einsum('stgoi,gh->sgitho', kt, eye_g)
    skip = jnp.einsum('st,gh,io,go->sgitho', jnp.eye(L, dtype=F32), eye_g, jnp.eye(C, dtype=F32),
                      d_skip.reshape(G, C))
    m_intra = m_intra + skip
    e_re, e_im = pw_re[:L][::-1], pw_im[:L][::-1]
    ws_re = jnp.einsum('sgp,gpi->sgip', e_re, bb_re) - jnp.einsum('sgp,gpi->sgip', e_im, bb_im)
    ws_im = jnp.einsum('sgp,gpi->sgip', e_re, bb_im) + jnp.einsum('sgp,gpi->sgip', e_im, bb_re)
    o_re, o_im = pw_re[1:L + 1], pw_im[1:L + 1]
    wo_re = jnp.einsum('gop,tgp->gpto', c_re, o_re) - jnp.einsum('gop,tgp->gpto', c_im, o_im)
    wo_im = -(jnp.einsum('gop,tgp->gpto', c_re, o_im) + jnp.einsum('gop,tgp->gpto', c_im, o_re))

    mi_h, ws_h, wo_h, al_h = [], [], [], []
    for hf in range(SSM_HALVES):
        gs = slice(hf * gh, (hf + 1) * gh)
        mi_h.append(m_intra[:, gs, :, :, gs, :].reshape(L * gh * C, L * gh * C))
        eye_h = jnp.eye(gh, dtype=F32)
        wsr = jnp.einsum('sgip,gh->sgihp', ws_re[:, gs], eye_h).reshape(L * gh * C, gh * P)
        wsi = jnp.einsum('sgip,gh->sgihp', ws_im[:, gs], eye_h).reshape(L * gh * C, gh * P)
        ws_h.append(jnp.concatenate([wsr, wsi], axis=1))
        wor = jnp.einsum('gpto,gh->gptho', wo_re[gs], eye_h).reshape(gh * P, L * gh * C)
        woi = jnp.einsum('gpto,gh->gptho', wo_im[gs], eye_h).reshape(gh * P, L * gh * C)
        wo_h.append(jnp.concatenate([wor, woi], axis=0))
        al_h.append(jnp.stack([pw_re[L, gs].reshape(gh * P), pw_im[L, gs].reshape(gh * P)]))
    return (jnp.stack(mi_h).astype(BF16), jnp.stack(ws_h).astype(BF16), jnp.stack(wo_h).astype(BF16),
            jnp.stack(al_h))


def _tail_kernel(x_ref, oa_ref, y_ref, szs_ref, sma_ref, sms_ref, gate_ref, wglu_ref, wpa_ref, wps_ref,
                 wout_ref, fg_ref, o_ref):
    y = jnp.concatenate([y_ref[0], y_ref[1]], axis=-1)
    yy = _dot(_gelu(y).astype(BF16), wglu_ref[...])
    o_ssm = yy[:, :SSM_WIDTH] * _sigmoid(yy[:, SSM_WIDTH:]) * szs_ref[...].astype(F32)
    merged = (sma_ref[...].astype(F32) * _dot(oa_ref[...], wpa_ref[...])
              + sms_ref[...].astype(F32) * _dot(o_ssm.astype(BF16), wps_ref[...]))
    xo = x_ref[...] + gate_ref[0] * _dot(merged.astype(BF16), wout_ref[...])
    ms = jnp.mean(xo * xo, axis=-1, keepdims=True)
    o_ref[...] = xo * lax.rsqrt(ms + NORM_EPS) * fg_ref[...]


def _tail(x2, oa, y3, szs, sma, sms, gate, wglu, wpa, wps, wout, fgain, seq):
    t, d = x2.shape
    tm = PROJ_ROWS
    per_seq = seq // tm
    row = lambda n: pl.BlockSpec((tm, n), lambda i: (i, 0))
    full = lambda a: pl.BlockSpec(a.shape, lambda i: (0,) * a.ndim)
    return pl.pallas_call(
        _tail_kernel,
        grid=(t // tm,),
        in_specs=[row(d), row(ATTN_WIDTH), pl.BlockSpec((SSM_HALVES, tm, LANES), lambda i: (0, i, 0)),
                  row(SSM_WIDTH), row(d), row(d),
                  pl.BlockSpec((1, 1, d), lambda i: (i // per_seq, 0, 0)),
                  full(wglu), full(wpa), full(wps), full(wout), full(fgain)],
        out_specs=row(d),
        out_shape=jax.ShapeDtypeStruct((t, d), F32),
        compiler_params=pltpu.CompilerParams(vmem_limit_bytes=VMEM_LIMIT),
        name="tail",
    )(x2, oa, y3, szs, sma, sms, gate, wglu, wpa, wps, wout, fgain)


def _head_perm():
    idx = np.arange(ATTN_WIDTH)
    h, rem = idx // LANES, idx % LANES
    g, dd = rem // HEAD_DIM, rem % HEAD_DIM
    return (g * HEADS_PER_GROUP + h) * HEAD_DIM + dd


def _rope_tables(seq):
    half = HEAD_DIM // 2
    inv_freq = ROPE_THETA ** (-jnp.arange(half, dtype=F32) / half)
    ang = jnp.arange(seq, dtype=F32)[:, None] * inv_freq[None, :]
    cos, sin = jnp.cos(ang), jnp.sin(ang)
    zero = jnp.zeros_like(sin)
    reps = LANES // HEAD_DIM
    cos_t = jnp.tile(jnp.concatenate([cos, cos], axis=1), (1, reps))
    sa_t = jnp.tile(jnp.concatenate([-sin, zero], axis=1), (1, reps))
    sb_t = jnp.tile(jnp.concatenate([zero, sin], axis=1), (1, reps))
    return cos_t, sa_t, sb_t


def _proj_weight(w_in):
    d = w_in.shape[0]
    o_q, o_kv, o_g, o_z, o_u, o_zs, o_m = 0, 512, 1280, 1304, 1816, 2072, 2328
    perm = _head_perm()
    kv = lambda j: w_in[:, o_kv + j * KV_WIDTH: o_kv + (j + 1) * KV_WIDTH]
    gates = jnp.pad(w_in[:, o_g:o_z], ((0, 0), (0, LANES - 3 * N_HEADS)))
    cols = [w_in[:, o_q:o_kv][:, perm] * (HEAD_DIM ** -0.5),
            kv(0), kv(2), kv(4), kv(1), kv(3), kv(5), gates,
            w_in[:, o_z:o_u][:, perm], w_in[:, o_u:o_zs], w_in[:, o_zs:o_m], w_in[:, o_m:]]
    w = jnp.concatenate(cols, axis=1)
    assert w.shape == (d, _C_END)
    return w.astype(BF16)


def _attn_constants(seq):
    n_cmp = LANES
    cs = np.arange(n_cmp) * CMP_STRIDE
    ss = np.arange(LANES) * SEL_BLOCK
    ovl = (np.minimum(cs[:, None] + CMP_BLOCK, ss[None, :] + SEL_BLOCK) > np.maximum(cs[:, None], ss[None, :]))
    ovl = ovl & (np.arange(LANES)[None, :] < seq // SEL_BLOCK) & (cs[:, None] + CMP_BLOCK <= seq)
    tq, tk = ATTN_Q, ATTN_K
    v = np.arange(tk // tq)[:, None, None]
    r = np.arange(tq)[None, :, None]
    cc = np.arange(tk)[None, None, :]
    dbias = np.where(cc <= v * tq + r, 0.0, MASK_VALUE).astype(np.float32)
    eg = np.zeros((LANES, 3 * ATTN_WIDTH), np.float32)
    for g in range(N_GROUPS):
        for h in range(HEADS_PER_GROUP):
            for j in range(3):
                base = j * ATTN_WIDTH + h * LANES + g * HEAD_DIM
                eg[(g * HEADS_PER_GROUP + h) * 3 + j, base:base + HEAD_DIM] = 1.0
    return (jnp.asarray(ovl.T, BF16), jnp.asarray(eg, BF16), jnp.asarray(dbias, F32))


def kernel(x, c, w_ada, b_ada, norm_gain, w_in, pe_cmp_k, w_cmp_k1, b_cmp_k1, w_cmp_k2, pe_cmp_v, w_cmp_v1,
           b_cmp_v1, w_cmp_v2, lam_re, lam_im, log_dt, b_re, b_im, c_re, c_im, d_skip, w_glu, w_proj_attn,
           w_proj_ssm, w_out, final_gain):
    bsz, seq, d = x.shape
    depth = w_in.shape[0]
    assert depth == 1, "the tail kernel fuses the final norm into the (single) layer"
    assert bsz % SSM_BT == 0 and seq % SSM_CHUNK == 0
    assert seq % (ATTN_K) == 0 and seq % PROJ_ROWS == 0 and seq >= WINDOW + ATTN_Q
    assert (seq - CMP_BLOCK) // CMP_STRIDE + 1 <= LANES and seq // CMP_STRIDE == LANES
    t = bsz * seq
    cos_t, sa_t, sb_t = _rope_tables(seq)
    ovlt, eg, dbias = _attn_constants(seq)
    perm = _head_perm()
    n_chunks = seq // SSM_CHUNK

    x2 = x.reshape(t, d)
    for l in range(depth):
        mod = _ada(c, w_ada[l], b_ada[l])
        shift, mscale, gate = [m.reshape(bsz, 1, d) for m in jnp.split(mod, 3, axis=-1)]
        (q, kc_r, ks0, ks1, kw, vc_r, vs0, vs1, vw0, vw1, g3, sz, u, szs, sma, sms) = _proj(
            x2, norm_gain[l].reshape(1, d), mscale, shift, cos_t, sa_t, sb_t, _proj_weight(w_in[l]), seq)

        nch = seq // CMP_STRIDE
        kc, vc = _compress(
            kc_r.reshape(bsz, nch, CMP_STRIDE * KV_WIDTH), vc_r.reshape(bsz, nch, CMP_STRIDE * KV_WIDTH),
            _compress_params(pe_cmp_k[l], w_cmp_k1[l], b_cmp_k1[l], w_cmp_k2[l]),
            _compress_params(pe_cmp_v[l], w_cmp_v1[l], b_cmp_v1[l], w_cmp_v2[l]))
        o_attn = _attn(q, kc, vc, ks0, ks1, vs0, vs1, kw, vw0, vw1, g3, sz, ovlt, eg, dbias, bsz, seq)

        mi, ws, wo, al = _ssm_params(lam_re[l], lam_im[l], log_dt[l], b_re[l], b_im[l], c_re[l], c_im[l],
                                     d_skip[l])
        y3 = _ssm(u.reshape(SSM_HALVES, t // SSM_CHUNK, SSM_CHUNK * LANES), mi, ws, wo, al, n_chunks)
        y3 = y3.reshape(SSM_HALVES, t, LANES)

        x2 = _tail(x2, o_attn, y3, szs, sma, sms, gate, w_glu[l].astype(BF16),
                   w_proj_attn[l][perm].astype(BF16), w_proj_ssm[l].astype(BF16), w_out[l].astype(BF16),
                   final_gain.reshape(1, d) if l == depth - 1 else jnp.ones((1, d), F32), seq)
    return x2.reshape(bsz, seq, d)
```

```python
import functools
import math

import jax
import jax.numpy as jnp
import numpy as np
from jax import lax
from jax.experimental import pallas as pl
from jax.experimental.pallas import tpu as pltpu

F32 = jnp.float32
BF16 = jnp.bfloat16

N_HEADS = 8
N_GROUPS = 2
HEADS_PER_GROUP = N_HEADS // N_GROUPS
HEAD_DIM = 64
ATTN_WIDTH = N_HEADS * HEAD_DIM
KV_WIDTH = N_GROUPS * HEAD_DIM
CMP_BLOCK = 32
CMP_STRIDE = 16
CMP_HIDDEN = 2 * HEAD_DIM
SEL_BLOCK = 64
N_SEL = 16
WINDOW = 512
ROPE_THETA = 10000.0
FORCE_SCORE = 1.0e4
MASK_VALUE = -1.0e30
SSM_WIDTH = 256
SSM_GROUP = 16
SSM_GROUPS = SSM_WIDTH // SSM_GROUP
SSM_STATE = 64
NORM_EPS = 1e-6

LANES = 128
VMEM_LIMIT = 56 * 1024 * 1024

PROJ_ROWS = 512
ATTN_Q = 128
ATTN_K = 512
SSM_CHUNK = 8
SSM_HALVES = 2
SSM_BT = 8


def _dot(a, b):
    return jnp.dot(a, b, preferred_element_type=F32)


def _dot_nt(a, b):
    return lax.dot_general(a, b, (((1,), (1,)), ((), ())), preferred_element_type=F32)


def _dot_hilo(a, b_bf16):
    hi = a.astype(BF16)
    lo = (a - hi.astype(F32)).astype(BF16)
    return _dot(hi, b_bf16) + _dot(lo, b_bf16)


def _sigmoid(x):
    return 1.0 / (1.0 + jnp.exp(-x))


def _silu(x):
    return x * _sigmoid(x)


def _gelu(x):
    return 0.5 * x * (1.0 + jnp.tanh(math.sqrt(2.0 / math.pi) * (x + 0.044715 * (x * x * x))))


def _ada_kernel(c_ref, w_ref, b_ref, o_ref):
    c = c_ref[...]
    o_ref[...] = _dot(_silu(c).astype(BF16), w_ref[...].astype(BF16)) + b_ref[...]


def _ada(c, w, b):
    bsz, d = c.shape
    n = w.shape[1]
    return pl.pallas_call(
        _ada_kernel,
        grid=(n // d,),
        in_specs=[pl.BlockSpec((bsz, d), lambda j: (0, 0)),
                  pl.BlockSpec((d, d), lambda j: (0, j)),
                  pl.BlockSpec((1, d), lambda j: (0, j))],
        out_specs=pl.BlockSpec((bsz, d), lambda j: (0, j)),
        out_shape=jax.ShapeDtypeStruct((bsz, n), F32),
        compiler_params=pltpu.CompilerParams(vmem_limit_bytes=VMEM_LIMIT),
        name="ada",
    )(c, w, b.reshape(1, n))


_C_Q = 0
_C_KC = 512
_C_KS = 640
_C_KW = 768
_C_VC = 896
_C_VS = 1024
_C_VW = 1152
_C_G3 = 1280
_C_Z = 1408
_C_U = 1920
_C_ZS = 2176
_C_MA = 2432
_C_MS = 3456
_C_END = 4480


def _proj_kernel(x_ref, gain_ref, scale_ref, shift_ref, cos_ref, sa_ref, sb_ref, w_ref,
                 q_ref, kc_ref, ks0_ref, ks1_ref, kw_ref, vc_ref, vs0_ref, vs1_ref, vw0_ref, vw1_ref,
                 g3_ref, sz_ref, u_ref, szs_ref, sma_ref, sms_ref, stage_ref, *, per_seq):
    x = x_ref[...]
    ms = jnp.mean(x * x, axis=-1, keepdims=True)
    h = x * lax.rsqrt(ms + NORM_EPS) * gain_ref[...]
    h = h * (1.0 + scale_ref[0]) + shift_ref[0]
    hb = h.astype(BF16)

    def proj(a, b):
        return _dot(hb, w_ref[:, a:b])

    cos = cos_ref[...]
    sa = sa_ref[...]
    sb = sb_ref[...]

    def rope(t):
        return t * cos + pltpu.roll(t, LANES - 32, 1) * sa + pltpu.roll(t, 32, 1) * sb

    narrow = proj(_C_Q, _C_Z)
    part = lambda a: narrow[:, a:a + LANES]
    for v in range(ATTN_WIDTH // LANES):
        q_ref[:, v * LANES:(v + 1) * LANES] = rope(part(_C_Q + v * LANES)).astype(BF16)
    kw_ref[...] = rope(part(_C_KW)).astype(BF16)
    tm = x.shape[0]
    for out_ref, val in ((kc_ref, rope(part(_C_KC))), (vc_ref, part(_C_VC))):
        stage_ref[...] = val
        for j in range(CMP_STRIDE):
            out_ref[:, j * LANES:(j + 1) * LANES] = stage_ref[pl.ds(j, tm // CMP_STRIDE, stride=CMP_STRIDE), :]
    lane = lax.broadcasted_iota(jnp.int32, (tm, LANES), 1)
    pos = (pl.program_id(0) % per_seq) * tm + lax.broadcasted_iota(jnp.int32, (tm, LANES), 0)
    blk = pos >> int(math.log2(SEL_BLOCK))
    first = lane < HEAD_DIM
    ks = rope(part(_C_KS))
    ks0_ref[...] = jnp.where(first, ks, jnp.where(lane - HEAD_DIM == blk, 1.0, 0.0)).astype(BF16)
    ks1_ref[...] = jnp.where(first, jnp.where(lane == blk, 1.0, 0.0), ks).astype(BF16)
    vs = part(_C_VS)
    vs0_ref[...] = jnp.where(first, vs, 1.0).astype(BF16)
    vs1_ref[...] = jnp.where(first, 1.0, vs).astype(BF16)
    vw = part(_C_VW)
    vw0_ref[...] = jnp.where(first, vw, 1.0).astype(BF16)
    vw1_ref[...] = jnp.where(first, 1.0, vw).astype(BF16)
    g3_ref[...] = _sigmoid(part(_C_G3))
    sz_ref[...] = _silu(proj(_C_Z, _C_U)).astype(BF16)
    u = proj(_C_U, _C_ZS)
    for hf in range(SSM_HALVES):
        stage_ref[...] = u[:, hf * LANES:(hf + 1) * LANES]
        for j in range(SSM_CHUNK):
            u_ref[hf, :, j * LANES:(j + 1) * LANES] = stage_ref[
                pl.ds(j, tm // SSM_CHUNK, stride=SSM_CHUNK), :].astype(BF16)
    szs_ref[...] = _silu(proj(_C_ZS, _C_MA)).astype(BF16)
    sma_ref[...] = _sigmoid(proj(_C_MA, _C_MS)).astype(BF16)
    sms_ref[...] = _sigmoid(proj(_C_MS, _C_END)).astype(BF16)


def _proj(x2, gain, scale, shift, cos_t, sa_t, sb_t, w, seq):
    t, d = x2.shape
    tm = PROJ_ROWS
    per_seq = seq // tm
    row = lambda n: pl.BlockSpec((tm, n), lambda i: (i, 0))
    mod = pl.BlockSpec((1, 1, d), lambda i: (i // per_seq, 0, 0))
    tab = pl.BlockSpec((tm, LANES), lambda i: (i % per_seq, 0))
    cmp_w, ssm_w = CMP_STRIDE * LANES, SSM_CHUNK * LANES
    out_shape = [
        jax.ShapeDtypeStruct((t, ATTN_WIDTH), BF16),
        jax.ShapeDtypeStruct((t // CMP_STRIDE, cmp_w), F32),
        jax.ShapeDtypeStruct((t, LANES), BF16),
        jax.ShapeDtypeStruct((t, LANES), BF16),
        jax.ShapeDtypeStruct((t, LANES), BF16),
        jax.ShapeDtypeStruct((t // CMP_STRIDE, cmp_w), F32),
        jax.ShapeDtypeStruct((t, LANES), BF16),
        jax.ShapeDtypeStruct((t, LANES), BF16),
        jax.ShapeDtypeStruct((t, LANES), BF16),
        jax.ShapeDtypeStruct((t, LANES), BF16),
        jax.ShapeDtypeStruct((t, LANES), F32),
        jax.ShapeDtypeStruct((t, ATTN_WIDTH), BF16),
        jax.ShapeDtypeStruct((SSM_HALVES, t // SSM_CHUNK, ssm_w), BF16),
        jax.ShapeDtypeStruct((t, SSM_WIDTH), BF16),
        jax.ShapeDtypeStruct((t, d), BF16),
        jax.ShapeDtypeStruct((t, d), BF16),
    ]
    cmp_rows = pl.BlockSpec((tm // CMP_STRIDE, cmp_w), lambda i: (i, 0))
    out_specs = [row(ATTN_WIDTH), cmp_rows, row(LANES), row(LANES), row(LANES), cmp_rows] + [row(LANES)] * 5 + [
                 row(ATTN_WIDTH),
                 pl.BlockSpec((SSM_HALVES, tm // SSM_CHUNK, ssm_w), lambda i: (0, i, 0)),
                 row(SSM_WIDTH), row(d), row(d)]
    return pl.pallas_call(
        functools.partial(_proj_kernel, per_seq=per_seq),
        grid=(t // tm,),
        in_specs=[row(d), pl.BlockSpec((1, d), lambda i: (0, 0)), mod, mod, tab, tab, tab,
                  pl.BlockSpec((d, _C_END), lambda i: (0, 0))],
        out_specs=out_specs,
        out_shape=out_shape,
        scratch_shapes=[pltpu.VMEM((tm, LANES), F32)],
        compiler_params=pltpu.CompilerParams(vmem_limit_bytes=VMEM_LIMIT),
        name="proj",
    )(x2, gain, scale, shift, cos_t, sa_t, sb_t, w)


def _compress_kernel(rk_ref, rv_ref, pea_k, peb_k, wa_k, wb_k, b1_k, w2_k,
                     pea_v, peb_v, wa_v, wb_v, b1_v, w2_v, kc_ref, vc_ref):
    def one(r_ref, pea, peb, wa, wb, b1, w2):
        r = r_ref[0]
        first = _dot((r + pea[...]).astype(BF16), wa[...])
        second = _dot((r + peb[...]).astype(BF16), wb[...])
        nxt = pltpu.roll(second, second.shape[0] - 1, 0)
        hid = _gelu(first + nxt + b1[...])
        return _dot(hid.astype(BF16), w2[...])

    kc_ref[0] = one(rk_ref, pea_k, peb_k, wa_k, wb_k, b1_k, w2_k)
    vc_ref[0] = one(rv_ref, pea_v, peb_v, wa_v, wb_v, b1_v, w2_v)


def _compress(rk, rv, pk, pv):
    bsz, nch, width = rk.shape
    full = lambda a: pl.BlockSpec(a.shape, lambda b: (0,) * a.ndim)
    rspec = pl.BlockSpec((1, nch, width), lambda b: (b, 0, 0))
    ospec = pl.BlockSpec((1, nch, LANES), lambda b: (b, 0, 0))
    return pl.pallas_call(
        _compress_kernel,
        grid=(bsz,),
        in_specs=[rspec, rspec] + [full(a) for a in pk] + [full(a) for a in pv],
        out_specs=[ospec, ospec],
        out_shape=[jax.ShapeDtypeStruct((bsz, nch, LANES), F32)] * 2,
        compiler_params=pltpu.CompilerParams(vmem_limit_bytes=VMEM_LIMIT),
        name="compress",
    )(rk, rv, *pk, *pv)


def _compress_params(pe, w1, b1, w2):
    half = CMP_BLOCK // 2
    eye = jnp.eye(N_GROUPS, dtype=F32)
    w1r = w1.reshape(CMP_BLOCK, HEAD_DIM, CMP_HIDDEN)

    def expand(wl):
        return jnp.einsum('ldj,gh->lgdhj', wl, eye).reshape(half * KV_WIDTH, N_GROUPS * CMP_HIDDEN)

    def pe_lanes(p):
        return jnp.broadcast_to(p[:, None, :], (half, N_GROUPS, HEAD_DIM)).reshape(1, half * KV_WIDTH)

    w2e = jnp.einsum('jd,gh->gjhd', w2, eye).reshape(N_GROUPS * CMP_HIDDEN, KV_WIDTH)
    return (pe_lanes(pe[:half]), pe_lanes(pe[half:]),
            expand(w1r[:half]).astype(BF16), expand(w1r[half:]).astype(BF16),
            jnp.tile(b1, N_GROUPS).reshape(1, N_GROUPS * CMP_HIDDEN), w2e.astype(BF16))


def _attn_kernel(q_ref, kc_ref, vc_ref, ks0_ref, ks1_ref, vs0_ref, vs1_ref, kw_ref, vw0_ref, vw1_ref,
                 g3_ref, sz_ref, ovlt_ref, eg_ref, dbias_ref, o_ref, slc_ref, *, seq):
    tq, tk = ATTN_Q, ATTN_K
    hg = HEADS_PER_GROUP
    rows = hg * tq
    n_sel_blocks = seq // SEL_BLOCK
    sel_shift = int(math.log2(SEL_BLOCK))
    q0 = pl.program_id(1) * tq
    lane = lax.broadcasted_iota(jnp.int32, (tq, LANES), 1)
    t_row = q0 + lax.broadcasted_iota(jnp.int32, (tq, LANES), 0)
    cvalid = (lane * CMP_STRIDE + (CMP_BLOCK - 1)) <= t_row

    span = WINDOW + tq
    start = pl.multiple_of(jnp.maximum(q0 - WINDOW, 0), tq)
    kp = start + lax.broadcasted_iota(jnp.int32, (tq, span), 1)
    tt = q0 + lax.broadcasted_iota(jnp.int32, (tq, span), 0)
    wbias = jnp.where((kp <= tt) & (kp > tt - WINDOW), 0.0, MASK_VALUE)

    jidx = lax.broadcasted_iota(jnp.int32, (n_sel_blocks, tq), 0)
    t_lane = q0 + lax.broadcasted_iota(jnp.int32, (n_sel_blocks, tq), 1)
    forced = (jidx == 0) | (jidx == (t_lane >> sel_shift))
    future = jidx * SEL_BLOCK > t_lane

    gexp = _dot_hilo(g3_ref[...], eg_ref[...])
    width = ATTN_WIDTH
    ks_refs, vs_refs, vw_refs = (ks0_ref, ks1_ref), (vs0_ref, vs1_ref), (vw0_ref, vw1_ref)

    groups = range(N_GROUPS)
    in_group = [(lane >= HEAD_DIM * g) & (lane < HEAD_DIM * (g + 1)) for g in groups]
    zero = jnp.zeros((tq, LANES), BF16)
    q_heads = [q_ref[:, h * LANES:(h + 1) * LANES] for h in range(hg)]
    qg = [jnp.concatenate([jnp.where(in_group[g], qh, zero) for qh in q_heads], axis=0) for g in groups]

    kcb = kc_ref[0].astype(BF16)
    vcb = vc_ref[0].astype(BF16)
    sc = [_dot_nt(qg[g], kcb).reshape(hg, tq, LANES) for g in groups]
    s3 = [_dot_nt(qg[g], kw_ref[pl.ds(start, span), :]).reshape(hg, tq, span) for g in groups]

    p_cmp = []
    for g in groups:
        s = jnp.where(cvalid[None], sc[g], MASK_VALUE)
        m = jnp.max(s, axis=-1, keepdims=True)
        e = jnp.where(cvalid[None], jnp.exp(s - m), 0.0)
        den = jnp.sum(e, axis=-1, keepdims=True)
        p_cmp.append(e * (1.0 / jnp.where(den > 0.0, den, 1.0)))
    o_cmp = [_dot(p_cmp[g].reshape(rows, LANES).astype(BF16), vcb) for g in groups]

    imp = []
    for g in groups:
        psum = p_cmp[g][0]
        for h in range(1, hg):
            psum = psum + p_cmp[g][h]
        p_hi = psum.astype(BF16)
        p_lo = (psum - p_hi.astype(F32)).astype(BF16)
        imp.append((_dot_nt(ovlt_ref[...], p_hi) + _dot_nt(ovlt_ref[...], p_lo))[0:n_sel_blocks])

    o_win = []
    for g in groups:
        sw = s3[g] + wbias[None]
        m3 = jnp.max(sw, axis=-1, keepdims=True)
        p3 = jnp.exp(sw - m3).astype(BF16).reshape(rows, span)
        ow = _dot(p3, vw_refs[g][pl.ds(start, span), :])
        o_win.append(ow * (1.0 / pltpu.roll(ow, HEAD_DIM, 1)))

    qsel = []
    for g in groups:
        imp_g = jnp.where(forced, FORCE_SCORE, jnp.where(future, -1.0, imp[g]))
        cnt = jnp.zeros((n_sel_blocks, tq), jnp.int32)
        for i in range(n_sel_blocks):
            other = imp_g[i:i + 1, :]
            ahead = (other > imp_g) | ((other == imp_g) & (jidx > i))
            cnt = cnt + ahead.astype(jnp.int32)
        mask_t = jnp.where(cnt < min(N_SEL, n_sel_blocks), 0.0, MASK_VALUE)
        lo_rows = HEAD_DIM * (1 - g)
        parts = [mask_t, jnp.zeros((LANES - n_sel_blocks - lo_rows, tq), F32)]
        if lo_rows:
            parts = [jnp.zeros((lo_rows, tq), F32)] + parts
        maskq = jnp.concatenate(parts, axis=0).T.astype(BF16)
        qsel.append(jnp.concatenate([jnp.where(in_group[g], qh, maskq) for qh in q_heads], axis=0))
    partial = [(o_cmp[g], o_win[g]) for g in groups]

    n_tiles = q0 // tk + 1
    for n in range(1, seq // tk + 1):
        @pl.when(n_tiles == n)
        def _(n=n):
            klen = n * tk
            s2 = [_dot_nt(qsel[g], ks_refs[g][0:klen, :]) for g in range(N_GROUPS)]
            for g in range(N_GROUPS):
                last = (s2[g][:, klen - tk:].reshape(hg, tq, tk) + dbias_ref[0][None]).reshape(rows, tk)
                sg = last if n == 1 else jnp.concatenate([s2[g][:, :klen - tk], last], axis=1)
                m2 = jnp.max(sg, axis=-1, keepdims=True)
                p2 = jnp.exp(sg - m2).astype(BF16)
                slc_ref[g] = _dot(p2, vs_refs[g][0:klen, :])

    combined = []
    for g in range(N_GROUPS):
        ov = slc_ref[g]
        o_slc = ov * (1.0 / pltpu.roll(ov, HEAD_DIM, 1))
        combined.append((partial[g][0], o_slc, partial[g][1]))

    first_half = lane < HEAD_DIM
    for h in range(hg):
        rws = slice(h * tq, (h + 1) * tq)
        cols = slice(h * LANES, (h + 1) * LANES)
        acc = jnp.zeros((tq, LANES), F32)
        for j in range(3):
            both = jnp.where(first_half, combined[0][j][rws], combined[1][j][rws])
            acc = acc + gexp[:, j * width + h * LANES: j * width + (h + 1) * LANES] * both
        o_ref[:, cols] = (acc * sz_ref[:, cols].astype(F32)).astype(BF16)


def _attn(q, kc, vc, ks0, ks1, vs0, vs1, kw, vw0, vw1, g3, sz, ovlt, eg, dbias, bsz, seq):
    tq, tk = ATTN_Q, ATTN_K
    nq = seq // tq
    nch = kc.shape[1]
    qrow = lambda n: pl.BlockSpec((tq, n), lambda b, i: (b * nq + i, 0))
    per_seq = pl.BlockSpec((seq, LANES), lambda b, i: (b, 0))
    cmp_spec = pl.BlockSpec((1, nch, LANES), lambda b, i: (b, 0, 0))
    full = lambda a: pl.BlockSpec(a.shape, lambda b, i: (0,) * a.ndim)
    diag = pl.BlockSpec((1, tq, tk), lambda b, i: (i % (tk // tq), 0, 0))
    return pl.pallas_call(
        functools.partial(_attn_kernel, seq=seq),
        grid=(bsz, nq),
        in_specs=[qrow(ATTN_WIDTH), cmp_spec, cmp_spec] + [per_seq] * 7
                 + [qrow(LANES), qrow(ATTN_WIDTH), full(ovlt), full(eg), diag],
        out_specs=qrow(ATTN_WIDTH),
        out_shape=jax.ShapeDtypeStruct((bsz * seq, ATTN_WIDTH), BF16),
        scratch_shapes=[pltpu.VMEM((N_GROUPS, HEADS_PER_GROUP * tq, LANES), F32)],
        compiler_params=pltpu.CompilerParams(vmem_limit_bytes=VMEM_LIMIT),
        name="attn",
    )(q, kc, vc, ks0, ks1, vs0, vs1, kw, vw0, vw1, g3, sz, ovlt, eg, dbias)


def _ssm_kernel(u_ref, mi_ref, ws_ref, wo_ref, al_ref, y_ref, sx_ref, *, n_chunks):
    u = u_ref[0]
    n_tiles = sx_ref.shape[0]
    nt = n_tiles // 2
    tile = lambda j: slice(j * LANES, (j + 1) * LANES)
    for j in range(n_tiles):
        sx_ref[j] = _dot(u, ws_ref[0, :, tile(j)])
    a_re = [al_ref[0, 0:1, tile(j)] for j in range(nt)]
    a_im = [al_ref[0, 1:2, tile(j)] for j in range(nt)]
    bt = SSM_BT

    def body(k, carry):
        rows = pl.ds(k, bt, stride=n_chunks)
        new = []
        for j in range(nt):
            x_re, x_im = carry[j], carry[nt + j]
            s_re = sx_ref[j, rows, :]
            s_im = sx_ref[nt + j, rows, :]
            sx_ref[j, rows, :] = x_re
            sx_ref[nt + j, rows, :] = x_im
            new.append((a_re[j] * x_re - a_im[j] * x_im + s_re, a_re[j] * x_im + a_im[j] * x_re + s_im))
        return tuple([n[0] for n in new] + [n[1] for n in new])

    zero = jnp.zeros((bt, LANES), F32)
    lax.fori_loop(0, n_chunks, body, (zero,) * n_tiles)
    y = _dot(u, mi_ref[0])
    for j in range(n_tiles):
        y = y + _dot(sx_ref[j].astype(BF16), wo_ref[0, tile(j), :])
    for j in range(SSM_CHUNK):
        y_ref[0, pl.ds(j, y.shape[0], stride=SSM_CHUNK), :] = y[:, tile(j)]


def _ssm(u3, mi, ws, wo, al, n_chunks):
    halves, rows, width = u3.shape
    r = SSM_BT * n_chunks
    wspec = lambda a: pl.BlockSpec((1,) + a.shape[1:], lambda hf, i: (hf, 0, 0))
    return pl.pallas_call(
        functools.partial(_ssm_kernel, n_chunks=n_chunks),
        grid=(halves, rows // r),
        in_specs=[pl.BlockSpec((1, r, width), lambda hf, i: (hf, i, 0)),
                  wspec(mi), wspec(ws), wspec(wo), wspec(al)],
        out_specs=pl.BlockSpec((1, r * SSM_CHUNK, LANES), lambda hf, i: (hf, i, 0)),
        out_shape=jax.ShapeDtypeStruct((halves, rows * SSM_CHUNK, LANES), F32),
        scratch_shapes=[pltpu.VMEM((ws.shape[2] // LANES, r, LANES), F32)],
        compiler_params=pltpu.CompilerParams(vmem_limit_bytes=VMEM_LIMIT),
        name="ssm",
    )(u3, mi, ws, wo, al)


def _ssm_params(lam_re, lam_im, log_dt, b_re, b_im, c_re, c_im, d_skip):
    L = SSM_CHUNK
    G, P, C = SSM_GROUPS, SSM_STATE, SSM_GROUP
    gh = G // SSM_HALVES
    dt = jnp.exp(log_dt)[:, None]
    lr, li = lam_re, lam_im
    mag = jnp.exp(lr * dt)
    ab_re = mag * jnp.cos(li * dt)
    ab_im = mag * jnp.sin(li * dt)
    nr, ni = ab_re - 1.0, ab_im
    den = lr * lr + li * li
    cr = ((nr * lr + ni * li) / den)[..., None]
    ci = ((ni * lr - nr * li) / den)[..., None]
    bb_re = cr * b_re - ci * b_im
    bb_im = cr * b_im + ci * b_re
    pr, pi = [jnp.ones_like(ab_re)], [jnp.zeros_like(ab_im)]
    for _ in range(L):
        pr.append(pr[-1] * ab_re - pi[-1] * ab_im)
        pi.append(pr[-2] * ab_im + pi[-1] * ab_re)
    pw_re = jnp.stack(pr)
    pw_im = jnp.stack(pi)
    ca_re = jnp.einsum('gcp,dgp->dgcp', c_re, pw_re[:L]) - jnp.einsum('gcp,dgp->dgcp', c_im, pw_im[:L])
    ca_im = jnp.einsum('gcp,dgp->dgcp', c_re, pw_im[:L]) + jnp.einsum('gcp,dgp->dgcp', c_im, pw_re[:L])
    kern = jnp.einsum('dgop,gpi->dgoi', ca_re, bb_re) - jnp.einsum('dgop,gpi->dgoi', ca_im, bb_im)
    lag = jnp.arange(L)[None, :] - jnp.arange(L)[:, None]
    kt = jnp.where((lag >= 0)[:, :, None, None, None], kern[jnp.clip(lag, 0, L - 1)], 0.0)
    eye_g = jnp.eye(G, dtype=F32)
    m_intra = jnp.einsum('stgoi,gh->sgitho', kt, eye_g)
    skip = jnp.einsum('st,gh,io,go->sgitho', jnp.eye(L, dtype=F32), eye_g, jnp.eye(C, dtype=F32),
                      d_skip.reshape(G, C))
    m_intra = m_intra + skip
    e_re, e_im = pw_re[:L][::-1], pw_im[:L][::-1]
    ws_re = jnp.einsum('sgp,gpi->sgip', e_re, bb_re) - jnp.einsum('sgp,gpi->sgip', e_im, bb_im)
    ws_im = jnp.einsum('sgp,gpi->sgip', e_re, bb_im) + jnp.einsum('sgp,gpi->sgip', e_im, bb_re)
    o_re, o_im = pw_re[1:L + 1], pw_im[1:L + 1]
    wo_re = jnp.einsum('gop,tgp->gpto', c_re, o_re) - jnp.einsum('gop,tgp->gpto', c_im, o_im)
    wo_im = -(jnp.einsum('gop,tgp->gpto', c_re, o_im) + jnp.einsum('gop,tgp->gpto', c_im, o_re))

    mi_h, ws_h, wo_h, al_h = [], [], [], []
    for hf in range(SSM_HALVES):
        gs = slice(hf * gh, (hf + 1) * gh)
        mi_h.append(m_intra[:, gs, :, :, gs, :].reshape(L * gh * C, L * gh * C))
        eye_h = jnp.eye(gh, dtype=F32)
        wsr = jnp.einsum('sgip,gh->sgihp', ws_re[:, gs], eye_h).reshape(L * gh * C, gh * P)
        wsi = jnp.einsum('sgip,gh->sgihp', ws_im[:, gs], eye_h).reshape(L * gh * C, gh * P)
        ws_h.append(jnp.concatenate([wsr, wsi], axis=1))
        wor = jnp.einsum('gpto,gh->gptho', wo_re[gs], eye_h).reshape(gh * P, L * gh * C)
        woi = jnp.einsum('gpto,gh->gptho', wo_im[gs], eye_h).reshape(gh * P, L * gh * C)
        wo_h.append(jnp.concatenate([wor, woi], axis=0))
        al_h.append(jnp.stack([pw_re[L, gs].reshape(gh * P), pw_im[L, gs].reshape(gh * P)]))
    return (jnp.stack(mi_h).astype(BF16), jnp.stack(ws_h).astype(BF16), jnp.stack(wo_h).astype(BF16),
            jnp.stack(al_h))


def _tail_kernel(x_ref, oa_ref, y_ref, szs_ref, sma_ref, sms_ref, gate_ref, wglu_ref, wpa_ref, wps_ref,
                 wout_ref, fg_ref, o_ref):
    y = jnp.concatenate([y_ref[0], y_ref[1]], axis=-1)
    yy = _dot(_gelu(y).astype(BF16), wglu_ref[...])
    o_ssm = yy[:, :SSM_WIDTH] * _sigmoid(yy[:, SSM_WIDTH:]) * szs_ref[...].astype(F32)
    merged = (sma_ref[...].astype(F32) * _dot(oa_ref[...], wpa_ref[...])
              + sms_ref[...].astype(F32) * _dot(o_ssm.astype(BF16), wps_ref[...]))
    xo = x_ref[...] + gate_ref[0] * _dot(merged.astype(BF16), wout_ref[...])
    ms = jnp.mean(xo * xo, axis=-1, keepdims=True)
    o_ref[...] = xo * lax.rsqrt(ms + NORM_EPS) * fg_ref[...]


def _tail(x2, oa, y3, szs, sma, sms, gate, wglu, wpa, wps, wout, fgain, seq):
    t, d = x2.shape
    tm = PROJ_ROWS
    per_seq = seq // tm
    row = lambda n: pl.BlockSpec((tm, n), lambda i: (i, 0))
    full = lambda a: pl.BlockSpec(a.shape, lambda i: (0,) * a.ndim)
    return pl.pallas_call(
        _tail_kernel,
        grid=(t // tm,),
        in_specs=[row(d), row(ATTN_WIDTH), pl.BlockSpec((SSM_HALVES, tm, LANES), lambda i: (0, i, 0)),
                  row(SSM_WIDTH), row(d), row(d),
                  pl.BlockSpec((1, 1, d), lambda i: (i // per_seq, 0, 0)),
                  full(wglu), full(wpa), full(wps), full(wout), full(fgain)],
        out_specs=row(d),
        out_shape=jax.ShapeDtypeStruct((t, d), F32),
        compiler_params=pltpu.CompilerParams(vmem_limit_bytes=VMEM_LIMIT),
        name="tail",
    )(x2, oa, y3, szs, sma, sms, gate, wglu, wpa, wps, wout, fgain)


def _head_perm():
    idx = np.arange(ATTN_WIDTH)
    h, rem = idx // LANES, idx % LANES
    g, dd = rem // HEAD_DIM, rem % HEAD_DIM
    return (g * HEADS_PER_GROUP + h) * HEAD_DIM + dd


def _rope_tables(seq):
    half = HEAD_DIM // 2
    inv_freq = ROPE_THETA ** (-jnp.arange(half, dtype=F32) / half)
    ang = jnp.arange(seq, dtype=F32)[:, None] * inv_freq[None, :]
    cos, sin = jnp.cos(ang), jnp.sin(ang)
    zero = jnp.zeros_like(sin)
    reps = LANES // HEAD_DIM
    cos_t = jnp.tile(jnp.concatenate([cos, cos], axis=1), (1, reps))
    sa_t = jnp.tile(jnp.concatenate([-sin, zero], axis=1), (1, reps))
    sb_t = jnp.tile(jnp.concatenate([zero, sin], axis=1), (1, reps))
    return cos_t, sa_t, sb_t


def _proj_weight(w_in):
    d = w_in.shape[0]
    o_q, o_kv, o_g, o_z, o_u, o_zs, o_m = 0, 512, 1280, 1304, 1816, 2072, 2328
    perm = _head_perm()
    kv = lambda j: w_in[:, o_kv + j * KV_WIDTH: o_kv + (j + 1) * KV_WIDTH]
    gates = jnp.pad(w_in[:, o_g:o_z], ((0, 0), (0, LANES - 3 * N_HEADS)))
    cols = [w_in[:, o_q:o_kv][:, perm] * (HEAD_DIM ** -0.5),
            kv(0), kv(2), kv(4), kv(1), kv(3), kv(5), gates,
            w_in[:, o_z:o_u][:, perm], w_in[:, o_u:o_zs], w_in[:, o_zs:o_m], w_in[:, o_m:]]
    w = jnp.concatenate(cols, axis=1)
    assert w.shape == (d, _C_END)
    return w.astype(BF16)


def _attn_constants(seq):
    n_cmp = LANES
    cs = np.arange(n_cmp) * CMP_STRIDE
    ss = np.arange(LANES) * SEL_BLOCK
    ovl = (np.minimum(cs[:, None] + CMP_BLOCK, ss[None, :] + SEL_BLOCK) > np.maximum(cs[:, None], ss[None, :]))
    ovl = ovl & (np.arange(LANES)[None, :] < seq // SEL_BLOCK) & (cs[:, None] + CMP_BLOCK <= seq)
    tq, tk = ATTN_Q, ATTN_K
    v = np.arange(tk // tq)[:, None, None]
    r = np.arange(tq)[None, :, None]
    cc = np.arange(tk)[None, None, :]
    dbias = np.where(cc <= v * tq + r, 0.0, MASK_VALUE).astype(np.float32)
    eg = np.zeros((LANES, 3 * ATTN_WIDTH), np.float32)
    for g in range(N_GROUPS):
        for h in range(HEADS_PER_GROUP):
            for j in range(3):
                base = j * ATTN_WIDTH + h * LANES + g * HEAD_DIM
                eg[(g * HEADS_PER_GROUP + h) * 3 + j, base:base + HEAD_DIM] = 1.0
    return (jnp.asarray(ovl.T, BF16), jnp.asarray(eg, BF16), jnp.asarray(dbias, F32))


def kernel(x, c, w_ada, b_ada, norm_gain, w_in, pe_cmp_k, w_cmp_k1, b_cmp_k1, w_cmp_k2, pe_cmp_v, w_cmp_v1,
           b_cmp_v1, w_cmp_v2, lam_re, lam_im, log_dt, b_re, b_im, c_re, c_im, d_skip, w_glu, w_proj_attn,
           w_proj_ssm, w_out, final_gain):
    bsz, seq, d = x.shape
    depth = w_in.shape[0]
    assert depth == 1, "the tail kernel fuses the final norm into the (single) layer"
    assert bsz % SSM_BT == 0 and seq % SSM_CHUNK == 0
    assert seq % (ATTN_K) == 0 and seq % PROJ_ROWS == 0 and seq >= WINDOW + ATTN_Q
    assert (seq - CMP_BLOCK) // CMP_STRIDE + 1 <= LANES and seq // CMP_STRIDE == LANES
    t = bsz * seq
    cos_t, sa_t, sb_t = _rope_tables(seq)
    ovlt, eg, dbias = _attn_constants(seq)
    perm = _head_perm()
    n_chunks = seq // SSM_CHUNK

    x2 = x.reshape(t, d)
    for l in range(depth):
        mod = _ada(c, w_ada[l], b_ada[l])
        shift, mscale, gate = [m.reshape(bsz, 1, d) for m in jnp.split(mod, 3, axis=-1)]
        (q, kc_r, ks0, ks1, kw, vc_r, vs0, vs1, vw0, vw1, g3, sz, u, szs, sma, sms) = _proj(
            x2, norm_gain[l].reshape(1, d), mscale, shift, cos_t, sa_t, sb_t, _proj_weight(w_in[l]), seq)

        nch = seq // CMP_STRIDE
        kc, vc = _compress(
            kc_r.reshape(bsz, nch, CMP_STRIDE * LANES), vc_r.reshape(bsz, nch, CMP_STRIDE * LANES),
            _compress_params(pe_cmp_k[l], w_cmp_k1[l], b_cmp_k1[l], w_cmp_k2[l]),
            _compress_params(pe_cmp_v[l], w_cmp_v1[l], b_cmp_v1[l], w_cmp_v2[l]))
        o_attn = _attn(q, kc, vc, ks0, ks1, vs0, vs1, kw, vw0, vw1, g3, sz, ovlt, eg, dbias, bsz, seq)

        mi, ws, wo, al = _ssm_params(lam_re[l], lam_im[l], log_dt[l], b_re[l], b_im[l], c_re[l], c_im[l],
                                     d_skip[l])
        y3 = _ssm(u, mi, ws, wo, al, n_chunks)

        x2 = _tail(x2, o_attn, y3, szs, sma, sms, gate, w_glu[l].astype(BF16),
                   w_proj_attn[l][perm].astype(BF16), w_proj_ssm[l].astype(BF16), w_out[l].astype(BF16),
                   final_gain.reshape(1, d) if l == depth - 1 else jnp.ones((1, d), F32), seq)
    return x2.reshape(bsz, seq, d)
```

```python
import functools
import math

import jax
import jax.numpy as jnp
import numpy as np
from jax import lax
from jax.experimental import pallas as pl
from jax.experimental.pallas import tpu as pltpu

F32 = jnp.float32
BF16 = jnp.bfloat16

N_HEADS = 8
N_GROUPS = 2
HEADS_PER_GROUP = N_HEADS // N_GROUPS
HEAD_DIM = 64
ATTN_WIDTH = N_HEADS * HEAD_DIM
KV_WIDTH = N_GROUPS * HEAD_DIM
CMP_BLOCK = 32
CMP_STRIDE = 16
CMP_HIDDEN = 2 * HEAD_DIM
SEL_BLOCK = 64
N_SEL = 16
WINDOW = 512
ROPE_THETA = 10000.0
FORCE_SCORE = 1.0e4
MASK_VALUE = -1.0e30
SSM_WIDTH = 256
SSM_GROUP = 16
SSM_GROUPS = SSM_WIDTH // SSM_GROUP
SSM_STATE = 64
NORM_EPS = 1e-6

LANES = 128
SUBLANES = 8
VMEM_LIMIT = 56 * 1024 * 1024

PROJ_ROWS = 512
ATTN_Q = 128
ATTN_K = 512
SSM_CHUNK = 8
SSM_HALVES = 2
SSM_BT = 8


def _dot(a, b):
    return jnp.dot(a, b, preferred_element_type=F32)


def _dot_nt(a, b):
    return lax.dot_general(a, b, (((1,), (1,)), ((), ())), preferred_element_type=F32)


def _dot_hilo(a, b_bf16):
    hi = a.astype(BF16)
    lo = (a - hi.astype(F32)).astype(BF16)
    return _dot(hi, b_bf16) + _dot(lo, b_bf16)


def _sigmoid(x):
    return 1.0 / (1.0 + jnp.exp(-x))


def _silu(x):
    return x * _sigmoid(x)


def _gelu(x):
    return 0.5 * x * (1.0 + jnp.tanh(math.sqrt(2.0 / math.pi) * (x + 0.044715 * (x * x * x))))


def _ada_kernel(c_ref, w_ref, b_ref, o_ref):
    c = c_ref[...]
    o_ref[...] = _dot(_silu(c).astype(BF16), w_ref[...].astype(BF16)) + b_ref[...]


def _ada(c, w, b):
    bsz, d = c.shape
    n = w.shape[1]
    return pl.pallas_call(
        _ada_kernel,
        grid=(n // d,),
        in_specs=[pl.BlockSpec((bsz, d), lambda j: (0, 0)),
                  pl.BlockSpec((d, d), lambda j: (0, j)),
                  pl.BlockSpec((1, d), lambda j: (0, j))],
        out_specs=pl.BlockSpec((bsz, d), lambda j: (0, j)),
        out_shape=jax.ShapeDtypeStruct((bsz, n), F32),
        compiler_params=pltpu.CompilerParams(vmem_limit_bytes=VMEM_LIMIT),
        name="ada",
    )(c, w, b.reshape(1, n))


_C_Q = 0
_C_KC = 512
_C_KS = 640
_C_KW = 768
_C_VC = 896
_C_VS = 1024
_C_VW = 1152
_C_G3 = 1280
_C_Z = 1408
_C_U = 1920
_C_ZS = 2176
_C_END = 2432


def _modulated_norm(x, gain, scale, shift):
    ms = jnp.mean(x * x, axis=-1, keepdims=True)
    return x * lax.rsqrt(ms + NORM_EPS) * gain * (1.0 + scale) + shift


def _proj_kernel(x_ref, gain_ref, scale_ref, shift_ref, cos_ref, sa_ref, sb_ref, w_ref,
                 q_ref, kc_ref, ks0_ref, ks1_ref, kw_ref, vc_ref, vs0_ref, vs1_ref, vw0_ref, vw1_ref,
                 g3_ref, sz_ref, u_ref, szs_ref, stage_ref, *, per_seq):
    x = x_ref[...]
    hb = _modulated_norm(x, gain_ref[...], scale_ref[0], shift_ref[0]).astype(BF16)

    def proj(a, b):
        return _dot(hb, w_ref[:, a:b])

    cos = cos_ref[...]
    sa = sa_ref[...]
    sb = sb_ref[...]

    def rope(t):
        return t * cos + pltpu.roll(t, LANES - 32, 1) * sa + pltpu.roll(t, 32, 1) * sb

    narrow = proj(_C_Q, _C_Z)
    part = lambda a: narrow[:, a:a + LANES]
    for v in range(ATTN_WIDTH // LANES):
        q_ref[:, v * LANES:(v + 1) * LANES] = rope(part(_C_Q + v * LANES)).astype(BF16)
    kw_ref[...] = rope(part(_C_KW)).astype(BF16)
    tm = x.shape[0]
    for out_ref, val in ((kc_ref, rope(part(_C_KC))), (vc_ref, part(_C_VC))):
        stage_ref[...] = val
        for j in range(CMP_STRIDE):
            out_ref[:, j * LANES:(j + 1) * LANES] = stage_ref[pl.ds(j, tm // CMP_STRIDE, stride=CMP_STRIDE), :]
    lane = lax.broadcasted_iota(jnp.int32, (tm, LANES), 1)
    pos = (pl.program_id(0) % per_seq) * tm + lax.broadcasted_iota(jnp.int32, (tm, LANES), 0)
    blk = pos >> int(math.log2(SEL_BLOCK))
    first = lane < HEAD_DIM
    ks = rope(part(_C_KS))
    ks0_ref[...] = jnp.where(first, ks, jnp.where(lane - HEAD_DIM == blk, 1.0, 0.0)).astype(BF16)
    ks1_ref[...] = jnp.where(first, jnp.where(lane == blk, 1.0, 0.0), ks).astype(BF16)
    vs = part(_C_VS)
    vs0_ref[...] = jnp.where(first, vs, 1.0).astype(BF16)
    vs1_ref[...] = jnp.where(first, 1.0, vs).astype(BF16)
    vw = part(_C_VW)
    vw0_ref[...] = jnp.where(first, vw, 1.0).astype(BF16)
    vw1_ref[...] = jnp.where(first, 1.0, vw).astype(BF16)
    g3_ref[...] = _sigmoid(part(_C_G3))
    sz_ref[...] = _silu(proj(_C_Z, _C_U)).astype(BF16)
    u = proj(_C_U, _C_ZS)
    for hf in range(SSM_HALVES):
        stage_ref[...] = u[:, hf * LANES:(hf + 1) * LANES]
        for j in range(SSM_CHUNK):
            u_ref[hf, :, j * LANES:(j + 1) * LANES] = stage_ref[
                pl.ds(j, tm // SSM_CHUNK, stride=SSM_CHUNK), :].astype(BF16)
    szs_ref[...] = _silu(proj(_C_ZS, _C_END)).astype(BF16)


def _proj(x2, gain, scale, shift, cos_t, sa_t, sb_t, w, seq):
    t, d = x2.shape
    tm = PROJ_ROWS
    per_seq = seq // tm
    row = lambda n: pl.BlockSpec((tm, n), lambda i: (i, 0))
    mod = pl.BlockSpec((1, 1, d), lambda i: (i // per_seq, 0, 0))
    tab = pl.BlockSpec((tm, LANES), lambda i: (i % per_seq, 0))
    cmp_w, ssm_w = CMP_STRIDE * LANES, SSM_CHUNK * LANES
    out_shape = [
        jax.ShapeDtypeStruct((t, ATTN_WIDTH), BF16),
        jax.ShapeDtypeStruct((t // CMP_STRIDE, cmp_w), F32),
        jax.ShapeDtypeStruct((t, LANES), BF16),
        jax.ShapeDtypeStruct((t, LANES), BF16),
        jax.ShapeDtypeStruct((t, LANES), BF16),
        jax.ShapeDtypeStruct((t // CMP_STRIDE, cmp_w), F32),
        jax.ShapeDtypeStruct((t, LANES), BF16),
        jax.ShapeDtypeStruct((t, LANES), BF16),
        jax.ShapeDtypeStruct((t, LANES), BF16),
        jax.ShapeDtypeStruct((t, LANES), BF16),
        jax.ShapeDtypeStruct((t, LANES), F32),
        jax.ShapeDtypeStruct((t, ATTN_WIDTH), BF16),
        jax.ShapeDtypeStruct((SSM_HALVES, t // SSM_CHUNK, ssm_w), BF16),
        jax.ShapeDtypeStruct((t, SSM_WIDTH), BF16),
    ]
    cmp_rows = pl.BlockSpec((tm // CMP_STRIDE, cmp_w), lambda i: (i, 0))
    out_specs = [row(ATTN_WIDTH), cmp_rows, row(LANES), row(LANES), row(LANES), cmp_rows] + [row(LANES)] * 5 + [
                 row(ATTN_WIDTH),
                 pl.BlockSpec((SSM_HALVES, tm // SSM_CHUNK, ssm_w), lambda i: (0, i, 0)),
                 row(SSM_WIDTH)]
    return pl.pallas_call(
        functools.partial(_proj_kernel, per_seq=per_seq),
        grid=(t // tm,),
        in_specs=[row(d), pl.BlockSpec((1, d), lambda i: (0, 0)), mod, mod, tab, tab, tab,
                  pl.BlockSpec((d, _C_END), lambda i: (0, 0))],
        out_specs=out_specs,
        out_shape=out_shape,
        scratch_shapes=[pltpu.VMEM((tm, LANES), F32)],
        compiler_params=pltpu.CompilerParams(vmem_limit_bytes=VMEM_LIMIT),
        name="proj",
    )(x2, gain, scale, shift, cos_t, sa_t, sb_t, w)


def _compress_kernel(rk_ref, rv_ref, pea_k, peb_k, wa_k, wb_k, b1_k, w2_k,
                     pea_v, peb_v, wa_v, wb_v, b1_v, w2_v, kc_ref, vc_ref):
    def one(r_ref, pea, peb, wa, wb, b1, w2):
        r = r_ref[0]
        first = _dot((r + pea[...]).astype(BF16), wa[...])
        second = _dot((r + peb[...]).astype(BF16), wb[...])
        nxt = pltpu.roll(second, second.shape[0] - 1, 0)
        hid = _gelu(first + nxt + b1[...])
        return _dot(hid.astype(BF16), w2[...])

    kc_ref[0] = one(rk_ref, pea_k, peb_k, wa_k, wb_k, b1_k, w2_k)
    vc_ref[0] = one(rv_ref, pea_v, peb_v, wa_v, wb_v, b1_v, w2_v)


def _compress(rk, rv, pk, pv):
    bsz, nch, width = rk.shape
    full = lambda a: pl.BlockSpec(a.shape, lambda b: (0,) * a.ndim)
    rspec = pl.BlockSpec((1, nch, width), lambda b: (b, 0, 0))
    ospec = pl.BlockSpec((1, nch, LANES), lambda b: (b, 0, 0))
    return pl.pallas_call(
        _compress_kernel,
        grid=(bsz,),
        in_specs=[rspec, rspec] + [full(a) for a in pk] + [full(a) for a in pv],
        out_specs=[ospec, ospec],
        out_shape=[jax.ShapeDtypeStruct((bsz, nch, LANES), F32)] * 2,
        compiler_params=pltpu.CompilerParams(vmem_limit_bytes=VMEM_LIMIT),
        name="compress",
    )(rk, rv, *pk, *pv)


def _compress_params(pe, w1, b1, w2):
    half = CMP_BLOCK // 2
    eye = jnp.eye(N_GROUPS, dtype=F32)
    w1r = w1.reshape(CMP_BLOCK, HEAD_DIM, CMP_HIDDEN)

    def expand(wl):
        return jnp.einsum('ldj,gh->lgdhj', wl, eye).reshape(half * KV_WIDTH, N_GROUPS * CMP_HIDDEN)

    def pe_lanes(p):
        return jnp.broadcast_to(p[:, None, :], (half, N_GROUPS, HEAD_DIM)).reshape(1, half * KV_WIDTH)

    w2e = jnp.einsum('jd,gh->gjhd', w2, eye).reshape(N_GROUPS * CMP_HIDDEN, KV_WIDTH)
    return (pe_lanes(pe[:half]), pe_lanes(pe[half:]),
            expand(w1r[:half]).astype(BF16), expand(w1r[half:]).astype(BF16),
            jnp.tile(b1, N_GROUPS).reshape(1, N_GROUPS * CMP_HIDDEN), w2e.astype(BF16))


def _attn_kernel(q_ref, kc_ref, vc_ref, ks0_ref, ks1_ref, vs0_ref, vs1_ref, kw_ref, vw0_ref, vw1_ref,
                 g3_ref, sz_ref, ovlt_ref, eg_ref, dbias_ref, o_ref, slc_ref, *, seq):
    tq, tk = ATTN_Q, ATTN_K
    hg = HEADS_PER_GROUP
    rows = hg * tq
    n_sel_blocks = seq // SEL_BLOCK
    sel_shift = int(math.log2(SEL_BLOCK))
    q0 = pl.program_id(1) * tq
    lane = lax.broadcasted_iota(jnp.int32, (tq, LANES), 1)
    t_row = q0 + lax.broadcasted_iota(jnp.int32, (tq, LANES), 0)
    cvalid = (lane * CMP_STRIDE + (CMP_BLOCK - 1)) <= t_row

    span = WINDOW + tq
    start = pl.multiple_of(jnp.maximum(q0 - WINDOW, 0), tq)
    kp = start + lax.broadcasted_iota(jnp.int32, (tq, span), 1)
    tt = q0 + lax.broadcasted_iota(jnp.int32, (tq, span), 0)
    wbias = jnp.where((kp <= tt) & (kp > tt - WINDOW), 0.0, MASK_VALUE)

    jidx = lax.broadcasted_iota(jnp.int32, (n_sel_blocks, tq), 0)
    t_lane = q0 + lax.broadcasted_iota(jnp.int32, (n_sel_blocks, tq), 1)
    forced = (jidx == 0) | (jidx == (t_lane >> sel_shift))
    future = jidx * SEL_BLOCK > t_lane
    jloc = lax.broadcasted_iota(jnp.int32, (SUBLANES, tq), 0)

    gexp = _dot_hilo(g3_ref[...], eg_ref[...])
    width = ATTN_WIDTH
    ks_refs, vs_refs, vw_refs = (ks0_ref, ks1_ref), (vs0_ref, vs1_ref), (vw0_ref, vw1_ref)

    groups = range(N_GROUPS)
    in_group = [(lane >= HEAD_DIM * g) & (lane < HEAD_DIM * (g + 1)) for g in groups]
    zero = jnp.zeros((tq, LANES), BF16)
    q_heads = [q_ref[:, h * LANES:(h + 1) * LANES] for h in range(hg)]
    qg = [jnp.concatenate([jnp.where(in_group[g], qh, zero) for qh in q_heads], axis=0) for g in groups]

    kcb = kc_ref[0].astype(BF16)
    vcb = vc_ref[0].astype(BF16)
    sc = [_dot_nt(qg[g], kcb).reshape(hg, tq, LANES) for g in groups]
    s3 = [_dot_nt(qg[g], kw_ref[pl.ds(start, span), :]).reshape(hg, tq, span) for g in groups]

    p_cmp = []
    for g in groups:
        s = jnp.where(cvalid[None], sc[g], MASK_VALUE)
        m = jnp.max(s, axis=-1, keepdims=True)
        e = jnp.where(cvalid[None], jnp.exp(s - m), 0.0)
        den = jnp.sum(e, axis=-1, keepdims=True)
        p_cmp.append(e * (1.0 / jnp.where(den > 0.0, den, 1.0)))
    o_cmp = [_dot(p_cmp[g].reshape(rows, LANES).astype(BF16), vcb) for g in groups]

    imp = []
    for g in groups:
        psum = p_cmp[g][0]
        for h in range(1, hg):
            psum = psum + p_cmp[g][h]
        p_hi = psum.astype(BF16)
        p_lo = (psum - p_hi.astype(F32)).astype(BF16)
        imp.append((_dot_nt(ovlt_ref[...], p_hi) + _dot_nt(ovlt_ref[...], p_lo))[0:n_sel_blocks])

    o_win = []
    for g in groups:
        sw = s3[g] + wbias[None]
        m3 = jnp.max(sw, axis=-1, keepdims=True)
        p3 = jnp.exp(sw - m3).astype(BF16).reshape(rows, span)
        ow = _dot(p3, vw_refs[g][pl.ds(start, span), :])
        o_win.append(ow * (1.0 / pltpu.roll(ow, HEAD_DIM, 1)))

    qsel = []
    for g in groups:
        imp_g = jnp.where(forced, FORCE_SCORE, jnp.where(future, -1.0, imp[g]))
        tiles = [imp_g[k * SUBLANES:(k + 1) * SUBLANES] for k in range(n_sel_blocks // SUBLANES)]
        cnts = [jnp.zeros((SUBLANES, tq), jnp.int32) for _ in tiles]
        for i in range(n_sel_blocks):
            other = jnp.broadcast_to(imp_g[i:i + 1, :], (SUBLANES, tq))
            ki, ri = divmod(i, SUBLANES)
            for k, tile_k in enumerate(tiles):
                if k < ki:
                    ahead = other > tile_k
                elif k > ki:
                    ahead = other >= tile_k
                else:
                    ahead = (other > tile_k) | ((other == tile_k) & (jloc > ri))
                cnts[k] = cnts[k] + jnp.where(ahead, 1, 0)
        cnt = jnp.concatenate(cnts, axis=0)
        mask_t = jnp.where(cnt < min(N_SEL, n_sel_blocks), 0.0, MASK_VALUE)
        lo_rows = HEAD_DIM * (1 - g)
        parts = [mask_t, jnp.zeros((LANES - n_sel_blocks - lo_rows, tq), F32)]
        if lo_rows:
            parts = [jnp.zeros((lo_rows, tq), F32)] + parts
        maskq = jnp.concatenate(parts, axis=0).T.astype(BF16)
        qsel.append(jnp.concatenate([jnp.where(in_group[g], qh, maskq) for qh in q_heads], axis=0))
    partial = [(o_cmp[g], o_win[g]) for g in groups]

    n_tiles = q0 // tk + 1
    for n in range(1, seq // tk + 1):
        @pl.when(n_tiles == n)
        def _(n=n):
            klen = n * tk
            s2 = [_dot_nt(qsel[g], ks_refs[g][0:klen, :]) for g in range(N_GROUPS)]
            for g in range(N_GROUPS):
                last = (s2[g][:, klen - tk:].reshape(hg, tq, tk) + dbias_ref[0][None]).reshape(rows, tk)
                sg = last if n == 1 else jnp.concatenate([s2[g][:, :klen - tk], last], axis=1)
                m2 = jnp.max(sg, axis=-1, keepdims=True)
                p2 = jnp.exp(sg - m2).astype(BF16)
                slc_ref[g] = _dot(p2, vs_refs[g][0:klen, :])

    combined = []
    for g in range(N_GROUPS):
        ov = slc_ref[g]
        o_slc = ov * (1.0 / pltpu.roll(ov, HEAD_DIM, 1))
        combined.append((partial[g][0], o_slc, partial[g][1]))

    first_half = lane < HEAD_DIM
    for h in range(hg):
        rws = slice(h * tq, (h + 1) * tq)
        cols = slice(h * LANES, (h + 1) * LANES)
        acc = jnp.zeros((tq, LANES), F32)
        for j in range(3):
            both = jnp.where(first_half, combined[0][j][rws], combined[1][j][rws])
            acc = acc + gexp[:, j * width + h * LANES: j * width + (h + 1) * LANES] * both
        o_ref[:, cols] = (acc * sz_ref[:, cols].astype(F32)).astype(BF16)


def _attn(q, kc, vc, ks0, ks1, vs0, vs1, kw, vw0, vw1, g3, sz, ovlt, eg, dbias, bsz, seq):
    tq, tk = ATTN_Q, ATTN_K
    nq = seq // tq
    nch = kc.shape[1]
    qrow = lambda n: pl.BlockSpec((tq, n), lambda b, i: (b * nq + i, 0))
    per_seq = pl.BlockSpec((seq, LANES), lambda b, i: (b, 0))
    cmp_spec = pl.BlockSpec((1, nch, LANES), lambda b, i: (b, 0, 0))
    full = lambda a: pl.BlockSpec(a.shape, lambda b, i: (0,) * a.ndim)
    diag = pl.BlockSpec((1, tq, tk), lambda b, i: (i % (tk // tq), 0, 0))
    return pl.pallas_call(
        functools.partial(_attn_kernel, seq=seq),
        grid=(bsz, nq),
        in_specs=[qrow(ATTN_WIDTH), cmp_spec, cmp_spec] + [per_seq] * 7
                 + [qrow(LANES), qrow(ATTN_WIDTH), full(ovlt), full(eg), diag],
        out_specs=qrow(ATTN_WIDTH),
        out_shape=jax.ShapeDtypeStruct((bsz * seq, ATTN_WIDTH), BF16),
        scratch_shapes=[pltpu.VMEM((N_GROUPS, HEADS_PER_GROUP * tq, LANES), F32)],
        compiler_params=pltpu.CompilerParams(vmem_limit_bytes=VMEM_LIMIT),
        name="attn",
    )(q, kc, vc, ks0, ks1, vs0, vs1, kw, vw0, vw1, g3, sz, ovlt, eg, dbias)


def _ssm_kernel(u_ref, mi_ref, ws_ref, wo_ref, al_ref, y_ref, sx_ref, *, n_chunks):
    u = u_ref[0]
    n_tiles = sx_ref.shape[0]
    nt = n_tiles // 2
    tile = lambda j: slice(j * LANES, (j + 1) * LANES)
    for j in range(n_tiles):
        sx_ref[j] = _dot(u, ws_ref[0, :, tile(j)])
    a_re = [al_ref[0, 0:1, tile(j)] for j in range(nt)]
    a_im = [al_ref[0, 1:2, tile(j)] for j in range(nt)]
    bt = SSM_BT

    def body(k, carry):
        rows = pl.ds(k, bt, stride=n_chunks)
        new = []
        for j in range(nt):
            x_re, x_im = carry[j], carry[nt + j]
            s_re = sx_ref[j, rows, :]
            s_im = sx_ref[nt + j, rows, :]
            sx_ref[j, rows, :] = x_re
            sx_ref[nt + j, rows, :] = x_im
            new.append((a_re[j] * x_re - a_im[j] * x_im + s_re, a_re[j] * x_im + a_im[j] * x_re + s_im))
        return tuple([n[0] for n in new] + [n[1] for n in new])

    zero = jnp.zeros((bt, LANES), F32)
    lax.fori_loop(0, n_chunks, body, (zero,) * n_tiles)
    width = u.shape[1]
    ctile = 2 * LANES
    y_cols = [_dot(u[:, :(c + 1) * ctile], mi_ref[0, :(c + 1) * ctile, c * ctile:(c + 1) * ctile])
              for c in range(width // ctile)]
    xs = jnp.concatenate([sx_ref[j].astype(BF16) for j in range(n_tiles)], axis=1)
    y = jnp.concatenate(y_cols, axis=1) + _dot(xs, wo_ref[0])
    for j in range(SSM_CHUNK):
        y_ref[0, pl.ds(j, y.shape[0], stride=SSM_CHUNK), :] = y[:, tile(j)]


def _ssm(u3, mi, ws, wo, al, n_chunks):
    halves, rows, width = u3.shape
    r = SSM_BT * n_chunks
    wspec = lambda a: pl.BlockSpec((1,) + a.shape[1:], lambda hf, i: (hf, 0, 0))
    return pl.pallas_call(
        functools.partial(_ssm_kernel, n_chunks=n_chunks),
        grid=(halves, rows // r),
        in_specs=[pl.BlockSpec((1, r, width), lambda hf, i: (hf, i, 0)),
                  wspec(mi), wspec(ws), wspec(wo), wspec(al)],
        out_specs=pl.BlockSpec((1, r * SSM_CHUNK, LANES), lambda hf, i: (hf, i, 0)),
        out_shape=jax.ShapeDtypeStruct((halves, rows * SSM_CHUNK, LANES), F32),
        scratch_shapes=[pltpu.VMEM((ws.shape[2] // LANES, r, LANES), F32)],
        compiler_params=pltpu.CompilerParams(vmem_limit_bytes=VMEM_LIMIT),
        name="ssm",
    )(u3, mi, ws, wo, al)


def _ssm_params(lam_re, lam_im, log_dt, b_re, b_im, c_re, c_im, d_skip):
    L = SSM_CHUNK
    G, P, C = SSM_GROUPS, SSM_STATE, SSM_GROUP
    gh = G // SSM_HALVES
    dt = jnp.exp(log_dt)[:, None]
    lr, li = lam_re, lam_im
    mag = jnp.exp(lr * dt)
    ab_re = mag * jnp.cos(li * dt)
    ab_im = mag * jnp.sin(li * dt)
    nr, ni = ab_re - 1.0, ab_im
    den = lr * lr + li * li
    cr = ((nr * lr + ni * li) / den)[..., None]
    ci = ((ni * lr - nr * li) / den)[..., None]
    bb_re = cr * b_re - ci * b_im
    bb_im = cr * b_im + ci * b_re
    pr, pi = [jnp.ones_like(ab_re)], [jnp.zeros_like(ab_im)]
    for _ in range(L):
        pr.append(pr[-1] * ab_re - pi[-1] * ab_im)
        pi.append(pr[-2] * ab_im + pi[-1] * ab_re)
    pw_re = jnp.stack(pr)
    pw_im = jnp.stack(pi)
    ca_re = jnp.einsum('gcp,dgp->dgcp', c_re, pw_re[:L]) - jnp.einsum('gcp,dgp->dgcp', c_im, pw_im[:L])
    ca_im = jnp.einsum('gcp,dgp->dgcp', c_re, pw_im[:L]) + jnp.einsum('gcp,dgp->dgcp', c_im, pw_re[:L])
    kern = jnp.einsum('dgop,gpi->dgoi', ca_re, bb_re) - jnp.einsum('dgop,gpi->dgoi', ca_im, bb_im)
    lag = jnp.arange(L)[None, :] - jnp.arange(L)[:, None]
    kt = jnp.where((lag >= 0)[:, :, None, None, None], kern[jnp.clip(lag, 0, L - 1)], 0.0)
    d_gc = d_skip.reshape(G, C)
    e_re, e_im = pw_re[:L][::-1], pw_im[:L][::-1]
    ws_re = jnp.einsum('sgp,gpi->sgip', e_re, bb_re) - jnp.einsum('sgp,gpi->sgip', e_im, bb_im)
    ws_im = jnp.einsum('sgp,gpi->sgip', e_re, bb_im) + jnp.einsum('sgp,gpi->sgip', e_im, bb_re)
    o_re, o_im = pw_re[1:L + 1], pw_im[1:L + 1]
    wo_re = jnp.einsum('gop,tgp->gpto', c_re, o_re) - jnp.einsum('gop,tgp->gpto', c_im, o_im)
    wo_im = -(jnp.einsum('gop,tgp->gpto', c_re, o_im) + jnp.einsum('gop,tgp->gpto', c_im, o_re))

    mi_h, ws_h, wo_h, al_h = [], [], [], []
    for hf in range(SSM_HALVES):
        gs = slice(hf * gh, (hf + 1) * gh)
        eye_h = jnp.eye(gh, dtype=F32)
        m_intra = jnp.einsum('stgoi,gh->sgitho', kt[:, :, gs], eye_h)
        skip = jnp.einsum('st,gh,io,go->sgitho', jnp.eye(L, dtype=F32), eye_h, jnp.eye(C, dtype=F32), d_gc[gs])
        mi_h.append((m_intra + skip).reshape(L * gh * C, L * gh * C))
        wsr = jnp.einsum('sgip,gh->sgihp', ws_re[:, gs], eye_h).reshape(L * gh * C, gh * P)
        wsi = jnp.einsum('sgip,gh->sgihp', ws_im[:, gs], eye_h).reshape(L * gh * C, gh * P)
        ws_h.append(jnp.concatenate([wsr, wsi], axis=1))
        wor = jnp.einsum('gpto,gh->gptho', wo_re[gs], eye_h).reshape(gh * P, L * gh * C)
        woi = jnp.einsum('gpto,gh->gptho', wo_im[gs], eye_h).reshape(gh * P, L * gh * C)
        wo_h.append(jnp.concatenate([wor, woi], axis=0))
        al_h.append(jnp.stack([pw_re[L, gs].reshape(gh * P), pw_im[L, gs].reshape(gh * P)]))
    return (jnp.stack(mi_h).astype(BF16), jnp.stack(ws_h).astype(BF16), jnp.stack(wo_h).astype(BF16),
            jnp.stack(al_h))


def _tail_kernel(x_ref, oa_ref, y_ref, szs_ref, gain_ref, scale_ref, shift_ref, gate_ref, wm_ref, wglu_ref,
                 wpa_ref, wps_ref, wout_ref, fg_ref, o_ref):
    x = x_ref[...]
    d = x.shape[1]
    hb = _modulated_norm(x, gain_ref[...], scale_ref[0], shift_ref[0]).astype(BF16)
    mg = _sigmoid(_dot(hb, wm_ref[...]))
    y = jnp.concatenate([y_ref[0], y_ref[1]], axis=-1)
    yy = _dot(_gelu(y).astype(BF16), wglu_ref[...])
    o_ssm = yy[:, :SSM_WIDTH] * _sigmoid(yy[:, SSM_WIDTH:]) * szs_ref[...].astype(F32)
    merged = (mg[:, :d] * _dot(oa_ref[...], wpa_ref[...])
              + mg[:, d:] * _dot(o_ssm.astype(BF16), wps_ref[...]))
    xo = x + gate_ref[0] * _dot(merged.astype(BF16), wout_ref[...])
    ms = jnp.mean(xo * xo, axis=-1, keepdims=True)
    o_ref[...] = xo * lax.rsqrt(ms + NORM_EPS) * fg_ref[...]


def _tail(x2, oa, y3, szs, gain, scale, shift, gate, wm, wglu, wpa, wps, wout, fgain, seq):
    t, d = x2.shape
    tm = PROJ_ROWS
    per_seq = seq // tm
    row = lambda n: pl.BlockSpec((tm, n), lambda i: (i, 0))
    full = lambda a: pl.BlockSpec(a.shape, lambda i: (0,) * a.ndim)
    mod = pl.BlockSpec((1, 1, d), lambda i: (i // per_seq, 0, 0))
    return pl.pallas_call(
        _tail_kernel,
        grid=(t // tm,),
        in_specs=[row(d), row(ATTN_WIDTH), pl.BlockSpec((SSM_HALVES, tm, LANES), lambda i: (0, i, 0)),
                  row(SSM_WIDTH), full(gain), mod, mod, mod,
                  full(wm), full(wglu), full(wpa), full(wps), full(wout), full(fgain)],
        out_specs=row(d),
        out_shape=jax.ShapeDtypeStruct((t, d), F32),
        compiler_params=pltpu.CompilerParams(vmem_limit_bytes=VMEM_LIMIT),
        name="tail",
    )(x2, oa, y3, szs, gain, scale, shift, gate, wm, wglu, wpa, wps, wout, fgain)


def _head_perm():
    idx = np.arange(ATTN_WIDTH)
    h, rem = idx // LANES, idx % LANES
    g, dd = rem // HEAD_DIM, rem % HEAD_DIM
    return (g * HEADS_PER_GROUP + h) * HEAD_DIM + dd


def _rope_tables(seq):
    half = HEAD_DIM // 2
    inv_freq = ROPE_THETA ** (-jnp.arange(half, dtype=F32) / half)
    ang = jnp.arange(seq, dtype=F32)[:, None] * inv_freq[None, :]
    cos, sin = jnp.cos(ang), jnp.sin(ang)
    zero = jnp.zeros_like(sin)
    reps = LANES // HEAD_DIM
    cos_t = jnp.tile(jnp.concatenate([cos, cos], axis=1), (1, reps))
    sa_t = jnp.tile(jnp.concatenate([-sin, zero], axis=1), (1, reps))
    sb_t = jnp.tile(jnp.concatenate([zero, sin], axis=1), (1, reps))
    return cos_t, sa_t, sb_t


def _proj_weight(w_in):
    d = w_in.shape[0]
    o_q, o_kv, o_g, o_z, o_u, o_zs, o_m = 0, 512, 1280, 1304, 1816, 2072, 2328
    perm = _head_perm()
    kv = lambda j: w_in[:, o_kv + j * KV_WIDTH: o_kv + (j + 1) * KV_WIDTH]
    gates = jnp.pad(w_in[:, o_g:o_z], ((0, 0), (0, LANES - 3 * N_HEADS)))
    cols = [w_in[:, o_q:o_kv][:, perm] * (HEAD_DIM ** -0.5),
            kv(0), kv(2), kv(4), kv(1), kv(3), kv(5), gates,
            w_in[:, o_z:o_u][:, perm], w_in[:, o_u:o_zs], w_in[:, o_zs:o_m]]
    w = jnp.concatenate(cols, axis=1)
    assert w.shape == (d, _C_END)
    return w.astype(BF16), w_in[:, o_m:].astype(BF16)


def _attn_constants(seq):
    n_cmp = LANES
    cs = np.arange(n_cmp) * CMP_STRIDE
    ss = np.arange(LANES) * SEL_BLOCK
    ovl = (np.minimum(cs[:, None] + CMP_BLOCK, ss[None, :] + SEL_BLOCK) > np.maximum(cs[:, None], ss[None, :]))
    ovl = ovl & (np.arange(LANES)[None, :] < seq // SEL_BLOCK) & (cs[:, None] + CMP_BLOCK <= seq)
    tq, tk = ATTN_Q, ATTN_K
    v = np.arange(tk // tq)[:, None, None]
    r = np.arange(tq)[None, :, None]
    cc = np.arange(tk)[None, None, :]
    dbias = np.where(cc <= v * tq + r, 0.0, MASK_VALUE).astype(np.float32)
    eg = np.zeros((LANES, 3 * ATTN_WIDTH), np.float32)
    for g in range(N_GROUPS):
        for h in range(HEADS_PER_GROUP):
            for j in range(3):
                base = j * ATTN_WIDTH + h * LANES + g * HEAD_DIM
                eg[(g * HEADS_PER_GROUP + h) * 3 + j, base:base + HEAD_DIM] = 1.0
    return (jnp.asarray(ovl.T, BF16), jnp.asarray(eg, BF16), jnp.asarray(dbias, F32))


def kernel(x, c, w_ada, b_ada, norm_gain, w_in, pe_cmp_k, w_cmp_k1, b_cmp_k1, w_cmp_k2, pe_cmp_v, w_cmp_v1,
           b_cmp_v1, w_cmp_v2, lam_re, lam_im, log_dt, b_re, b_im, c_re, c_im, d_skip, w_glu, w_proj_attn,
           w_proj_ssm, w_out, final_gain):
    bsz, seq, d = x.shape
    depth = w_in.shape[0]
    assert depth == 1, "the tail kernel fuses the final norm into the (single) layer"
    assert bsz % SSM_BT == 0 and seq % SSM_CHUNK == 0
    assert seq % (ATTN_K) == 0 and seq % PROJ_ROWS == 0 and seq >= WINDOW + ATTN_Q
    assert (seq - CMP_BLOCK) // CMP_STRIDE + 1 <= LANES and seq // CMP_STRIDE == LANES
    t = bsz * seq
    cos_t, sa_t, sb_t = _rope_tables(seq)
    ovlt, eg, dbias = _attn_constants(seq)
    perm = _head_perm()
    n_chunks = seq // SSM_CHUNK

    x2 = x.reshape(t, d)
    for l in range(depth):
        mod = _ada(c, w_ada[l], b_ada[l])
        shift, mscale, gate = [m.reshape(bsz, 1, d) for m in jnp.split(mod, 3, axis=-1)]
        gain = norm_gain[l].reshape(1, d)
        w_proj, w_merge = _proj_weight(w_in[l])
        (q, kc_r, ks0, ks1, kw, vc_r, vs0, vs1, vw0, vw1, g3, sz, u, szs) = _proj(
            x2, gain, mscale, shift, cos_t, sa_t, sb_t, w_proj, seq)

        nch = seq // CMP_STRIDE
        kc, vc = _compress(
            kc_r.reshape(bsz, nch, CMP_STRIDE * LANES), vc_r.reshape(bsz, nch, CMP_STRIDE * LANES),
            _compress_params(pe_cmp_k[l], w_cmp_k1[l], b_cmp_k1[l], w_cmp_k2[l]),
            _compress_params(pe_cmp_v[l], w_cmp_v1[l], b_cmp_v1[l], w_cmp_v2[l]))
        o_attn = _attn(q, kc, vc, ks0, ks1, vs0, vs1, kw, vw0, vw1, g3, sz, ovlt, eg, dbias, bsz, seq)

        mi, ws, wo, al = _ssm_params(lam_re[l], lam_im[l], log_dt[l], b_re[l], b_im[l], c_re[l], c_im[l],
                                     d_skip[l])
        y3 = _ssm(u, mi, ws, wo, al, n_chunks)

        x2 = _tail(x2, o_attn, y3, szs, gain, mscale, shift, gate, w_merge, w_glu[l].astype(BF16),
                   w_proj_attn[l][perm].astype(BF16), w_proj_ssm[l].astype(BF16), w_out[l].astype(BF16),
                   final_gain.reshape(1, d) if l == depth - 1 else jnp.ones((1, d), F32), seq)
    return x2.reshape(bsz, seq, d)
```

```python
import functools
import math

import jax
import jax.numpy as jnp
import numpy as np
from jax import lax
from jax.experimental import pallas as pl
from jax.experimental.pallas import tpu as pltpu

F32 = jnp.float32
BF16 = jnp.bfloat16

N_HEADS = 8
N_GROUPS = 2
HEADS_PER_GROUP = N_HEADS // N_GROUPS
HEAD_DIM = 64
ATTN_WIDTH = N_HEADS * HEAD_DIM
KV_WIDTH = N_GROUPS * HEAD_DIM
CMP_BLOCK = 32
CMP_STRIDE = 16
CMP_HIDDEN = 2 * HEAD_DIM
SEL_BLOCK = 64
N_SEL = 16
WINDOW = 512
ROPE_THETA = 10000.0
FORCE_SCORE = 1.0e4
MASK_VALUE = -1.0e30
SSM_WIDTH = 256
SSM_GROUP = 16
SSM_GROUPS = SSM_WIDTH // SSM_GROUP
SSM_STATE = 64
NORM_EPS = 1e-6

LANES = 128
SUBLANES = 8
VMEM_LIMIT = 56 * 1024 * 1024

PROJ_ROWS = 512
ATTN_Q = 256
ATTN_K = 512
SELECT_Q = 512
SELECT_SUB = 128
SSM_CHUNK = 8
SSM_HALVES = 2
SSM_BT = 8


def _dot(a, b):
    return jnp.dot(a, b, preferred_element_type=F32)


def _dot_nt(a, b):
    return lax.dot_general(a, b, (((1,), (1,)), ((), ())), preferred_element_type=F32)


def _dot_hilo(a, b_bf16):
    hi = a.astype(BF16)
    lo = (a - hi.astype(F32)).astype(BF16)
    return _dot(hi, b_bf16) + _dot(lo, b_bf16)


def _sigmoid(x):
    return 1.0 / (1.0 + jnp.exp(-x))


def _silu(x):
    return x * _sigmoid(x)


def _gelu(x):
    return 0.5 * x * (1.0 + jnp.tanh(math.sqrt(2.0 / math.pi) * (x + 0.044715 * (x * x * x))))


def _ada_kernel(c_ref, w_ref, b_ref, o_ref):
    c = c_ref[...]
    o_ref[...] = _dot(_silu(c).astype(BF16), w_ref[...].astype(BF16)) + b_ref[...]


def _ada(c, w, b):
    bsz, d = c.shape
    n = w.shape[1]
    return pl.pallas_call(
        _ada_kernel,
        grid=(n // d,),
        in_specs=[pl.BlockSpec((bsz, d), lambda j: (0, 0)),
                  pl.BlockSpec((d, d), lambda j: (0, j)),
                  pl.BlockSpec((1, d), lambda j: (0, j))],
        out_specs=pl.BlockSpec((bsz, d), lambda j: (0, j)),
        out_shape=jax.ShapeDtypeStruct((bsz, n), F32),
        compiler_params=pltpu.CompilerParams(vmem_limit_bytes=VMEM_LIMIT),
        name="ada",
    )(c, w, b.reshape(1, n))


_C_Q = 0
_C_KC = 512
_C_KS = 640
_C_KW = 768
_C_VC = 896
_C_VS = 1024
_C_VW = 1152
_C_G3 = 1280
_C_Z = 1408
_C_U = 1920
_C_ZS = 2176
_C_END = 2432


def _modulated_norm(x, gain, scale, shift):
    ms = jnp.mean(x * x, axis=-1, keepdims=True)
    return x * lax.rsqrt(ms + NORM_EPS) * gain * (1.0 + scale) + shift


def _proj_kernel(x_ref, gain_ref, scale_ref, shift_ref, cos_ref, sa_ref, sb_ref, w_ref,
                 q_ref, kc_ref, ks0_ref, ks1_ref, kw_ref, vc_ref, vs0_ref, vs1_ref, vw0_ref, vw1_ref,
                 g3_ref, sz_ref, u_ref, szs_ref, stage_ref, *, per_seq):
    x = x_ref[...]
    hb = _modulated_norm(x, gain_ref[...], scale_ref[0], shift_ref[0]).astype(BF16)

    def proj(a, b):
        return _dot(hb, w_ref[:, a:b])

    cos = cos_ref[...]
    sa = sa_ref[...]
    sb = sb_ref[...]

    def rope(t):
        return t * cos + pltpu.roll(t, LANES - 32, 1) * sa + pltpu.roll(t, 32, 1) * sb

    narrow = proj(_C_Q, _C_Z)
    part = lambda a: narrow[:, a:a + LANES]
    for v in range(ATTN_WIDTH // LANES):
        q_ref[:, v * LANES:(v + 1) * LANES] = rope(part(_C_Q + v * LANES)).astype(BF16)
    kw_ref[...] = rope(part(_C_KW)).astype(BF16)
    tm = x.shape[0]
    for out_ref, val in ((kc_ref, rope(part(_C_KC))), (vc_ref, part(_C_VC))):
        stage_ref[...] = val
        for j in range(CMP_STRIDE):
            out_ref[:, j * LANES:(j + 1) * LANES] = stage_ref[pl.ds(j, tm // CMP_STRIDE, stride=CMP_STRIDE), :]
    lane = lax.broadcasted_iota(jnp.int32, (tm, LANES), 1)
    pos = (pl.program_id(0) % per_seq) * tm + lax.broadcasted_iota(jnp.int32, (tm, LANES), 0)
    blk = pos >> int(math.log2(SEL_BLOCK))
    first = lane < HEAD_DIM
    ks = rope(part(_C_KS))
    ks0_ref[...] = jnp.where(first, ks, jnp.where(lane - HEAD_DIM == blk, 1.0, 0.0)).astype(BF16)
    ks1_ref[...] = jnp.where(first, jnp.where(lane == blk, 1.0, 0.0), ks).astype(BF16)
    ones = jnp.ones((tm, LANES), BF16)
    zero = jnp.zeros((tm, LANES), F32)
    for v_val, refs in ((part(_C_VS), (vs0_ref, vs1_ref)), (part(_C_VW), (vw0_ref, vw1_ref))):
        refs[0][:, :LANES] = jnp.where(first, v_val, zero).astype(BF16)
        refs[1][:, :LANES] = jnp.where(first, zero, v_val).astype(BF16)
        refs[0][:, LANES:] = ones
        refs[1][:, LANES:] = ones
    g3_ref[...] = _sigmoid(part(_C_G3))
    sz_ref[...] = _silu(proj(_C_Z, _C_U)).astype(BF16)
    u = proj(_C_U, _C_ZS)
    for hf in range(SSM_HALVES):
        stage_ref[...] = u[:, hf * LANES:(hf + 1) * LANES]
        for j in range(SSM_CHUNK):
            u_ref[hf, :, j * LANES:(j + 1) * LANES] = stage_ref[
                pl.ds(j, tm // SSM_CHUNK, stride=SSM_CHUNK), :].astype(BF16)
    szs_ref[...] = _silu(proj(_C_ZS, _C_END)).astype(BF16)


def _proj(x2, gain, scale, shift, cos_t, sa_t, sb_t, w, seq):
    t, d = x2.shape
    tm = PROJ_ROWS
    per_seq = seq // tm
    row = lambda n: pl.BlockSpec((tm, n), lambda i: (i, 0))
    mod = pl.BlockSpec((1, 1, d), lambda i: (i // per_seq, 0, 0))
    tab = pl.BlockSpec((tm, LANES), lambda i: (i % per_seq, 0))
    cmp_w, ssm_w = CMP_STRIDE * LANES, SSM_CHUNK * LANES
    out_shape = [
        jax.ShapeDtypeStruct((t, ATTN_WIDTH), BF16),
        jax.ShapeDtypeStruct((t // CMP_STRIDE, cmp_w), F32),
        jax.ShapeDtypeStruct((t, LANES), BF16),
        jax.ShapeDtypeStruct((t, LANES), BF16),
        jax.ShapeDtypeStruct((t, LANES), BF16),
        jax.ShapeDtypeStruct((t // CMP_STRIDE, cmp_w), F32),
        jax.ShapeDtypeStruct((t, 2 * LANES), BF16),
        jax.ShapeDtypeStruct((t, 2 * LANES), BF16),
        jax.ShapeDtypeStruct((t, 2 * LANES), BF16),
        jax.ShapeDtypeStruct((t, 2 * LANES), BF16),
        jax.ShapeDtypeStruct((t, LANES), F32),
        jax.ShapeDtypeStruct((t, ATTN_WIDTH), BF16),
        jax.ShapeDtypeStruct((SSM_HALVES, t // SSM_CHUNK, ssm_w), BF16),
        jax.ShapeDtypeStruct((t, SSM_WIDTH), BF16),
    ]
    cmp_rows = pl.BlockSpec((tm // CMP_STRIDE, cmp_w), lambda i: (i, 0))
    out_specs = [row(ATTN_WIDTH), cmp_rows, row(LANES), row(LANES), row(LANES), cmp_rows] + [row(2 * LANES)] * 4 + [row(LANES)] + [
                 row(ATTN_WIDTH),
                 pl.BlockSpec((SSM_HALVES, tm // SSM_CHUNK, ssm_w), lambda i: (0, i, 0)),
                 row(SSM_WIDTH)]
    return pl.pallas_call(
        functools.partial(_proj_kernel, per_seq=per_seq),
        grid=(t // tm,),
        in_specs=[row(d), pl.BlockSpec((1, d), lambda i: (0, 0)), mod, mod, tab, tab, tab,
                  pl.BlockSpec((d, _C_END), lambda i: (0, 0))],
        out_specs=out_specs,
        out_shape=out_shape,
        scratch_shapes=[pltpu.VMEM((tm, LANES), F32)],
        compiler_params=pltpu.CompilerParams(vmem_limit_bytes=VMEM_LIMIT),
        name="proj",
    )(x2, gain, scale, shift, cos_t, sa_t, sb_t, w)


def _compress_kernel(rk_ref, rv_ref, pea_k, peb_k, wa_k, wb_k, b1_k, w2_k,
                     pea_v, peb_v, wa_v, wb_v, b1_v, w2_v, kc_ref, vc_ref):
    def one(r_ref, pea, peb, wa, wb, b1, w2):
        r = r_ref[0]
        first = _dot((r + pea[...]).astype(BF16), wa[...])
        second = _dot((r + peb[...]).astype(BF16), wb[...])
        nxt = pltpu.roll(second, second.shape[0] - 1, 0)
        hid = _gelu(first + nxt + b1[...])
        return _dot(hid.astype(BF16), w2[...])

    kc_ref[0] = one(rk_ref, pea_k, peb_k, wa_k, wb_k, b1_k, w2_k)
    vc_ref[0] = one(rv_ref, pea_v, peb_v, wa_v, wb_v, b1_v, w2_v)


def _compress(rk, rv, pk, pv):
    bsz, nch, width = rk.shape
    full = lambda a: pl.BlockSpec(a.shape, lambda b: (0,) * a.ndim)
    rspec = pl.BlockSpec((1, nch, width), lambda b: (b, 0, 0))
    ospec = pl.BlockSpec((1, nch, LANES), lambda b: (b, 0, 0))
    return pl.pallas_call(
        _compress_kernel,
        grid=(bsz,),
        in_specs=[rspec, rspec] + [full(a) for a in pk] + [full(a) for a in pv],
        out_specs=[ospec, ospec],
        out_shape=[jax.ShapeDtypeStruct((bsz, nch, LANES), F32)] * 2,
        compiler_params=pltpu.CompilerParams(vmem_limit_bytes=VMEM_LIMIT),
        name="compress",
    )(rk, rv, *pk, *pv)


def _compress_params(pe, w1, b1, w2):
    half = CMP_BLOCK // 2
    eye = jnp.eye(N_GROUPS, dtype=F32)
    w1r = w1.reshape(CMP_BLOCK, HEAD_DIM, CMP_HIDDEN)

    def expand(wl):
        return jnp.einsum('ldj,gh->lgdhj', wl, eye).reshape(half * KV_WIDTH, N_GROUPS * CMP_HIDDEN)

    def pe_lanes(p):
        return jnp.broadcast_to(p[:, None, :], (half, N_GROUPS, HEAD_DIM)).reshape(1, half * KV_WIDTH)

    w2e = jnp.einsum('jd,gh->gjhd', w2, eye).reshape(N_GROUPS * CMP_HIDDEN, KV_WIDTH)
    return (pe_lanes(pe[:half]), pe_lanes(pe[half:]),
            expand(w1r[:half]).astype(BF16), expand(w1r[half:]).astype(BF16),
            jnp.tile(b1, N_GROUPS).reshape(1, N_GROUPS * CMP_HIDDEN), w2e.astype(BF16))


def _select_kernel(q_ref, kc_ref, vc_ref, g3_ref, ovlt_ref, egc_ref, mask_ref, ocmp_ref, gate_ref, *, seq):
    ts = SELECT_SUB
    hg = HEADS_PER_GROUP
    rows = hg * ts
    n_sel_blocks = seq // SEL_BLOCK
    sel_shift = int(math.log2(SEL_BLOCK))
    base = pl.program_id(1) * SELECT_Q
    lane = lax.broadcasted_iota(jnp.int32, (ts, LANES), 1)
    row = lax.broadcasted_iota(jnp.int32, (ts, LANES), 0)
    jidx = lax.broadcasted_iota(jnp.int32, (n_sel_blocks, ts), 0)
    qcol = lax.broadcasted_iota(jnp.int32, (n_sel_blocks, ts), 1)
    jloc = lax.broadcasted_iota(jnp.int32, (SUBLANES, ts), 0)
    groups = range(N_GROUPS)
    in_group = [(lane >= HEAD_DIM * g) & (lane < HEAD_DIM * (g + 1)) for g in groups]
    first_half = lane < HEAD_DIM
    zero = jnp.zeros((ts, LANES), BF16)
    kcb = kc_ref[0].astype(BF16)
    vcb = vc_ref[0].astype(BF16)
    subs = range(SELECT_Q // ts)
    units = [(s, g) for s in subs for g in groups]

    sc = {}
    for s, g in units:
        qh = [q_ref[s * ts:(s + 1) * ts, h * LANES:(h + 1) * LANES] for h in range(hg)]
        qg = jnp.concatenate([jnp.where(in_group[g], x, zero) for x in qh], axis=0)
        sc[s, g] = _dot_nt(qg, kcb).reshape(hg, ts, LANES)
    p_cmp = {}
    for s, g in units:
        cvalid = (lane * CMP_STRIDE + (CMP_BLOCK - 1)) <= base + s * ts + row
        sv = jnp.where(cvalid[None], sc[s, g], MASK_VALUE)
        m = jnp.max(sv, axis=-1, keepdims=True)
        e = jnp.where(cvalid[None], jnp.exp(sv - m), 0.0)
        den = jnp.sum(e, axis=-1, keepdims=True)
        p_cmp[s, g] = e * (1.0 / jnp.where(den > 0.0, den, 1.0))
    o_cmp = {u: _dot(p_cmp[u].reshape(rows, LANES).astype(BF16), vcb) for u in units}

    imp = {}
    for u in units:
        psum = p_cmp[u][0]
        for h in range(1, hg):
            psum = psum + p_cmp[u][h]
        p_hi = psum.astype(BF16)
        p_lo = (psum - p_hi.astype(F32)).astype(BF16)
        imp[u] = (_dot_nt(ovlt_ref[...], p_hi) + _dot_nt(ovlt_ref[...], p_lo))[0:n_sel_blocks]

    maskq = {}
    for s, g in units:
        t_lane = base + s * ts + qcol
        forced = (jidx == 0) | (jidx == (t_lane >> sel_shift))
        future = jidx * SEL_BLOCK > t_lane
        imp_g = jnp.where(forced, FORCE_SCORE, jnp.where(future, -1.0, imp[s, g]))
        tiles = [imp_g[k * SUBLANES:(k + 1) * SUBLANES] for k in range(n_sel_blocks // SUBLANES)]
        cnts = [jnp.zeros((SUBLANES, ts), jnp.int32) for _ in tiles]
        for i in range(n_sel_blocks):
            other = jnp.broadcast_to(imp_g[i:i + 1, :], (SUBLANES, ts))
            ki, ri = divmod(i, SUBLANES)
            for k, tile_k in enumerate(tiles):
                if k < ki:
                    ahead = other > tile_k
                elif k > ki:
                    ahead = other >= tile_k
                else:
                    ahead = (other > tile_k) | ((other == tile_k) & (jloc > ri))
                cnts[k] = cnts[k] + jnp.where(ahead, 1, 0)
        cnt = jnp.concatenate(cnts, axis=0)
        mask_t = jnp.where(cnt < min(N_SEL, n_sel_blocks), 0.0, MASK_VALUE)
        lo_rows = HEAD_DIM * (1 - g)
        parts = [mask_t, jnp.zeros((LANES - n_sel_blocks - lo_rows, ts), F32)]
        if lo_rows:
            parts = [jnp.zeros((lo_rows, ts), F32)] + parts
        maskq[s, g] = jnp.concatenate(parts, axis=0).T

    for s in subs:
        rs = slice(s * ts, (s + 1) * ts)
        mask_ref[rs, :] = (maskq[s, 0] + maskq[s, 1]).astype(BF16)
        gexp = _dot_hilo(g3_ref[rs, :], egc_ref[...])
        gate_ref[rs, :] = gexp[:, ATTN_WIDTH:]
        for h in range(hg):
            hr = slice(h * ts, (h + 1) * ts)
            both = jnp.where(first_half, o_cmp[s, 0][hr], o_cmp[s, 1][hr])
            ocmp_ref[rs, h * LANES:(h + 1) * LANES] = gexp[:, h * LANES:(h + 1) * LANES] * both


def _select(q, kc, vc, g3, ovlt, egc, bsz, seq):
    tsel = SELECT_Q
    nq = seq // tsel
    nch = kc.shape[1]
    qrow = lambda n: pl.BlockSpec((tsel, n), lambda b, i: (b * nq + i, 0))
    cmp_spec = pl.BlockSpec((1, nch, LANES), lambda b, i: (b, 0, 0))
    full = lambda a: pl.BlockSpec(a.shape, lambda b, i: (0,) * a.ndim)
    return pl.pallas_call(
        functools.partial(_select_kernel, seq=seq),
        grid=(bsz, nq),
        in_specs=[qrow(ATTN_WIDTH), cmp_spec, cmp_spec, qrow(LANES), full(ovlt), full(egc)],
        out_specs=[qrow(LANES), qrow(ATTN_WIDTH), qrow(2 * ATTN_WIDTH)],
        out_shape=[jax.ShapeDtypeStruct((bsz * seq, LANES), BF16),
                   jax.ShapeDtypeStruct((bsz * seq, ATTN_WIDTH), F32),
                   jax.ShapeDtypeStruct((bsz * seq, 2 * ATTN_WIDTH), F32)],
        compiler_params=pltpu.CompilerParams(vmem_limit_bytes=VMEM_LIMIT),
        name="select",
    )(q, kc, vc, g3, ovlt, egc)


def _attn_kernel(q_ref, mask_ref, ocmp_ref, ks0_ref, ks1_ref, vs0_ref, vs1_ref, kw_ref, vw0_ref, vw1_ref,
                 gate_ref, sz_ref, wbias_ref, dbias_ref, o_ref, slc_ref, win_ref, *, seq):
    tq, tk = ATTN_Q, ATTN_K
    hg = HEADS_PER_GROUP
    rows = hg * tq
    q0 = pl.program_id(1) * tq
    lane = lax.broadcasted_iota(jnp.int32, (tq, LANES), 1)

    span = WINDOW + tq
    start = pl.multiple_of(jnp.maximum(q0 - WINDOW, 0), tq)
    wbias = wbias_ref[0]
    width = ATTN_WIDTH
    ks_refs, vs_refs, vw_refs = (ks0_ref, ks1_ref), (vs0_ref, vs1_ref), (vw0_ref, vw1_ref)

    groups = range(N_GROUPS)
    in_group = [(lane >= HEAD_DIM * g) & (lane < HEAD_DIM * (g + 1)) for g in groups]
    zero = jnp.zeros((tq, LANES), BF16)
    q_heads = [q_ref[:, h * LANES:(h + 1) * LANES] for h in range(hg)]
    qg = [jnp.concatenate([jnp.where(in_group[g], qh, zero) for qh in q_heads], axis=0) for g in groups]
    maskq = mask_ref[...]
    qsel = [jnp.concatenate([jnp.where(in_group[g], qh, maskq) for qh in q_heads], axis=0) for g in groups]

    n_tiles = q0 // tk + 1
    for n in range(1, seq // tk + 1):
        @pl.when(n_tiles == n)
        def _(n=n):
            klen = n * tk
            s2 = [_dot_nt(qsel[g], ks_refs[g][0:klen, :]) for g in groups]
            s3 = [_dot_nt(qg[g], kw_ref[pl.ds(start, span), :]).reshape(hg, tq, span) for g in groups]
            for g in groups:
                last = (s2[g][:, klen - tk:].reshape(hg, tq, tk) + dbias_ref[0][None]).reshape(rows, tk)
                sg = last if n == 1 else jnp.concatenate([s2[g][:, :klen - tk], last], axis=1)
                m2 = jnp.max(sg, axis=-1, keepdims=True)
                p2 = jnp.exp(sg - m2).astype(BF16)
                slc_ref[g] = _dot(p2, vs_refs[g][0:klen, :])
            for g in groups:
                sw = s3[g] + wbias[None]
                m3 = jnp.max(sw, axis=-1, keepdims=True)
                p3 = jnp.exp(sw - m3).astype(BF16).reshape(rows, span)
                win_ref[g] = _dot(p3, vw_refs[g][pl.ds(start, span), :])

    o_slc, o_win = [], []
    for g in groups:
        o_slc.append(slc_ref[g, :, :LANES] * (1.0 / slc_ref[g, :, LANES:]))
        o_win.append(win_ref[g, :, :LANES] * (1.0 / win_ref[g, :, LANES:]))

    first_half = lane < HEAD_DIM
    for h in range(hg):
        rws = slice(h * tq, (h + 1) * tq)
        cols = slice(h * LANES, (h + 1) * LANES)
        acc = ocmp_ref[:, cols]
        for j, branch in enumerate((o_slc, o_win)):
            both = jnp.where(first_half, branch[0][rws], branch[1][rws])
            acc = acc + gate_ref[:, j * width + h * LANES: j * width + (h + 1) * LANES] * both
        o_ref[:, cols] = (acc * sz_ref[:, cols].astype(F32)).astype(BF16)


def _attn(q, maskq, ocmp, ks0, ks1, vs0, vs1, kw, vw0, vw1, gates, sz, wbias, dbias, bsz, seq):
    tq, tk = ATTN_Q, ATTN_K
    nq = seq // tq
    qrow = lambda n: pl.BlockSpec((tq, n), lambda b, i: (b * nq + i, 0))
    per_seq = pl.BlockSpec((seq, LANES), lambda b, i: (b, 0))
    val_seq = pl.BlockSpec((seq, 2 * LANES), lambda b, i: (b, 0))
    diag =pl.BlockSpec((1, tq, tk), lambda b, i: (i % (tk // tq), 0, 0))
    wspec = pl.BlockSpec((1,) + wbias.shape[1:], lambda b, i: (jnp.minimum(i, wbias.shape[0] - 1), 0, 0))
    return pl.pallas_call(
        functools.partial(_attn_kernel, seq=seq),
        grid=(bsz, nq),
        in_specs=[qrow(ATTN_WIDTH), qrow(LANES), qrow(ATTN_WIDTH), per_seq, per_seq, val_seq, val_seq, per_seq,
                  val_seq, val_seq]
                 + [qrow(2 * ATTN_WIDTH), qrow(ATTN_WIDTH), wspec, diag],
        out_specs=qrow(ATTN_WIDTH),
        out_shape=jax.ShapeDtypeStruct((bsz * seq, ATTN_WIDTH), BF16),
        scratch_shapes=[pltpu.VMEM((N_GROUPS, HEADS_PER_GROUP * tq, 2 * LANES), F32)] * 2,
        compiler_params=pltpu.CompilerParams(vmem_limit_bytes=VMEM_LIMIT),
        name="attn",
    )(q, maskq, ocmp, ks0, ks1, vs0, vs1, kw, vw0, vw1, gates, sz, wbias, dbias)


def _ssm_kernel(u_ref, mi_ref, ws_ref, wo_ref, al_ref, y_ref, sx_ref, *, n_chunks):
    u = u_ref[0]
    n_tiles = sx_ref.shape[0]
    nt = n_tiles // 2
    tile = lambda j: slice(j * LANES, (j + 1) * LANES)
    for j in range(n_tiles):
        sx_ref[j] = _dot(u, ws_ref[0, :, tile(j)])
    a_re = [al_ref[0, 0:1, tile(j)] for j in range(nt)]
    a_im = [al_ref[0, 1:2, tile(j)] for j in range(nt)]
    bt = SSM_BT

    def body(k, carry):
        rows = pl.ds(k, bt, stride=n_chunks)
        new = []
        for j in range(nt):
            x_re, x_im = carry[j], carry[nt + j]
            s_re = sx_ref[j, rows, :]
            s_im = sx_ref[nt + j, rows, :]
            sx_ref[j, rows, :] = x_re
            sx_ref[nt + j, rows, :] = x_im
            new.append((a_re[j] * x_re - a_im[j] * x_im + s_re, a_re[j] * x_im + a_im[j] * x_re + s_im))
        return tuple([n[0] for n in new] + [n[1] for n in new])

    zero = jnp.zeros((bt, LANES), F32)
    lax.fori_loop(0, n_chunks, body, (zero,) * n_tiles)
    width = u.shape[1]
    ctile = 2 * LANES
    y_cols = [_dot(u[:, :(c + 1) * ctile], mi_ref[0, :(c + 1) * ctile, c * ctile:(c + 1) * ctile])
              for c in range(width // ctile)]
    xs = jnp.concatenate([sx_ref[j].astype(BF16) for j in range(n_tiles)], axis=1)
    y = jnp.concatenate(y_cols, axis=1) + _dot(xs, wo_ref[0])
    for j in range(SSM_CHUNK):
        y_ref[0, pl.ds(j, y.shape[0], stride=SSM_CHUNK), :] = y[:, tile(j)]


def _ssm(u3, mi, ws, wo, al, n_chunks):
    halves, rows, width = u3.shape
    r = SSM_BT * n_chunks
    wspec = lambda a: pl.BlockSpec((1,) + a.shape[1:], lambda hf, i: (hf, 0, 0))
    return pl.pallas_call(
        functools.partial(_ssm_kernel, n_chunks=n_chunks),
        grid=(halves, rows // r),
        in_specs=[pl.BlockSpec((1, r, width), lambda hf, i: (hf, i, 0)),
                  wspec(mi), wspec(ws), wspec(wo), wspec(al)],
        out_specs=pl.BlockSpec((1, r * SSM_CHUNK, LANES), lambda hf, i: (hf, i, 0)),
        out_shape=jax.ShapeDtypeStruct((halves, rows * SSM_CHUNK, LANES), F32),
        scratch_shapes=[pltpu.VMEM((ws.shape[2] // LANES, r, LANES), F32)],
        compiler_params=pltpu.CompilerParams(vmem_limit_bytes=VMEM_LIMIT),
        name="ssm",
    )(u3, mi, ws, wo, al)


def _ssm_params(lam_re, lam_im, log_dt, b_re, b_im, c_re, c_im, d_skip):
    L = SSM_CHUNK
    G, P, C = SSM_GROUPS, SSM_STATE, SSM_GROUP
    gh = G // SSM_HALVES
    dt = jnp.exp(log_dt)[:, None]
    lr, li = lam_re, lam_im
    mag = jnp.exp(lr * dt)
    ab_re = mag * jnp.cos(li * dt)
    ab_im = mag * jnp.sin(li * dt)
    nr, ni = ab_re - 1.0, ab_im
    den = lr * lr + li * li
    cr = ((nr * lr + ni * li) / den)[..., None]
    ci = ((ni * lr - nr * li) / den)[..., None]
    bb_re = cr * b_re - ci * b_im
    bb_im = cr * b_im + ci * b_re
    pr, pi = [jnp.ones_like(ab_re)], [jnp.zeros_like(ab_im)]
    for _ in range(L):
        pr.append(pr[-1] * ab_re - pi[-1] * ab_im)
        pi.append(pr[-2] * ab_im + pi[-1] * ab_re)
    pw_re = jnp.stack(pr)
    pw_im = jnp.stack(pi)
    ca_re = jnp.einsum('gcp,dgp->dgcp', c_re, pw_re[:L]) - jnp.einsum('gcp,dgp->dgcp', c_im, pw_im[:L])
    ca_im = jnp.einsum('gcp,dgp->dgcp', c_re, pw_im[:L]) + jnp.einsum('gcp,dgp->dgcp', c_im, pw_re[:L])
    kern = jnp.einsum('dgop,gpi->dgoi', ca_re, bb_re) - jnp.einsum('dgop,gpi->dgoi', ca_im, bb_im)
    lag = np.arange(L)[None, :] - np.arange(L)[:, None]
    lag_or_zero_block = np.where(lag >= 0, lag, L)
    e_re, e_im = pw_re[:L][::-1], pw_im[:L][::-1]
    ws_re = jnp.einsum('sgp,gpi->sgip', e_re, bb_re) - jnp.einsum('sgp,gpi->sgip', e_im, bb_im)
    ws_im = jnp.einsum('sgp,gpi->sgip', e_re, bb_im) + jnp.einsum('sgp,gpi->sgip', e_im, bb_re)
    o_re, o_im = pw_re[1:L + 1], pw_im[1:L + 1]
    wo_re = jnp.einsum('gop,tgp->gpto', c_re, o_re) - jnp.einsum('gop,tgp->gpto', c_im, o_im)
    wo_im = -(jnp.einsum('gop,tgp->gpto', c_re, o_im) + jnp.einsum('gop,tgp->gpto', c_im, o_re))

    mi_h, ws_h, wo_h, al_h = [], [], [], []
    for hf in range(SSM_HALVES):
        gs = slice(hf * gh, (hf + 1) * gh)
        eye_h = jnp.eye(gh, dtype=F32)
        blocks = jnp.einsum('dgoi,gh->dgiho', kern[:, gs], eye_h).reshape(L, gh * C, gh * C)
        blocks = blocks.at[0].add(jnp.diag(d_skip[hf * gh * C:(hf + 1) * gh * C]))
        blocks = jnp.concatenate([blocks, jnp.zeros((1, gh * C, gh * C), F32)], axis=0)
        mi_h.append(blocks[lag_or_zero_block].transpose(0, 2, 1, 3).reshape(L * gh * C, L * gh * C))
        wsr = jnp.einsum('sgip,gh->sgihp', ws_re[:, gs], eye_h).reshape(L * gh * C, gh * P)
        wsi = jnp.einsum('sgip,gh->sgihp', ws_im[:, gs], eye_h).reshape(L * gh * C, gh * P)
        ws_h.append(jnp.concatenate([wsr, wsi], axis=1))
        wor = jnp.einsum('gpto,gh->gptho', wo_re[gs], eye_h).reshape(gh * P, L * gh * C)
        woi = jnp.einsum('gpto,gh->gptho', wo_im[gs], eye_h).reshape(gh * P, L * gh * C)
        wo_h.append(jnp.concatenate([wor, woi], axis=0))
        al_h.append(jnp.stack([pw_re[L, gs].reshape(gh * P), pw_im[L, gs].reshape(gh * P)]))
    return (jnp.stack(mi_h).astype(BF16), jnp.stack(ws_h).astype(BF16), jnp.stack(wo_h).astype(BF16),
            jnp.stack(al_h))


def _tail_kernel(x_ref, oa_ref, y_ref, szs_ref, gain_ref, scale_ref, shift_ref, gate_ref, wm_ref, wglu_ref,
                 wpa_ref, wps_ref, wout_ref, fg_ref, o_ref):
    x = x_ref[...]
    d = x.shape[1]
    hb = _modulated_norm(x, gain_ref[...], scale_ref[0], shift_ref[0]).astype(BF16)
    mg = _sigmoid(_dot(hb, wm_ref[...]))
    y = jnp.concatenate([y_ref[0], y_ref[1]], axis=-1)
    yy = _dot(_gelu(y).astype(BF16), wglu_ref[...])
    o_ssm = yy[:, :SSM_WIDTH] * _sigmoid(yy[:, SSM_WIDTH:]) * szs_ref[...].astype(F32)
    merged = (mg[:, :d] * _dot(oa_ref[...], wpa_ref[...])
              + mg[:, d:] * _dot(o_ssm.astype(BF16), wps_ref[...]))
    xo = x + gate_ref[0] * _dot(merged.astype(BF16), wout_ref[...])
    ms = jnp.mean(xo * xo, axis=-1, keepdims=True)
    o_ref[...] = xo * lax.rsqrt(ms + NORM_EPS) * fg_ref[...]


def _tail(x2, oa, y3, szs, gain, scale, shift, gate, wm, wglu, wpa, wps, wout, fgain, seq):
    t, d = x2.shape
    tm = PROJ_ROWS
    per_seq = seq // tm
    row = lambda n: pl.BlockSpec((tm, n), lambda i: (i, 0))
    full = lambda a: pl.BlockSpec(a.shape, lambda i: (0,) * a.ndim)
    mod = pl.BlockSpec((1, 1, d), lambda i: (i // per_seq, 0, 0))
    return pl.pallas_call(
        _tail_kernel,
        grid=(t // tm,),
        in_specs=[row(d), row(ATTN_WIDTH), pl.BlockSpec((SSM_HALVES, tm, LANES), lambda i: (0, i, 0)),
                  row(SSM_WIDTH), full(gain), mod, mod, mod,
                  full(wm), full(wglu), full(wpa), full(wps), full(wout), full(fgain)],
        out_specs=row(d),
        out_shape=jax.ShapeDtypeStruct((t, d), F32),
        compiler_params=pltpu.CompilerParams(vmem_limit_bytes=VMEM_LIMIT),
        name="tail",
    )(x2, oa, y3, szs, gain, scale, shift, gate, wm, wglu, wpa, wps, wout, fgain)


def _head_perm():
    idx = np.arange(ATTN_WIDTH)
    h, rem = idx // LANES, idx % LANES
    g, dd = rem // HEAD_DIM, rem % HEAD_DIM
    return (g * HEADS_PER_GROUP + h) * HEAD_DIM + dd


def _rope_tables(seq):
    half = HEAD_DIM // 2
    inv_freq = ROPE_THETA ** (-jnp.arange(half, dtype=F32) / half)
    ang = jnp.arange(seq, dtype=F32)[:, None] * inv_freq[None, :]
    cos, sin = jnp.cos(ang), jnp.sin(ang)
    zero = jnp.zeros_like(sin)
    reps = LANES // HEAD_DIM
    cos_t = jnp.tile(jnp.concatenate([cos, cos], axis=1), (1, reps))
    sa_t = jnp.tile(jnp.concatenate([-sin, zero], axis=1), (1, reps))
    sb_t = jnp.tile(jnp.concatenate([zero, sin], axis=1), (1, reps))
    return cos_t, sa_t, sb_t


def _proj_weight(w_in):
    d = w_in.shape[0]
    o_q, o_kv, o_g, o_z, o_u, o_zs, o_m = 0, 512, 1280, 1304, 1816, 2072, 2328
    perm = _head_perm()
    kv = lambda j: w_in[:, o_kv + j * KV_WIDTH: o_kv + (j + 1) * KV_WIDTH]
    gates = jnp.pad(w_in[:, o_g:o_z], ((0, 0), (0, LANES - 3 * N_HEADS)))
    cols = [w_in[:, o_q:o_kv][:, perm] * (HEAD_DIM ** -0.5),
            kv(0), kv(2), kv(4), kv(1), kv(3), kv(5), gates,
            w_in[:, o_z:o_u][:, perm], w_in[:, o_u:o_zs], w_in[:, o_zs:o_m]]
    w = jnp.concatenate(cols, axis=1)
    assert w.shape == (d, _C_END)
    return w.astype(BF16), w_in[:, o_m:].astype(BF16)


def _attn_constants(seq):
    n_cmp = LANES
    cs = np.arange(n_cmp) * CMP_STRIDE
    ss = np.arange(LANES) * SEL_BLOCK
    ovl = (np.minimum(cs[:, None] + CMP_BLOCK, ss[None, :] + SEL_BLOCK) > np.maximum(cs[:, None], ss[None, :]))
    ovl = ovl & (np.arange(LANES)[None, :] < seq // SEL_BLOCK) & (cs[:, None] + CMP_BLOCK <= seq)
    tq, tk = ATTN_Q, ATTN_K
    v = np.arange(tk // tq)[:, None, None]
    r = np.arange(tq)[None, :, None]
    cc = np.arange(tk)[None, None, :]
    dbias = np.where(cc <= v * tq + r, 0.0, MASK_VALUE).astype(np.float32)
    eg = np.zeros((LANES, 3 * ATTN_WIDTH), np.float32)
    for g in range(N_GROUPS):
        for h in range(HEADS_PER_GROUP):
            for j in range(3):
                base = j * ATTN_WIDTH + h * LANES + g * HEAD_DIM
                eg[(g * HEADS_PER_GROUP + h) * 3 + j, base:base + HEAD_DIM] = 1.0
    span = WINDOW + tq
    q0 = (np.arange(WINDOW // tq + 1) * tq)[:, None, None]
    kp = np.maximum(q0 - WINDOW, 0) + np.arange(span)[None, None, :]
    tt = q0 + np.arange(tq)[None, :, None]
    wbias = np.where((kp <= tt) & (kp > tt - WINDOW), 0.0, MASK_VALUE).astype(np.float32)
    return (jnp.asarray(ovl.T, BF16), jnp.asarray(eg, BF16), jnp.asarray(wbias, F32), jnp.asarray(dbias, F32))


def kernel(x, c, w_ada, b_ada, norm_gain, w_in, pe_cmp_k, w_cmp_k1, b_cmp_k1, w_cmp_k2, pe_cmp_v, w_cmp_v1,
           b_cmp_v1, w_cmp_v2, lam_re, lam_im, log_dt, b_re, b_im, c_re, c_im, d_skip, w_glu, w_proj_attn,
           w_proj_ssm, w_out, final_gain):
    bsz, seq, d = x.shape
    depth = w_in.shape[0]
    assert depth == 1, "the tail kernel fuses the final norm into the (single) layer"
    assert bsz % SSM_BT == 0 and seq % SSM_CHUNK == 0
    assert seq % (ATTN_K) == 0 and seq % PROJ_ROWS == 0 and seq >= WINDOW + ATTN_Q
    assert (seq - CMP_BLOCK) // CMP_STRIDE + 1 <= LANES and seq // CMP_STRIDE == LANES
    t = bsz * seq
    cos_t, sa_t, sb_t = _rope_tables(seq)
    ovlt, eg, wbias, dbias = _attn_constants(seq)
    perm = _head_perm()
    n_chunks = seq // SSM_CHUNK

    x2 = x.reshape(t, d)
    for l in range(depth):
        mod = _ada(c, w_ada[l], b_ada[l])
        shift, mscale, gate = [m.reshape(bsz, 1, d) for m in jnp.split(mod, 3, axis=-1)]
        gain = norm_gain[l].reshape(1, d)
        w_proj, w_merge = _proj_weight(w_in[l])
        (q, kc_r, ks0, ks1, kw, vc_r, vs0, vs1, vw0, vw1, g3, sz, u, szs) = _proj(
            x2, gain, mscale, shift, cos_t, sa_t, sb_t, w_proj, seq)

        nch = seq // CMP_STRIDE
        kc, vc = _compress(
            kc_r.reshape(bsz, nch, CMP_STRIDE * LANES), vc_r.reshape(bsz, nch, CMP_STRIDE * LANES),
            _compress_params(pe_cmp_k[l], w_cmp_k1[l], b_cmp_k1[l], w_cmp_k2[l]),
            _compress_params(pe_cmp_v[l], w_cmp_v1[l], b_cmp_v1[l], w_cmp_v2[l]))
        maskq, ocmp, gates = _select(q, kc, vc, g3, ovlt, eg, bsz, seq)
        o_attn = _attn(q, maskq, ocmp, ks0, ks1, vs0, vs1, kw, vw0, vw1, gates, sz, wbias, dbias, bsz, seq)

        mi, ws, wo, al = _ssm_params(lam_re[l], lam_im[l], log_dt[l], b_re[l], b_im[l], c_re[l], c_im[l],
                                     d_skip[l])
        y3 = _ssm(u, mi, ws, wo, al, n_chunks)

        x2 = _tail(x2, o_attn, y3, szs, gain, mscale, shift, gate, w_merge, w_glu[l].astype(BF16),
                   w_proj_attn[l][perm].astype(BF16), w_proj_ssm[l].astype(BF16), w_out[l].astype(BF16),
                   final_gain.reshape(1, d) if l == depth - 1 else jnp.ones((1, d), F32), seq)
    return x2.reshape(bsz, seq, d)
```

```python
import functools
import math

import jax
import jax.numpy as jnp
import numpy as np
from jax import lax
from jax.experimental import pallas as pl
from jax.experimental.pallas import tpu as pltpu

F32 = jnp.float32
BF16 = jnp.bfloat16

N_HEADS = 8
N_GROUPS = 2
HEADS_PER_GROUP = N_HEADS // N_GROUPS
HEAD_DIM = 64
ATTN_WIDTH = N_HEADS * HEAD_DIM
KV_WIDTH = N_GROUPS * HEAD_DIM
CMP_BLOCK = 32
CMP_STRIDE = 16
CMP_HIDDEN = 2 * HEAD_DIM
SEL_BLOCK = 64
N_SEL = 16
WINDOW = 512
ROPE_THETA = 10000.0
FORCE_SCORE = 1.0e4
MASK_VALUE = -1.0e30
SSM_WIDTH = 256
SSM_GROUP = 16
SSM_GROUPS = SSM_WIDTH // SSM_GROUP
SSM_STATE = 64
NORM_EPS = 1e-6

LANES = 128
SUBLANES = 8
VMEM_LIMIT = 56 * 1024 * 1024

PROJ_ROWS = 1024
ATTN_Q = 256
ATTN_K = 512
SELECT_Q = 512
SELECT_SUB = 128
SSM_CHUNK = 8
SSM_HALVES = 2
SSM_BT = 8
SSM_UNROLL = 4


def _dot(a, b):
    return jnp.dot(a, b, preferred_element_type=F32)


def _dot_nt(a, b):
    return lax.dot_general(a, b, (((1,), (1,)), ((), ())), preferred_element_type=F32)


def _dot_hilo(a, b_bf16):
    hi = a.astype(BF16)
    lo = (a - hi.astype(F32)).astype(BF16)
    return _dot(hi, b_bf16) + _dot(lo, b_bf16)


def _sigmoid(x):
    return 1.0 / (1.0 + jnp.exp(-x))


def _silu(x):
    return x * _sigmoid(x)


def _gelu(x):
    return 0.5 * x * (1.0 + jnp.tanh(math.sqrt(2.0 / math.pi) * (x + 0.044715 * (x * x * x))))


def _ada_kernel(c_ref, w_ref, b_ref, o_ref):
    c = c_ref[...]
    o_ref[...] = _dot(_silu(c).astype(BF16), w_ref[...].astype(BF16)) + b_ref[...]


def _ada(c, w, b):
    bsz, d = c.shape
    n = w.shape[1]
    return pl.pallas_call(
        _ada_kernel,
        grid=(n // d,),
        in_specs=[pl.BlockSpec((bsz, d), lambda j: (0, 0)),
                  pl.BlockSpec((d, d), lambda j: (0, j)),
                  pl.BlockSpec((1, d), lambda j: (0, j))],
        out_specs=pl.BlockSpec((bsz, d), lambda j: (0, j)),
        out_shape=jax.ShapeDtypeStruct((bsz, n), F32),
        compiler_params=pltpu.CompilerParams(vmem_limit_bytes=VMEM_LIMIT),
        name="ada",
    )(c, w, b.reshape(1, n))


_C_Q = 0
_C_KC = 512
_C_KS = 640
_C_KW = 768
_C_VC = 896
_C_VS = 1024
_C_VW = 1152
_C_G3 = 1280
_C_Z = 1408
_C_U = 1920
_C_ZS = 2176
_C_END = 2432


def _modulated_norm(x, gain, scale, shift):
    ms = jnp.mean(x * x, axis=-1, keepdims=True)
    return x * lax.rsqrt(ms + NORM_EPS) * gain * (1.0 + scale) + shift


def _proj_kernel(x_ref, gain_ref, scale_ref, shift_ref, cos_ref, sa_ref, sb_ref, w_ref,
                 q_ref, kc_ref, ks0_ref, ks1_ref, kw_ref, vc_ref, vs0_ref, vs1_ref, vw0_ref, vw1_ref,
                 g3_ref, sz_ref, u_ref, szs_ref, stage_ref, *, per_seq):
    x = x_ref[...]
    hb = _modulated_norm(x, gain_ref[...], scale_ref[0], shift_ref[0]).astype(BF16)

    def proj(a, b):
        return _dot(hb, w_ref[:, a:b])

    cos = cos_ref[...]
    sa = sa_ref[...]
    sb = sb_ref[...]

    def rope(t):
        return t * cos + pltpu.roll(t, LANES - 32, 1) * sa + pltpu.roll(t, 32, 1) * sb

    narrow = proj(_C_Q, _C_Z)
    part = lambda a: narrow[:, a:a + LANES]
    for v in range(ATTN_WIDTH // LANES):
        q_ref[:, v * LANES:(v + 1) * LANES] = rope(part(_C_Q + v * LANES)).astype(BF16)
    kw_ref[...] = rope(part(_C_KW)).astype(BF16)
    tm = x.shape[0]
    for out_ref, val in ((kc_ref, rope(part(_C_KC))), (vc_ref, part(_C_VC))):
        stage_ref[...] = val
        for j in range(CMP_STRIDE):
            out_ref[:, j * LANES:(j + 1) * LANES] = stage_ref[pl.ds(j, tm // CMP_STRIDE, stride=CMP_STRIDE), :]
    lane = lax.broadcasted_iota(jnp.int32, (tm, LANES), 1)
    pos = (pl.program_id(0) % per_seq) * tm + lax.broadcasted_iota(jnp.int32, (tm, LANES), 0)
    blk = pos >> int(math.log2(SEL_BLOCK))
    first = lane < HEAD_DIM
    ks = rope(part(_C_KS))
    ks0_ref[...] = jnp.where(first, ks, jnp.where(lane - HEAD_DIM == blk, 1.0, 0.0)).astype(BF16)
    ks1_ref[...] = jnp.where(first, jnp.where(lane == blk, 1.0, 0.0), ks).astype(BF16)
    ones = jnp.ones((tm, LANES), BF16)
    zero = jnp.zeros((tm, LANES), F32)
    for v_val, refs in ((part(_C_VS), (vs0_ref, vs1_ref)), (part(_C_VW), (vw0_ref, vw1_ref))):
        refs[0][:, :LANES] = jnp.where(first, v_val, zero).astype(BF16)
        refs[1][:, :LANES] = jnp.where(first, zero, v_val).astype(BF16)
        refs[0][:, LANES:] = ones
        refs[1][:, LANES:] = ones
    g3_ref[...] = _sigmoid(part(_C_G3))
    sz_ref[...] = _silu(proj(_C_Z, _C_U)).astype(BF16)
    u = proj(_C_U, _C_ZS)
    for hf in range(SSM_HALVES):
        stage_ref[...] = u[:, hf * LANES:(hf + 1) * LANES]
        for j in range(SSM_CHUNK):
            u_ref[hf, :, j * LANES:(j + 1) * LANES] = stage_ref[
                pl.ds(j, tm // SSM_CHUNK, stride=SSM_CHUNK), :].astype(BF16)
    szs_ref[...] = _silu(proj(_C_ZS, _C_END)).astype(BF16)


def _proj(x2, gain, scale, shift, cos_t, sa_t, sb_t, w, seq):
    t, d = x2.shape
    tm = PROJ_ROWS
    per_seq = seq // tm
    row = lambda n: pl.BlockSpec((tm, n), lambda i: (i, 0))
    mod = pl.BlockSpec((1, 1, d), lambda i: (i // per_seq, 0, 0))
    tab = pl.BlockSpec((tm, LANES), lambda i: (i % per_seq, 0))
    cmp_w, ssm_w = CMP_STRIDE * LANES, SSM_CHUNK * LANES
    out_shape = [
        jax.ShapeDtypeStruct((t, ATTN_WIDTH), BF16),
        jax.ShapeDtypeStruct((t // CMP_STRIDE, cmp_w), F32),
        jax.ShapeDtypeStruct((t, LANES), BF16),
        jax.ShapeDtypeStruct((t, LANES), BF16),
        jax.ShapeDtypeStruct((t, LANES), BF16),
        jax.ShapeDtypeStruct((t // CMP_STRIDE, cmp_w), F32),
        jax.ShapeDtypeStruct((t, 2 * LANES), BF16),
        jax.ShapeDtypeStruct((t, 2 * LANES), BF16),
        jax.ShapeDtypeStruct((t, 2 * LANES), BF16),
        jax.ShapeDtypeStruct((t, 2 * LANES), BF16),
        jax.ShapeDtypeStruct((t, LANES), F32),
        jax.ShapeDtypeStruct((t, ATTN_WIDTH), BF16),
        jax.ShapeDtypeStruct((SSM_HALVES, t // SSM_CHUNK, ssm_w), BF16),
        jax.ShapeDtypeStruct((t, SSM_WIDTH), BF16),
    ]
    cmp_rows = pl.BlockSpec((tm // CMP_STRIDE, cmp_w), lambda i: (i, 0))
    out_specs = [row(ATTN_WIDTH), cmp_rows, row(LANES), row(LANES), row(LANES), cmp_rows] + [row(2 * LANES)] * 4 + [row(LANES)] + [
                 row(ATTN_WIDTH),
                 pl.BlockSpec((SSM_HALVES, tm // SSM_CHUNK, ssm_w), lambda i: (0, i, 0)),
                 row(SSM_WIDTH)]
    return pl.pallas_call(
        functools.partial(_proj_kernel, per_seq=per_seq),
        grid=(t // tm,),
        in_specs=[row(d), pl.BlockSpec((1, d), lambda i: (0, 0)), mod, mod, tab, tab, tab,
                  pl.BlockSpec((d, _C_END), lambda i: (0, 0))],
        out_specs=out_specs,
        out_shape=out_shape,
        scratch_shapes=[pltpu.VMEM((tm, LANES), F32)],
        compiler_params=pltpu.CompilerParams(vmem_limit_bytes=VMEM_LIMIT),
        name="proj",
    )(x2, gain, scale, shift, cos_t, sa_t, sb_t, w)


def _compress_kernel(rk_ref, rv_ref, pea_k, peb_k, wa_k, wb_k, b1_k, w2_k,
                     pea_v, peb_v, wa_v, wb_v, b1_v, w2_v, kc_ref, vc_ref):
    def one(r_ref, pea, peb, wa, wb, b1, w2):
        r = r_ref[0]
        first = _dot((r + pea[...]).astype(BF16), wa[...])
        second = _dot((r + peb[...]).astype(BF16), wb[...])
        nxt = pltpu.roll(second, second.shape[0] - 1, 0)
        hid = _gelu(first + nxt + b1[...])
        return _dot(hid.astype(BF16), w2[...])

    kc_ref[0] = one(rk_ref, pea_k, peb_k, wa_k, wb_k, b1_k, w2_k)
    vc_ref[0] = one(rv_ref, pea_v, peb_v, wa_v, wb_v, b1_v, w2_v)


def _compress(rk, rv, pk, pv):
    bsz, nch, width = rk.shape
    full = lambda a: pl.BlockSpec(a.shape, lambda b: (0,) * a.ndim)
    rspec = pl.BlockSpec((1, nch, width), lambda b: (b, 0, 0))
    ospec = pl.BlockSpec((1, nch, LANES), lambda b: (b, 0, 0))
    return pl.pallas_call(
        _compress_kernel,
        grid=(bsz,),
        in_specs=[rspec, rspec] + [full(a) for a in pk] + [full(a) for a in pv],
        out_specs=[ospec, ospec],
        out_shape=[jax.ShapeDtypeStruct((bsz, nch, LANES), F32)] * 2,
        compiler_params=pltpu.CompilerParams(vmem_limit_bytes=VMEM_LIMIT),
        name="compress",
    )(rk, rv, *pk, *pv)


def _compress_params(pe, w1, b1, w2):
    half = CMP_BLOCK // 2
    eye = jnp.eye(N_GROUPS, dtype=F32)
    w1r = w1.reshape(CMP_BLOCK, HEAD_DIM, CMP_HIDDEN)

    def expand(wl):
        return jnp.einsum('ldj,gh->lgdhj', wl, eye).reshape(half * KV_WIDTH, N_GROUPS * CMP_HIDDEN)

    def pe_lanes(p):
        return jnp.broadcast_to(p[:, None, :], (half, N_GROUPS, HEAD_DIM)).reshape(1, half * KV_WIDTH)

    w2e = jnp.einsum('jd,gh->gjhd', w2, eye).reshape(N_GROUPS * CMP_HIDDEN, KV_WIDTH)
    return (pe_lanes(pe[:half]), pe_lanes(pe[half:]),
            expand(w1r[:half]).astype(BF16), expand(w1r[half:]).astype(BF16),
            jnp.tile(b1, N_GROUPS).reshape(1, N_GROUPS * CMP_HIDDEN), w2e.astype(BF16))


def _select_kernel(q_ref, kc_ref, vc_ref, g3_ref, ovlt_ref, egc_ref, mask_ref, ocmp_ref, gate_ref, *, seq):
    ts = SELECT_SUB
    hg = HEADS_PER_GROUP
    rows = hg * ts
    n_sel_blocks = seq // SEL_BLOCK
    sel_shift = int(math.log2(SEL_BLOCK))
    base = pl.program_id(1) * SELECT_Q
    lane = lax.broadcasted_iota(jnp.int32, (ts, LANES), 1)
    row = lax.broadcasted_iota(jnp.int32, (ts, LANES), 0)
    jidx = lax.broadcasted_iota(jnp.int32, (n_sel_blocks, ts), 0)
    qcol = lax.broadcasted_iota(jnp.int32, (n_sel_blocks, ts), 1)
    jloc = lax.broadcasted_iota(jnp.int32, (SUBLANES, ts), 0)
    groups = range(N_GROUPS)
    in_group = [(lane >= HEAD_DIM * g) & (lane < HEAD_DIM * (g + 1)) for g in groups]
    first_half = lane < HEAD_DIM
    zero = jnp.zeros((ts, LANES), BF16)
    kcb = kc_ref[0].astype(BF16)
    vcb = vc_ref[0].astype(BF16)
    subs = range(SELECT_Q // ts)
    units = [(s, g) for s in subs for g in groups]
    sc, p_cmp, o_cmp, imp, maskq = {}, {}, {}, {}, {}

    def scores(u):
        s, g = u
        qh = [q_ref[s * ts:(s + 1) * ts, h * LANES:(h + 1) * LANES] for h in range(hg)]
        qg = jnp.concatenate([jnp.where(in_group[g], x, zero) for x in qh], axis=0)
        sc[u] = _dot_nt(qg, kcb).reshape(hg, ts, LANES)

    def softmax(u):
        s, g = u
        cvalid = (lane * CMP_STRIDE + (CMP_BLOCK - 1)) <= base + s * ts + row
        sv = jnp.where(cvalid[None], sc[u], MASK_VALUE)
        m = jnp.max(sv, axis=-1, keepdims=True)
        e = jnp.where(cvalid[None], jnp.exp(sv - m), 0.0)
        den = jnp.sum(e, axis=-1, keepdims=True)
        p_cmp[u] = e * (1.0 / jnp.where(den > 0.0, den, 1.0))

    def outputs(u):
        o_cmp[u] = _dot(p_cmp[u].reshape(rows, LANES).astype(BF16), vcb)
        psum = p_cmp[u][0]
        for h in range(1, hg):
            psum = psum + p_cmp[u][h]
        p_hi = psum.astype(BF16)
        p_lo = (psum - p_hi.astype(F32)).astype(BF16)
        imp[u] = (_dot_nt(ovlt_ref[...], p_hi) + _dot_nt(ovlt_ref[...], p_lo))[0:n_sel_blocks]

    def rank(u):
        s, g = u
        t_lane = base + s * ts + qcol
        forced = (jidx == 0) | (jidx == (t_lane >> sel_shift))
        future = jidx * SEL_BLOCK > t_lane
        imp_g = jnp.where(forced, FORCE_SCORE, jnp.where(future, -1.0, imp[u]))
        tiles = [imp_g[k * SUBLANES:(k + 1) * SUBLANES] for k in range(n_sel_blocks // SUBLANES)]
        cnts = [jnp.zeros((SUBLANES, ts), jnp.int32) for _ in tiles]
        for i in range(n_sel_blocks):
            other = jnp.broadcast_to(imp_g[i:i + 1, :], (SUBLANES, ts))
            ki, ri = divmod(i, SUBLANES)
            for k, tile_k in enumerate(tiles):
                if k < ki:
                    ahead = other > tile_k
                elif k > ki:
                    ahead = other >= tile_k
                else:
                    ahead = (other > tile_k) | ((other == tile_k) & (jloc > ri))
                cnts[k] = cnts[k] + jnp.where(ahead, 1, 0)
        cnt = jnp.concatenate(cnts, axis=0)
        mask_t = jnp.where(cnt < min(N_SEL, n_sel_blocks), 0.0, MASK_VALUE)
        lo_rows = HEAD_DIM * (1 - g)
        parts = [mask_t, jnp.zeros((LANES - n_sel_blocks - lo_rows, ts), F32)]
        if lo_rows:
            parts = [jnp.zeros((lo_rows, ts), F32)] + parts
        maskq[u] = jnp.concatenate(parts, axis=0).T

    def store(s):
        rs = slice(s * ts, (s + 1) * ts)
        mask_ref[rs, :] = (maskq[s, 0] + maskq[s, 1]).astype(BF16)
        gexp = _dot_hilo(g3_ref[rs, :], egc_ref[...])
        gate_ref[rs, :] = gexp[:, ATTN_WIDTH:]
        for h in range(hg):
            hr = slice(h * ts, (h + 1) * ts)
            both = jnp.where(first_half, o_cmp[s, 0][hr], o_cmp[s, 1][hr])
            ocmp_ref[rs, h * LANES:(h + 1) * LANES] = gexp[:, h * LANES:(h + 1) * LANES] * both

    stages = (scores, softmax, outputs, rank)
    for step in range(len(units) + len(stages) - 1):
        for k, stage in enumerate(stages):
            if 0 <= step - k < len(units):
                s, g = units[step - k]
                stage((s, g))
                if stage is rank and g == N_GROUPS - 1:
                    store(s)


def _select(q, kc, vc, g3, ovlt, egc, bsz, seq):
    tsel = SELECT_Q
    nq = seq // tsel
    nch = kc.shape[1]
    qrow = lambda n: pl.BlockSpec((tsel, n), lambda b, i: (b * nq + i, 0))
    cmp_spec = pl.BlockSpec((1, nch, LANES), lambda b, i: (b, 0, 0))
    full = lambda a: pl.BlockSpec(a.shape, lambda b, i: (0,) * a.ndim)
    return pl.pallas_call(
        functools.partial(_select_kernel, seq=seq),
        grid=(bsz, nq),
        in_specs=[qrow(ATTN_WIDTH), cmp_spec, cmp_spec, qrow(LANES), full(ovlt), full(egc)],
        out_specs=[qrow(LANES), qrow(ATTN_WIDTH), qrow(2 * ATTN_WIDTH)],
        out_shape=[jax.ShapeDtypeStruct((bsz * seq, LANES), BF16),
                   jax.ShapeDtypeStruct((bsz * seq, ATTN_WIDTH), F32),
                   jax.ShapeDtypeStruct((bsz * seq, 2 * ATTN_WIDTH), F32)],
        compiler_params=pltpu.CompilerParams(vmem_limit_bytes=VMEM_LIMIT),
        name="select",
    )(q, kc, vc, g3, ovlt, egc)


def _attn_kernel(q_ref, mask_ref, ocmp_ref, ks0_ref, ks1_ref, vs0_ref, vs1_ref, kw_ref, vw0_ref, vw1_ref,
                 gate_ref, sz_ref, wbias_ref, dbias_ref, o_ref, slc_ref, win_ref, *, seq):
    tq, tk = ATTN_Q, ATTN_K
    hg = HEADS_PER_GROUP
    rows = hg * tq
    q0 = pl.program_id(1) * tq
    lane = lax.broadcasted_iota(jnp.int32, (tq, LANES), 1)

    span = WINDOW + tq
    start = pl.multiple_of(jnp.maximum(q0 - WINDOW, 0), tq)
    wbias = wbias_ref[0]
    width = ATTN_WIDTH
    ks_refs, vs_refs, vw_refs = (ks0_ref, ks1_ref), (vs0_ref, vs1_ref), (vw0_ref, vw1_ref)

    groups = range(N_GROUPS)
    in_group = [(lane >= HEAD_DIM * g) & (lane < HEAD_DIM * (g + 1)) for g in groups]
    zero = jnp.zeros((tq, LANES), BF16)
    q_heads = [q_ref[:, h * LANES:(h + 1) * LANES] for h in range(hg)]
    qg = [jnp.concatenate([jnp.where(in_group[g], qh, zero) for qh in q_heads], axis=0) for g in groups]
    maskq = mask_ref[...]
    qsel = [jnp.concatenate([jnp.where(in_group[g], qh, maskq) for qh in q_heads], axis=0) for g in groups]

    n_tiles = q0 // tk + 1
    for n in range(1, seq // tk + 1):
        @pl.when(n_tiles == n)
        def _(n=n):
            klen = n * tk
            s2 = [_dot_nt(qsel[g], ks_refs[g][0:klen, :]) for g in groups]
            s3 = [_dot_nt(qg[g], kw_ref[pl.ds(start, span), :]).reshape(hg, tq, span) for g in groups]
            for g in groups:
                last = (s2[g][:, klen - tk:].reshape(hg, tq, tk) + dbias_ref[0][None]).reshape(rows, tk)
                sg = last if n == 1 else jnp.concatenate([s2[g][:, :klen - tk], last], axis=1)
                m2 = jnp.max(sg, axis=-1, keepdims=True)
                p2 = jnp.exp(sg - m2).astype(BF16)
                slc_ref[g] = _dot(p2, vs_refs[g][0:klen, :])
            for g in groups:
                sw = s3[g] + wbias[None]
                m3 = jnp.max(sw, axis=-1, keepdims=True)
                p3 = jnp.exp(sw - m3).astype(BF16).reshape(rows, span)
                win_ref[g] = _dot(p3, vw_refs[g][pl.ds(start, span), :])

    o_slc, o_win = [], []
    for g in groups:
        o_slc.append(slc_ref[g, :, :LANES] * (1.0 / slc_ref[g, :, LANES:]))
        o_win.append(win_ref[g, :, :LANES] * (1.0 / win_ref[g, :, LANES:]))

    first_half = lane < HEAD_DIM
    for h in range(hg):
        rws = slice(h * tq, (h + 1) * tq)
        cols = slice(h * LANES, (h + 1) * LANES)
        acc = ocmp_ref[:, cols]
        for j, branch in enumerate((o_slc, o_win)):
            both = jnp.where(first_half, branch[0][rws], branch[1][rws])
            acc = acc + gate_ref[:, j * width + h * LANES: j * width + (h + 1) * LANES] * both
        o_ref[:, cols] = (acc * sz_ref[:, cols].astype(F32)).astype(BF16)


def _attn(q, maskq, ocmp, ks0, ks1, vs0, vs1, kw, vw0, vw1, gates, sz, wbias, dbias, bsz, seq):
    tq, tk = ATTN_Q, ATTN_K
    nq = seq // tq
    qrow = lambda n: pl.BlockSpec((tq, n), lambda b, i: (b * nq + i, 0))
    per_seq = pl.BlockSpec((seq, LANES), lambda b, i: (b, 0))
    val_seq = pl.BlockSpec((seq, 2 * LANES), lambda b, i: (b, 0))
    diag =pl.BlockSpec((1, tq, tk), lambda b, i: (i % (tk // tq), 0, 0))
    wspec = pl.BlockSpec((1,) + wbias.shape[1:], lambda b, i: (jnp.minimum(i, wbias.shape[0] - 1), 0, 0))
    return pl.pallas_call(
        functools.partial(_attn_kernel, seq=seq),
        grid=(bsz, nq),
        in_specs=[qrow(ATTN_WIDTH), qrow(LANES), qrow(ATTN_WIDTH), per_seq, per_seq, val_seq, val_seq, per_seq,
                  val_seq, val_seq]
                 + [qrow(2 * ATTN_WIDTH), qrow(ATTN_WIDTH), wspec, diag],
        out_specs=qrow(ATTN_WIDTH),
        out_shape=jax.ShapeDtypeStruct((bsz * seq, ATTN_WIDTH), BF16),
        scratch_shapes=[pltpu.VMEM((N_GROUPS, HEADS_PER_GROUP * tq, 2 * LANES), F32)] * 2,
        compiler_params=pltpu.CompilerParams(vmem_limit_bytes=VMEM_LIMIT),
        name="attn",
    )(q, maskq, ocmp, ks0, ks1, vs0, vs1, kw, vw0, vw1, gates, sz, wbias, dbias)


def _ssm_kernel(u_ref, mi_ref, ws_ref, wo_ref, al_ref, y_ref, sx_ref, *, n_chunks):
    u = u_ref[0]
    n_tiles = sx_ref.shape[0]
    nt = n_tiles // 2
    tile = lambda j: slice(j * LANES, (j + 1) * LANES)
    for j in range(n_tiles):
        sx_ref[j] = _dot(u, ws_ref[0, :, tile(j)])
    a_re = [al_ref[0, 0:1, tile(j)] for j in range(nt)]
    a_im = [al_ref[0, 1:2, tile(j)] for j in range(nt)]
    bt = SSM_BT

    def body(k, carry):
        rows = pl.ds(k, bt, stride=n_chunks)
        new = []
        for j in range(nt):
            x_re, x_im = carry[j], carry[nt + j]
            s_re = sx_ref[j, rows, :]
            s_im = sx_ref[nt + j, rows, :]
            sx_ref[j, rows, :] = x_re
            sx_ref[nt + j, rows, :] = x_im
            new.append((a_re[j] * x_re - a_im[j] * x_im + s_re, a_re[j] * x_im + a_im[j] * x_re + s_im))
        return tuple([n[0] for n in new] + [n[1] for n in new])

    zero = jnp.zeros((bt, LANES), F32)
    lax.fori_loop(0, n_chunks, body, (zero,) * n_tiles, unroll=SSM_UNROLL)
    width = u.shape[1]
    ctile = 2 * LANES
    y_cols = [_dot(u[:, :(c + 1) * ctile], mi_ref[0, :(c + 1) * ctile, c * ctile:(c + 1) * ctile])
              for c in range(width // ctile)]
    xs = jnp.concatenate([sx_ref[j].astype(BF16) for j in range(n_tiles)], axis=1)
    y = jnp.concatenate(y_cols, axis=1) + _dot(xs, wo_ref[0])
    for j in range(SSM_CHUNK):
        y_ref[0, pl.ds(j, y.shape[0], stride=SSM_CHUNK), :] = y[:, tile(j)]


def _ssm(u3, mi, ws, wo, al, n_chunks):
    halves, rows, width = u3.shape
    r = SSM_BT * n_chunks
    wspec = lambda a: pl.BlockSpec((1,) + a.shape[1:], lambda hf, i: (hf, 0, 0))
    return pl.pallas_call(
        functools.partial(_ssm_kernel, n_chunks=n_chunks),
        grid=(halves, rows // r),
        in_specs=[pl.BlockSpec((1, r, width), lambda hf, i: (hf, i, 0)),
                  wspec(mi), wspec(ws), wspec(wo), wspec(al)],
        out_specs=pl.BlockSpec((1, r * SSM_CHUNK, LANES), lambda hf, i: (hf, i, 0)),
        out_shape=jax.ShapeDtypeStruct((halves, rows * SSM_CHUNK, LANES), F32),
        scratch_shapes=[pltpu.VMEM((ws.shape[2] // LANES, r, LANES), F32)],
        compiler_params=pltpu.CompilerParams(vmem_limit_bytes=VMEM_LIMIT),
        name="ssm",
    )(u3, mi, ws, wo, al)


def _ssm_params(lam_re, lam_im, log_dt, b_re, b_im, c_re, c_im, d_skip):
    L = SSM_CHUNK
    G, P, C = SSM_GROUPS, SSM_STATE, SSM_GROUP
    gh = G // SSM_HALVES
    dt = jnp.exp(log_dt)[:, None]
    lr, li = lam_re, lam_im
    mag = jnp.exp(lr * dt)
    ab_re = mag * jnp.cos(li * dt)
    ab_im = mag * jnp.sin(li * dt)
    nr, ni = ab_re - 1.0, ab_im
    den = lr * lr + li * li
    cr = ((nr * lr + ni * li) / den)[..., None]
    ci = ((ni * lr - nr * li) / den)[..., None]
    bb_re = cr * b_re - ci * b_im
    bb_im = cr * b_im + ci * b_re
    pr, pi = [jnp.ones_like(ab_re)], [jnp.zeros_like(ab_im)]
    for _ in range(L):
        pr.append(pr[-1] * ab_re - pi[-1] * ab_im)
        pi.append(pr[-2] * ab_im + pi[-1] * ab_re)
    pw_re = jnp.stack(pr)
    pw_im = jnp.stack(pi)
    ca_re = jnp.einsum('gcp,dgp->dgcp', c_re, pw_re[:L]) - jnp.einsum('gcp,dgp->dgcp', c_im, pw_im[:L])
    ca_im = jnp.einsum('gcp,dgp->dgcp', c_re, pw_im[:L]) + jnp.einsum('gcp,dgp->dgcp', c_im, pw_re[:L])
    kern = jnp.einsum('dgop,gpi->dgoi', ca_re, bb_re) - jnp.einsum('dgop,gpi->dgoi', ca_im, bb_im)
    lag = np.arange(L)[None, :] - np.arange(L)[:, None]
    lag_or_zero_block = np.where(lag >= 0, lag, L)
    e_re, e_im = pw_re[:L][::-1], pw_im[:L][::-1]
    ws_re = jnp.einsum('sgp,gpi->sgip', e_re, bb_re) - jnp.einsum('sgp,gpi->sgip', e_im, bb_im)
    ws_im = jnp.einsum('sgp,gpi->sgip', e_re, bb_im) + jnp.einsum('sgp,gpi->sgip', e_im, bb_re)
    o_re, o_im = pw_re[1:L + 1], pw_im[1:L + 1]
    wo_re = jnp.einsum('gop,tgp->gpto', c_re, o_re) - jnp.einsum('gop,tgp->gpto', c_im, o_im)
    wo_im = -(jnp.einsum('gop,tgp->gpto', c_re, o_im) + jnp.einsum('gop,tgp->gpto', c_im, o_re))

    mi_h, ws_h, wo_h, al_h = [], [], [], []
    for hf in range(SSM_HALVES):
        gs = slice(hf * gh, (hf + 1) * gh)
        eye_h = jnp.eye(gh, dtype=F32)
        blocks = jnp.einsum('dgoi,gh->dgiho', kern[:, gs], eye_h).reshape(L, gh * C, gh * C)
        blocks = blocks.at[0].add(jnp.diag(d_skip[hf * gh * C:(hf + 1) * gh * C]))
        blocks = jnp.concatenate([blocks, jnp.zeros((1, gh * C, gh * C), F32)], axis=0)
        mi_h.append(blocks[lag_or_zero_block].transpose(0, 2, 1, 3).reshape(L * gh * C, L * gh * C))
        wsr = jnp.einsum('sgip,gh->sgihp', ws_re[:, gs], eye_h).reshape(L * gh * C, gh * P)
        wsi = jnp.einsum('sgip,gh->sgihp', ws_im[:, gs], eye_h).reshape(L * gh * C, gh * P)
        ws_h.append(jnp.concatenate([wsr, wsi], axis=1))
        wor = jnp.einsum('gpto,gh->gptho', wo_re[gs], eye_h).reshape(gh * P, L * gh * C)
        woi = jnp.einsum('gpto,gh->gptho', wo_im[gs], eye_h).reshape(gh * P, L * gh * C)
        wo_h.append(jnp.concatenate([wor, woi], axis=0))
        al_h.append(jnp.stack([pw_re[L, gs].reshape(gh * P), pw_im[L, gs].reshape(gh * P)]))
    return (jnp.stack(mi_h).astype(BF16), jnp.stack(ws_h).astype(BF16), jnp.stack(wo_h).astype(BF16),
            jnp.stack(al_h))


def _tail_kernel(x_ref, oa_ref, y_ref, szs_ref, gain_ref, scale_ref, shift_ref, gate_ref, wm_ref, wglu_ref,
                 wpa_ref, wps_ref, wout_ref, fg_ref, o_ref):
    x = x_ref[...]
    d = x.shape[1]
    hb = _modulated_norm(x, gain_ref[...], scale_ref[0], shift_ref[0]).astype(BF16)
    mg = _sigmoid(_dot(hb, wm_ref[...]))
    y = jnp.concatenate([y_ref[0], y_ref[1]], axis=-1)
    yy = _dot(_gelu(y).astype(BF16), wglu_ref[...])
    o_ssm = yy[:, :SSM_WIDTH] * _sigmoid(yy[:, SSM_WIDTH:]) * szs_ref[...].astype(F32)
    merged = (mg[:, :d] * _dot(oa_ref[...], wpa_ref[...])
              + mg[:, d:] * _dot(o_ssm.astype(BF16), wps_ref[...]))
    xo = x + gate_ref[0] * _dot(merged.astype(BF16), wout_ref[...])
    ms = jnp.mean(xo * xo, axis=-1, keepdims=True)
    o_ref[...] = xo * lax.rsqrt(ms + NORM_EPS) * fg_ref[...]


def _tail(x2, oa, y3, szs, gain, scale, shift, gate, wm, wglu, wpa, wps, wout, fgain, seq):
    t, d = x2.shape
    tm = PROJ_ROWS
    per_seq = seq // tm
    row = lambda n: pl.BlockSpec((tm, n), lambda i: (i, 0))
    full = lambda a: pl.BlockSpec(a.shape, lambda i: (0,) * a.ndim)
    mod = pl.BlockSpec((1, 1, d), lambda i: (i // per_seq, 0, 0))
    return pl.pallas_call(
        _tail_kernel,
        grid=(t // tm,),
        in_specs=[row(d), row(ATTN_WIDTH), pl.BlockSpec((SSM_HALVES, tm, LANES), lambda i: (0, i, 0)),
                  row(SSM_WIDTH), full(gain), mod, mod, mod,
                  full(wm), full(wglu), full(wpa), full(wps), full(wout), full(fgain)],
        out_specs=row(d),
        out_shape=jax.ShapeDtypeStruct((t, d), F32),
        compiler_params=pltpu.CompilerParams(vmem_limit_bytes=VMEM_LIMIT),
        name="tail",
    )(x2, oa, y3, szs, gain, scale, shift, gate, wm, wglu, wpa, wps, wout, fgain)


def _head_perm():
    idx = np.arange(ATTN_WIDTH)
    h, rem = idx // LANES, idx % LANES
    g, dd = rem // HEAD_DIM, rem % HEAD_DIM
    return (g * HEADS_PER_GROUP + h) * HEAD_DIM + dd


def _rope_tables(seq):
    half = HEAD_DIM // 2
    inv_freq = ROPE_THETA ** (-jnp.arange(half, dtype=F32) / half)
    ang = jnp.arange(seq, dtype=F32)[:, None] * inv_freq[None, :]
    cos, sin = jnp.cos(ang), jnp.sin(ang)
    zero = jnp.zeros_like(sin)
    reps = LANES // HEAD_DIM
    cos_t = jnp.tile(jnp.concatenate([cos, cos], axis=1), (1, reps))
    sa_t = jnp.tile(jnp.concatenate([-sin, zero], axis=1), (1, reps))
    sb_t = jnp.tile(jnp.concatenate([zero, sin], axis=1), (1, reps))
    return cos_t, sa_t, sb_t


def _proj_weight(w_in):
    d = w_in.shape[0]
    o_q, o_kv, o_g, o_z, o_u, o_zs, o_m = 0, 512, 1280, 1304, 1816, 2072, 2328
    perm = _head_perm()
    kv = lambda j: w_in[:, o_kv + j * KV_WIDTH: o_kv + (j + 1) * KV_WIDTH]
    gates = jnp.pad(w_in[:, o_g:o_z], ((0, 0), (0, LANES - 3 * N_HEADS)))
    cols = [w_in[:, o_q:o_kv][:, perm] * (HEAD_DIM ** -0.5),
            kv(0), kv(2), kv(4), kv(1), kv(3), kv(5), gates,
            w_in[:, o_z:o_u][:, perm], w_in[:, o_u:o_zs], w_in[:, o_zs:o_m]]
    w = jnp.concatenate(cols, axis=1)
    assert w.shape == (d, _C_END)
    return w.astype(BF16), w_in[:, o_m:].astype(BF16)


def _attn_constants(seq):
    n_cmp = LANES
    cs = np.arange(n_cmp) * CMP_STRIDE
    ss = np.arange(LANES) * SEL_BLOCK
    ovl = (np.minimum(cs[:, None] + CMP_BLOCK, ss[None, :] + SEL_BLOCK) > np.maximum(cs[:, None], ss[None, :]))
    ovl = ovl & (np.arange(LANES)[None, :] < seq // SEL_BLOCK) & (cs[:, None] + CMP_BLOCK <= seq)
    tq, tk = ATTN_Q, ATTN_K
    v = np.arange(tk // tq)[:, None, None]
    r = np.arange(tq)[None, :, None]
    cc = np.arange(tk)[None, None, :]
    dbias = np.where(cc <= v * tq + r, 0.0, MASK_VALUE).astype(np.float32)
    eg = np.zeros((LANES, 3 * ATTN_WIDTH), np.float32)
    for g in range(N_GROUPS):
        for h in range(HEADS_PER_GROUP):
            for j in range(3):
                base = j * ATTN_WIDTH + h * LANES + g * HEAD_DIM
                eg[(g * HEADS_PER_GROUP + h) * 3 + j, base:base + HEAD_DIM] = 1.0
    span = WINDOW + tq
    q0 = (np.arange(WINDOW // tq + 1) * tq)[:, None, None]
    kp = np.maximum(q0 - WINDOW, 0) + np.arange(span)[None, None, :]
    tt = q0 + np.arange(tq)[None, :, None]
    wbias = np.where((kp <= tt) & (kp > tt - WINDOW), 0.0, MASK_VALUE).astype(np.float32)
    return (jnp.asarray(ovl.T, BF16), jnp.asarray(eg, BF16), jnp.asarray(wbias, F32), jnp.asarray(dbias, F32))


def kernel(x, c, w_ada, b_ada, norm_gain, w_in, pe_cmp_k, w_cmp_k1, b_cmp_k1, w_cmp_k2, pe_cmp_v, w_cmp_v1,
           b_cmp_v1, w_cmp_v2, lam_re, lam_im, log_dt, b_re, b_im, c_re, c_im, d_skip, w_glu, w_proj_attn,
           w_proj_ssm, w_out, final_gain):
    bsz, seq, d = x.shape
    depth = w_in.shape[0]
    assert depth == 1, "the tail kernel fuses the final norm into the (single) layer"
    assert bsz % SSM_BT == 0 and seq % SSM_CHUNK == 0
    assert seq % (ATTN_K) == 0 and seq % PROJ_ROWS == 0 and seq >= WINDOW + ATTN_Q
    assert (seq - CMP_BLOCK) // CMP_STRIDE + 1 <= LANES and seq // CMP_STRIDE == LANES
    t = bsz * seq
    cos_t, sa_t, sb_t = _rope_tables(seq)
    ovlt, eg, wbias, dbias = _attn_constants(seq)
    perm = _head_perm()
    n_chunks = seq // SSM_CHUNK

    x2 = x.reshape(t, d)
    for l in range(depth):
        mod = _ada(c, w_ada[l], b_ada[l])
        shift, mscale, gate = [m.reshape(bsz, 1, d) for m in jnp.split(mod, 3, axis=-1)]
        gain = norm_gain[l].reshape(1, d)
        w_proj, w_merge = _proj_weight(w_in[l])
        (q, kc_r, ks0, ks1, kw, vc_r, vs0, vs1, vw0, vw1, g3, sz, u, szs) = _proj(
            x2, gain, mscale, shift, cos_t, sa_t, sb_t, w_proj, seq)

        nch = seq // CMP_STRIDE
        kc, vc = _compress(
            kc_r.reshape(bsz, nch, CMP_STRIDE * LANES), vc_r.reshape(bsz, nch, CMP_STRIDE * LANES),
            _compress_params(pe_cmp_k[l], w_cmp_k1[l], b_cmp_k1[l], w_cmp_k2[l]),
            _compress_params(pe_cmp_v[l], w_cmp_v1[l], b_cmp_v1[l], w_cmp_v2[l]))
        maskq, ocmp, gates = _select(q, kc, vc, g3, ovlt, eg, bsz, seq)
        o_attn = _attn(q, maskq, ocmp, ks0, ks1, vs0, vs1, kw, vw0, vw1, gates, sz, wbias, dbias, bsz, seq)

        mi, ws, wo, al = _ssm_params(lam_re[l], lam_im[l], log_dt[l], b_re[l], b_im[l], c_re[l], c_im[l],
                                     d_skip[l])
        y3 = _ssm(u, mi, ws, wo, al, n_chunks)

        x2 = _tail(x2, o_attn, y3, szs, gain, mscale, shift, gate, w_merge, w_glu[l].astype(BF16),
                   w_proj_attn[l][perm].astype(BF16), w_proj_ssm[l].astype(BF16), w_out[l].astype(BF16),
                   final_gain.reshape(1, d) if l == depth - 1 else jnp.ones((1, d), F32), seq)
    return x2.reshape(bsz, seq, d)
```

```python
import functools
import math

import jax
import jax.numpy as jnp
import numpy as np
from jax import lax
from jax.experimental import pallas as pl
from jax.experimental.pallas import tpu as pltpu

F32 = jnp.float32
BF16 = jnp.bfloat16

N_HEADS = 8
N_GROUPS = 2
HEADS_PER_GROUP = N_HEADS // N_GROUPS
HEAD_DIM = 64
ATTN_WIDTH = N_HEADS * HEAD_DIM
KV_WIDTH = N_GROUPS * HEAD_DIM
CMP_BLOCK = 32
CMP_STRIDE = 16
CMP_HIDDEN = 2 * HEAD_DIM
SEL_BLOCK = 64
N_SEL = 16
WINDOW = 512
ROPE_THETA = 10000.0
FORCE_SCORE = 1.0e4
MASK_VALUE = -1.0e30
SSM_WIDTH = 256
SSM_GROUP = 16
SSM_GROUPS = SSM_WIDTH // SSM_GROUP
SSM_STATE = 64
NORM_EPS = 1e-6

LANES = 128
SUBLANES = 8
VMEM_LIMIT = 56 * 1024 * 1024

PROJ_ROWS = 1024
ATTN_Q = 256
SELECT_Q = 512
SELECT_SUB = 128
SSM_CHUNK = 8
SSM_HALVES = 2
SSM_BT = 8
SSM_UNROLL = 4


def _dot(a, b):
    return jnp.dot(a, b, preferred_element_type=F32)


def _dot_nt(a, b):
    return lax.dot_general(a, b, (((1,), (1,)), ((), ())), preferred_element_type=F32)


def _dot_hilo(a, b_bf16):
    hi = a.astype(BF16)
    lo = (a - hi.astype(F32)).astype(BF16)
    return _dot(hi, b_bf16) + _dot(lo, b_bf16)


def _sigmoid(x):
    return 1.0 / (1.0 + jnp.exp(-x))


def _silu(x):
    return x * _sigmoid(x)


def _gelu(x):
    return 0.5 * x * (1.0 + jnp.tanh(math.sqrt(2.0 / math.pi) * (x + 0.044715 * (x * x * x))))


def _ada_kernel(c_ref, w_ref, b_ref, o_ref):
    c = c_ref[...]
    o_ref[...] = _dot(_silu(c).astype(BF16), w_ref[...].astype(BF16)) + b_ref[...]


def _ada(c, w, b):
    bsz, d = c.shape
    n = w.shape[1]
    return pl.pallas_call(
        _ada_kernel,
        grid=(n // d,),
        in_specs=[pl.BlockSpec((bsz, d), lambda j: (0, 0)),
                  pl.BlockSpec((d, d), lambda j: (0, j)),
                  pl.BlockSpec((1, d), lambda j: (0, j))],
        out_specs=pl.BlockSpec((bsz, d), lambda j: (0, j)),
        out_shape=jax.ShapeDtypeStruct((bsz, n), F32),
        compiler_params=pltpu.CompilerParams(vmem_limit_bytes=VMEM_LIMIT),
        name="ada",
    )(c, w, b.reshape(1, n))


_C_Q = 0
_C_KC = 512
_C_KS = 640
_C_KW = 768
_C_VC = 896
_C_VS = 1024
_C_VW = 1152
_C_G3 = 1280
_C_Z = 1408
_C_U = 1920
_C_ZS = 2176
_C_END = 2432


def _modulated_norm(x, gain, scale, shift):
    ms = jnp.mean(x * x, axis=-1, keepdims=True)
    return x * lax.rsqrt(ms + NORM_EPS) * gain * (1.0 + scale) + shift


def _proj_kernel(x_ref, gain_ref, scale_ref, shift_ref, cos_ref, sa_ref, sb_ref, w_ref,
                 q_ref, kc_ref, ks0_ref, ks1_ref, kw_ref, vc_ref, vs0_ref, vs1_ref, vw0_ref, vw1_ref,
                 g3_ref, sz_ref, u_ref, szs_ref, stage_ref, *, per_seq):
    x = x_ref[...]
    hb = _modulated_norm(x, gain_ref[...], scale_ref[0], shift_ref[0]).astype(BF16)

    def proj(a, b):
        return _dot(hb, w_ref[:, a:b])

    cos = cos_ref[...]
    sa = sa_ref[...]
    sb = sb_ref[...]

    def rope(t):
        return t * cos + pltpu.roll(t, LANES - 32, 1) * sa + pltpu.roll(t, 32, 1) * sb

    narrow = proj(_C_Q, _C_Z)
    part = lambda a: narrow[:, a:a + LANES]
    for v in range(ATTN_WIDTH // LANES):
        q_ref[:, v * LANES:(v + 1) * LANES] = rope(part(_C_Q + v * LANES)).astype(BF16)
    kw_ref[...] = rope(part(_C_KW)).astype(BF16)
    tm = x.shape[0]
    for out_ref, val in ((kc_ref, rope(part(_C_KC))), (vc_ref, part(_C_VC))):
        stage_ref[...] = val
        for j in range(CMP_STRIDE):
            out_ref[:, j * LANES:(j + 1) * LANES] = stage_ref[pl.ds(j, tm // CMP_STRIDE, stride=CMP_STRIDE), :]
    lane = lax.broadcasted_iota(jnp.int32, (tm, LANES), 1)
    pos = (pl.program_id(0) % per_seq) * tm + lax.broadcasted_iota(jnp.int32, (tm, LANES), 0)
    blk = pos >> int(math.log2(SEL_BLOCK))
    first = lane < HEAD_DIM
    ks = rope(part(_C_KS))
    ks0_ref[...] = jnp.where(first, ks, jnp.where(lane - HEAD_DIM == blk, 1.0, 0.0)).astype(BF16)
    ks1_ref[...] = jnp.where(first, jnp.where(lane == blk, 1.0, 0.0), ks).astype(BF16)
    ones = jnp.ones((tm, LANES), BF16)
    zero = jnp.zeros((tm, LANES), F32)
    for v_val, refs in ((part(_C_VS), (vs0_ref, vs1_ref)), (part(_C_VW), (vw0_ref, vw1_ref))):
        refs[0][:, :LANES] = jnp.where(first, v_val, zero).astype(BF16)
        refs[1][:, :LANES] = jnp.where(first, zero, v_val).astype(BF16)
        refs[0][:, LANES:] = ones
        refs[1][:, LANES:] = ones
    g3_ref[...] = _sigmoid(part(_C_G3))
    sz_ref[...] = _silu(proj(_C_Z, _C_U)).astype(BF16)
    u = proj(_C_U, _C_ZS)
    for hf in range(SSM_HALVES):
        stage_ref[...] = u[:, hf * LANES:(hf + 1) * LANES]
        for j in range(SSM_CHUNK):
            u_ref[hf, :, j * LANES:(j + 1) * LANES] = stage_ref[
                pl.ds(j, tm // SSM_CHUNK, stride=SSM_CHUNK), :].astype(BF16)
    szs_ref[...] = _silu(proj(_C_ZS, _C_END)).astype(BF16)


def _proj(x2, gain, scale, shift, cos_t, sa_t, sb_t, w, seq):
    t, d = x2.shape
    tm = PROJ_ROWS
    per_seq = seq // tm
    row = lambda n: pl.BlockSpec((tm, n), lambda i: (i, 0))
    mod = pl.BlockSpec((1, 1, d), lambda i: (i // per_seq, 0, 0))
    tab = pl.BlockSpec((tm, LANES), lambda i: (i % per_seq, 0))
    cmp_w, ssm_w = CMP_STRIDE * LANES, SSM_CHUNK * LANES
    out_shape = [
        jax.ShapeDtypeStruct((t, ATTN_WIDTH), BF16),
        jax.ShapeDtypeStruct((t // CMP_STRIDE, cmp_w), F32),
        jax.ShapeDtypeStruct((t, LANES), BF16),
        jax.ShapeDtypeStruct((t, LANES), BF16),
        jax.ShapeDtypeStruct((t, LANES), BF16),
        jax.ShapeDtypeStruct((t // CMP_STRIDE, cmp_w), F32),
        jax.ShapeDtypeStruct((t, 2 * LANES), BF16),
        jax.ShapeDtypeStruct((t, 2 * LANES), BF16),
        jax.ShapeDtypeStruct((t, 2 * LANES), BF16),
        jax.ShapeDtypeStruct((t, 2 * LANES), BF16),
        jax.ShapeDtypeStruct((t, LANES), F32),
        jax.ShapeDtypeStruct((t, ATTN_WIDTH), BF16),
        jax.ShapeDtypeStruct((SSM_HALVES, t // SSM_CHUNK, ssm_w), BF16),
        jax.ShapeDtypeStruct((t, SSM_WIDTH), BF16),
    ]
    cmp_rows = pl.BlockSpec((tm // CMP_STRIDE, cmp_w), lambda i: (i, 0))
    out_specs = [row(ATTN_WIDTH), cmp_rows, row(LANES), row(LANES), row(LANES), cmp_rows] + [row(2 * LANES)] * 4 + [row(LANES)] + [
                 row(ATTN_WIDTH),
                 pl.BlockSpec((SSM_HALVES, tm // SSM_CHUNK, ssm_w), lambda i: (0, i, 0)),
                 row(SSM_WIDTH)]
    return pl.pallas_call(
        functools.partial(_proj_kernel, per_seq=per_seq),
        grid=(t // tm,),
        in_specs=[row(d), pl.BlockSpec((1, d), lambda i: (0, 0)), mod, mod, tab, tab, tab,
                  pl.BlockSpec((d, _C_END), lambda i: (0, 0))],
        out_specs=out_specs,
        out_shape=out_shape,
        scratch_shapes=[pltpu.VMEM((tm, LANES), F32)],
        compiler_params=pltpu.CompilerParams(vmem_limit_bytes=VMEM_LIMIT),
        name="proj",
    )(x2, gain, scale, shift, cos_t, sa_t, sb_t, w)


def _compress_kernel(rk_ref, rv_ref, pea_k, peb_k, wa_k, wb_k, b1_k, w2_k,
                     pea_v, peb_v, wa_v, wb_v, b1_v, w2_v, kc_ref, vc_ref):
    def one(r_ref, pea, peb, wa, wb, b1, w2):
        r = r_ref[0]
        first = _dot((r + pea[...]).astype(BF16), wa[...])
        second = _dot((r + peb[...]).astype(BF16), wb[...])
        nxt = pltpu.roll(second, second.shape[0] - 1, 0)
        hid = _gelu(first + nxt + b1[...])
        return _dot(hid.astype(BF16), w2[...])

    kc_ref[0] = one(rk_ref, pea_k, peb_k, wa_k, wb_k, b1_k, w2_k)
    vc_ref[0] = one(rv_ref, pea_v, peb_v, wa_v, wb_v, b1_v, w2_v)


def _compress(rk, rv, pk, pv):
    bsz, nch, width = rk.shape
    full = lambda a: pl.BlockSpec(a.shape, lambda b: (0,) * a.ndim)
    rspec = pl.BlockSpec((1, nch, width), lambda b: (b, 0, 0))
    ospec = pl.BlockSpec((1, nch, LANES), lambda b: (b, 0, 0))
    return pl.pallas_call(
        _compress_kernel,
        grid=(bsz,),
        in_specs=[rspec, rspec] + [full(a) for a in pk] + [full(a) for a in pv],
        out_specs=[ospec, ospec],
        out_shape=[jax.ShapeDtypeStruct((bsz, nch, LANES), F32)] * 2,
        compiler_params=pltpu.CompilerParams(vmem_limit_bytes=VMEM_LIMIT),
        name="compress",
    )(rk, rv, *pk, *pv)


def _compress_params(pe, w1, b1, w2):
    half = CMP_BLOCK // 2
    eye = jnp.eye(N_GROUPS, dtype=F32)
    w1r = w1.reshape(CMP_BLOCK, HEAD_DIM, CMP_HIDDEN)

    def expand(wl):
        return jnp.einsum('ldj,gh->lgdhj', wl, eye).reshape(half * KV_WIDTH, N_GROUPS * CMP_HIDDEN)

    def pe_lanes(p):
        return jnp.broadcast_to(p[:, None, :], (half, N_GROUPS, HEAD_DIM)).reshape(1, half * KV_WIDTH)

    w2e = jnp.einsum('jd,gh->gjhd', w2, eye).reshape(N_GROUPS * CMP_HIDDEN, KV_WIDTH)
    return (pe_lanes(pe[:half]), pe_lanes(pe[half:]),
            expand(w1r[:half]).astype(BF16), expand(w1r[half:]).astype(BF16),
            jnp.tile(b1, N_GROUPS).reshape(1, N_GROUPS * CMP_HIDDEN), w2e.astype(BF16))


def _select_kernel(q_ref, kc_ref, vc_ref, g3_ref, ovlt_ref, egc_ref, mask_ref, ocmp_ref, gate_ref, *, seq):
    ts = SELECT_SUB
    hg = HEADS_PER_GROUP
    rows = hg * ts
    n_sel_blocks = seq // SEL_BLOCK
    sel_shift = int(math.log2(SEL_BLOCK))
    base = pl.program_id(1) * SELECT_Q
    lane = lax.broadcasted_iota(jnp.int32, (ts, LANES), 1)
    row = lax.broadcasted_iota(jnp.int32, (ts, LANES), 0)
    jidx = lax.broadcasted_iota(jnp.int32, (n_sel_blocks, ts), 0)
    qcol = lax.broadcasted_iota(jnp.int32, (n_sel_blocks, ts), 1)
    jloc = lax.broadcasted_iota(jnp.int32, (SUBLANES, ts), 0)
    groups = range(N_GROUPS)
    in_group = [(lane >= HEAD_DIM * g) & (lane < HEAD_DIM * (g + 1)) for g in groups]
    first_half = lane < HEAD_DIM
    zero = jnp.zeros((ts, LANES), BF16)
    kcb = kc_ref[0].astype(BF16)
    vcb = vc_ref[0].astype(BF16)
    subs = range(SELECT_Q // ts)
    units = [(s, g) for s in subs for g in groups]
    sc, p_cmp, o_cmp, imp, maskq = {}, {}, {}, {}, {}

    def scores(u):
        s, g = u
        qh = [q_ref[s * ts:(s + 1) * ts, h * LANES:(h + 1) * LANES] for h in range(hg)]
        qg = jnp.concatenate([jnp.where(in_group[g], x, zero) for x in qh], axis=0)
        sc[u] = _dot_nt(qg, kcb).reshape(hg, ts, LANES)

    def softmax(u):
        s, g = u
        cvalid = (lane * CMP_STRIDE + (CMP_BLOCK - 1)) <= base + s * ts + row
        sv = jnp.where(cvalid[None], sc[u], MASK_VALUE)
        m = jnp.max(sv, axis=-1, keepdims=True)
        e = jnp.where(cvalid[None], jnp.exp(sv - m), 0.0)
        den = jnp.sum(e, axis=-1, keepdims=True)
        p_cmp[u] = e * (1.0 / jnp.where(den > 0.0, den, 1.0))

    def outputs(u):
        o_cmp[u] = _dot(p_cmp[u].reshape(rows, LANES).astype(BF16), vcb)
        psum = p_cmp[u][0]
        for h in range(1, hg):
            psum = psum + p_cmp[u][h]
        p_hi = psum.astype(BF16)
        p_lo = (psum - p_hi.astype(F32)).astype(BF16)
        imp[u] = (_dot_nt(ovlt_ref[...], p_hi) + _dot_nt(ovlt_ref[...], p_lo))[0:n_sel_blocks]

    def rank(u):
        s, g = u
        t_lane = base + s * ts + qcol
        forced = (jidx == 0) | (jidx == (t_lane >> sel_shift))
        future = jidx * SEL_BLOCK > t_lane
        imp_g = jnp.where(forced, FORCE_SCORE, jnp.where(future, -1.0, imp[u]))
        tiles = [imp_g[k * SUBLANES:(k + 1) * SUBLANES] for k in range(n_sel_blocks // SUBLANES)]
        cnts = [jnp.zeros((SUBLANES, ts), jnp.int32) for _ in tiles]
        for i in range(n_sel_blocks):
            other = jnp.broadcast_to(imp_g[i:i + 1, :], (SUBLANES, ts))
            ki, ri = divmod(i, SUBLANES)
            for k, tile_k in enumerate(tiles):
                if k < ki:
                    ahead = other > tile_k
                elif k > ki:
                    ahead = other >= tile_k
                else:
                    ahead = (other > tile_k) | ((other == tile_k) & (jloc > ri))
                cnts[k] = cnts[k] + jnp.where(ahead, 1, 0)
        cnt = jnp.concatenate(cnts, axis=0)
        mask_t = jnp.where(cnt < min(N_SEL, n_sel_blocks), 0.0, MASK_VALUE)
        lo_rows = HEAD_DIM * (1 - g)
        parts = [mask_t, jnp.zeros((LANES - n_sel_blocks - lo_rows, ts), F32)]
        if lo_rows:
            parts = [jnp.zeros((lo_rows, ts), F32)] + parts
        maskq[u] = jnp.concatenate(parts, axis=0).T

    def store(s):
        rs = slice(s * ts, (s + 1) * ts)
        mask_ref[rs, :] = (maskq[s, 0] + maskq[s, 1]).astype(BF16)
        gexp = _dot_hilo(g3_ref[rs, :], egc_ref[...])
        gate_ref[rs, :] = gexp[:, ATTN_WIDTH:]
        for h in range(hg):
            hr = slice(h * ts, (h + 1) * ts)
            both = jnp.where(first_half, o_cmp[s, 0][hr], o_cmp[s, 1][hr])
            ocmp_ref[rs, h * LANES:(h + 1) * LANES] = gexp[:, h * LANES:(h + 1) * LANES] * both

    stages = (scores, softmax, outputs, rank)
    for step in range(len(units) + len(stages) - 1):
        for k, stage in enumerate(stages):
            if 0 <= step - k < len(units):
                s, g = units[step - k]
                stage((s, g))
                if stage is rank and g == N_GROUPS - 1:
                    store(s)


def _select(q, kc, vc, g3, ovlt, egc, bsz, seq):
    tsel = SELECT_Q
    nq = seq // tsel
    nch = kc.shape[1]
    qrow = lambda n: pl.BlockSpec((tsel, n), lambda b, i: (b * nq + i, 0))
    cmp_spec = pl.BlockSpec((1, nch, LANES), lambda b, i: (b, 0, 0))
    full = lambda a: pl.BlockSpec(a.shape, lambda b, i: (0,) * a.ndim)
    return pl.pallas_call(
        functools.partial(_select_kernel, seq=seq),
        grid=(bsz, nq),
        in_specs=[qrow(ATTN_WIDTH), cmp_spec, cmp_spec, qrow(LANES), full(ovlt), full(egc)],
        out_specs=[qrow(LANES), qrow(ATTN_WIDTH), qrow(2 * ATTN_WIDTH)],
        out_shape=[jax.ShapeDtypeStruct((bsz * seq, LANES), BF16),
                   jax.ShapeDtypeStruct((bsz * seq, ATTN_WIDTH), F32),
                   jax.ShapeDtypeStruct((bsz * seq, 2 * ATTN_WIDTH), F32)],
        compiler_params=pltpu.CompilerParams(vmem_limit_bytes=VMEM_LIMIT),
        name="select",
    )(q, kc, vc, g3, ovlt, egc)


def _attn_kernel(q_ref, mask_ref, ocmp_ref, ks0_ref, ks1_ref, vs0_ref, vs1_ref, kw_ref, vw0_ref, vw1_ref,
                 gate_ref, sz_ref, wbias_ref, dbias_ref, o_ref, slc_ref, win_ref, *, seq):
    tq = ATTN_Q
    hg = HEADS_PER_GROUP
    rows = hg * tq
    q0 = pl.program_id(1) * tq
    lane = lax.broadcasted_iota(jnp.int32, (tq, LANES), 1)

    span = WINDOW + tq
    start = pl.multiple_of(jnp.maximum(q0 - WINDOW, 0), tq)
    wbias = wbias_ref[0]
    width = ATTN_WIDTH
    ks_refs, vs_refs, vw_refs = (ks0_ref, ks1_ref), (vs0_ref, vs1_ref), (vw0_ref, vw1_ref)

    groups = range(N_GROUPS)
    in_group = [(lane >= HEAD_DIM * g) & (lane < HEAD_DIM * (g + 1)) for g in groups]
    zero = jnp.zeros((tq, LANES), BF16)
    q_heads = [q_ref[:, h * LANES:(h + 1) * LANES] for h in range(hg)]
    qg = [jnp.concatenate([jnp.where(in_group[g], qh, zero) for qh in q_heads], axis=0) for g in groups]
    maskq = mask_ref[...]
    qsel = [jnp.concatenate([jnp.where(in_group[g], qh, maskq) for qh in q_heads], axis=0) for g in groups]

    n_tiles = q0 // tq + 1
    for n in range(1, seq // tq + 1):
        @pl.when(n_tiles == n)
        def _(n=n):
            klen = n * tq
            s2 = [_dot_nt(qsel[g], ks_refs[g][0:klen, :]) for g in groups]
            s3 = [_dot_nt(qg[g], kw_ref[pl.ds(start, span), :]).reshape(hg, tq, span) for g in groups]
            for g in groups:
                last = (s2[g][:, klen - tq:].reshape(hg, tq, tq) + dbias_ref[...][None]).reshape(rows, tq)
                sg = last if n == 1 else jnp.concatenate([s2[g][:, :klen - tq], last], axis=1)
                m2 = jnp.max(sg, axis=-1, keepdims=True)
                p2 = jnp.exp(sg - m2).astype(BF16)
                slc_ref[g] = _dot(p2, vs_refs[g][0:klen, :])
            for g in groups:
                sw = s3[g] + wbias[None]
                m3 = jnp.max(sw, axis=-1, keepdims=True)
                p3 = jnp.exp(sw - m3).astype(BF16).reshape(rows, span)
                win_ref[g] = _dot(p3, vw_refs[g][pl.ds(start, span), :])

    o_slc, o_win = [], []
    for g in groups:
        o_slc.append(slc_ref[g, :, :LANES] * (1.0 / slc_ref[g, :, LANES:]))
        o_win.append(win_ref[g, :, :LANES] * (1.0 / win_ref[g, :, LANES:]))

    first_half = lane < HEAD_DIM
    for h in range(hg):
        rws = slice(h * tq, (h + 1) * tq)
        cols = slice(h * LANES, (h + 1) * LANES)
        acc = ocmp_ref[:, cols]
        for j, branch in enumerate((o_slc, o_win)):
            both = jnp.where(first_half, branch[0][rws], branch[1][rws])
            acc = acc + gate_ref[:, j * width + h * LANES: j * width + (h + 1) * LANES] * both
        o_ref[:, cols] = (acc * sz_ref[:, cols].astype(F32)).astype(BF16)


def _attn(q, maskq, ocmp, ks0, ks1, vs0, vs1, kw, vw0, vw1, gates, sz, wbias, dbias, bsz, seq):
    tq = ATTN_Q
    nq = seq // tq
    qrow = lambda n: pl.BlockSpec((tq, n), lambda b, i: (b * nq + i, 0))
    per_seq = pl.BlockSpec((seq, LANES), lambda b, i: (b, 0))
    val_seq = pl.BlockSpec((seq, 2 * LANES), lambda b, i: (b, 0))
    diag = pl.BlockSpec((tq, tq), lambda b, i: (0, 0))
    wspec = pl.BlockSpec((1,) + wbias.shape[1:], lambda b, i: (jnp.minimum(i, wbias.shape[0] - 1), 0, 0))
    return pl.pallas_call(
        functools.partial(_attn_kernel, seq=seq),
        grid=(bsz, nq),
        in_specs=[qrow(ATTN_WIDTH), qrow(LANES), qrow(ATTN_WIDTH), per_seq, per_seq, val_seq, val_seq, per_seq,
                  val_seq, val_seq]
                 + [qrow(2 * ATTN_WIDTH), qrow(ATTN_WIDTH), wspec, diag],
        out_specs=qrow(ATTN_WIDTH),
        out_shape=jax.ShapeDtypeStruct((bsz * seq, ATTN_WIDTH), BF16),
        scratch_shapes=[pltpu.VMEM((N_GROUPS, HEADS_PER_GROUP * tq, 2 * LANES), F32)] * 2,
        compiler_params=pltpu.CompilerParams(vmem_limit_bytes=VMEM_LIMIT),
        name="attn",
    )(q, maskq, ocmp, ks0, ks1, vs0, vs1, kw, vw0, vw1, gates, sz, wbias, dbias)


def _ssm_kernel(u_ref, mi_ref, ws_ref, wo_ref, al_ref, y_ref, sx_ref, *, n_chunks):
    u = u_ref[0]
    n_tiles = sx_ref.shape[0]
    nt = n_tiles // 2
    tile = lambda j: slice(j * LANES, (j + 1) * LANES)
    for j in range(n_tiles):
        sx_ref[j] = _dot(u, ws_ref[0, :, tile(j)])
    a_re = [al_ref[0, 0:1, tile(j)] for j in range(nt)]
    a_im = [al_ref[0, 1:2, tile(j)] for j in range(nt)]
    bt = SSM_BT

    def body(k, carry):
        rows = pl.ds(k, bt, stride=n_chunks)
        new = []
        for j in range(nt):
            x_re, x_im = carry[j], carry[nt + j]
            s_re = sx_ref[j, rows, :]
            s_im = sx_ref[nt + j, rows, :]
            sx_ref[j, rows, :] = x_re
            sx_ref[nt + j, rows, :] = x_im
            new.append((a_re[j] * x_re - a_im[j] * x_im + s_re, a_re[j] * x_im + a_im[j] * x_re + s_im))
        return tuple([n[0] for n in new] + [n[1] for n in new])

    zero = jnp.zeros((bt, LANES), F32)
    lax.fori_loop(0, n_chunks, body, (zero,) * n_tiles, unroll=SSM_UNROLL)
    width = u.shape[1]
    ctile = 2 * LANES
    y_cols = [_dot(u[:, :(c + 1) * ctile], mi_ref[0, :(c + 1) * ctile, c * ctile:(c + 1) * ctile])
              for c in range(width // ctile)]
    xs = jnp.concatenate([sx_ref[j].astype(BF16) for j in range(n_tiles)], axis=1)
    y = jnp.concatenate(y_cols, axis=1) + _dot(xs, wo_ref[0])
    for j in range(SSM_CHUNK):
        y_ref[0, pl.ds(j, y.shape[0], stride=SSM_CHUNK), :] = y[:, tile(j)]


def _ssm(u3, mi, ws, wo, al, n_chunks):
    halves, rows, width = u3.shape
    r = SSM_BT * n_chunks
    wspec = lambda a: pl.BlockSpec((1,) + a.shape[1:], lambda hf, i: (hf, 0, 0))
    return pl.pallas_call(
        functools.partial(_ssm_kernel, n_chunks=n_chunks),
        grid=(halves, rows // r),
        in_specs=[pl.BlockSpec((1, r, width), lambda hf, i: (hf, i, 0)),
                  wspec(mi), wspec(ws), wspec(wo), wspec(al)],
        out_specs=pl.BlockSpec((1, r * SSM_CHUNK, LANES), lambda hf, i: (hf, i, 0)),
        out_shape=jax.ShapeDtypeStruct((halves, rows * SSM_CHUNK, LANES), F32),
        scratch_shapes=[pltpu.VMEM((ws.shape[2] // LANES, r, LANES), F32)],
        compiler_params=pltpu.CompilerParams(vmem_limit_bytes=VMEM_LIMIT),
        name="ssm",
    )(u3, mi, ws, wo, al)


def _ssm_params(lam_re, lam_im, log_dt, b_re, b_im, c_re, c_im, d_skip):
    L = SSM_CHUNK
    G, P, C = SSM_GROUPS, SSM_STATE, SSM_GROUP
    gh = G // SSM_HALVES
    dt = jnp.exp(log_dt)[:, None]
    lr, li = lam_re, lam_im
    mag = jnp.exp(lr * dt)
    ab_re = mag * jnp.cos(li * dt)
    ab_im = mag * jnp.sin(li * dt)
    nr, ni = ab_re - 1.0, ab_im
    den = lr * lr + li * li
    cr = ((nr * lr + ni * li) / den)[..., None]
    ci = ((ni * lr - nr * li) / den)[..., None]
    bb_re = cr * b_re - ci * b_im
    bb_im = cr * b_im + ci * b_re
    pr, pi = [jnp.ones_like(ab_re)], [jnp.zeros_like(ab_im)]
    for _ in range(L):
        pr.append(pr[-1] * ab_re - pi[-1] * ab_im)
        pi.append(pr[-2] * ab_im + pi[-1] * ab_re)
    pw_re = jnp.stack(pr)
    pw_im = jnp.stack(pi)
    ca_re = jnp.einsum('gcp,dgp->dgcp', c_re, pw_re[:L]) - jnp.einsum('gcp,dgp->dgcp', c_im, pw_im[:L])
    ca_im = jnp.einsum('gcp,dgp->dgcp', c_re, pw_im[:L]) + jnp.einsum('gcp,dgp->dgcp', c_im, pw_re[:L])
    kern = jnp.einsum('dgop,gpi->dgoi', ca_re, bb_re) - jnp.einsum('dgop,gpi->dgoi', ca_im, bb_im)
    lag = np.arange(L)[None, :] - np.arange(L)[:, None]
    lag_or_zero_block = np.where(lag >= 0, lag, L)
    e_re, e_im = pw_re[:L][::-1], pw_im[:L][::-1]
    ws_re = jnp.einsum('sgp,gpi->sgip', e_re, bb_re) - jnp.einsum('sgp,gpi->sgip', e_im, bb_im)
    ws_im = jnp.einsum('sgp,gpi->sgip', e_re, bb_im) + jnp.einsum('sgp,gpi->sgip', e_im, bb_re)
    o_re, o_im = pw_re[1:L + 1], pw_im[1:L + 1]
    wo_re = jnp.einsum('gop,tgp->gpto', c_re, o_re) - jnp.einsum('gop,tgp->gpto', c_im, o_im)
    wo_im = -(jnp.einsum('gop,tgp->gpto', c_re, o_im) + jnp.einsum('gop,tgp->gpto', c_im, o_re))

    mi_h, ws_h, wo_h, al_h = [], [], [], []
    for hf in range(SSM_HALVES):
        gs = slice(hf * gh, (hf + 1) * gh)
        eye_h = jnp.eye(gh, dtype=F32)
        blocks = jnp.einsum('dgoi,gh->dgiho', kern[:, gs], eye_h).reshape(L, gh * C, gh * C)
        blocks = blocks.at[0].add(jnp.diag(d_skip[hf * gh * C:(hf + 1) * gh * C]))
        blocks = jnp.concatenate([blocks, jnp.zeros((1, gh * C, gh * C), F32)], axis=0)
        mi_h.append(blocks[lag_or_zero_block].transpose(0, 2, 1, 3).reshape(L * gh * C, L * gh * C))
        wsr = jnp.einsum('sgip,gh->sgihp', ws_re[:, gs], eye_h).reshape(L * gh * C, gh * P)
        wsi = jnp.einsum('sgip,gh->sgihp', ws_im[:, gs], eye_h).reshape(L * gh * C, gh * P)
        ws_h.append(jnp.concatenate([wsr, wsi], axis=1))
        wor = jnp.einsum('gpto,gh->gptho', wo_re[gs], eye_h).reshape(gh * P, L * gh * C)
        woi = jnp.einsum('gpto,gh->gptho', wo_im[gs], eye_h).reshape(gh * P, L * gh * C)
        wo_h.append(jnp.concatenate([wor, woi], axis=0))
        al_h.append(jnp.stack([pw_re[L, gs].reshape(gh * P), pw_im[L, gs].reshape(gh * P)]))
    return (jnp.stack(mi_h).astype(BF16), jnp.stack(ws_h).astype(BF16), jnp.stack(wo_h).astype(BF16),
            jnp.stack(al_h))


def _tail_kernel(x_ref, oa_ref, y_ref, szs_ref, gain_ref, scale_ref, shift_ref, gate_ref, wm_ref, wglu_ref,
                 wpa_ref, wps_ref, wout_ref, fg_ref, o_ref):
    x = x_ref[...]
    d = x.shape[1]
    hb = _modulated_norm(x, gain_ref[...], scale_ref[0], shift_ref[0]).astype(BF16)
    mg = _sigmoid(_dot(hb, wm_ref[...]))
    y = jnp.concatenate([y_ref[0], y_ref[1]], axis=-1)
    yy = _dot(_gelu(y).astype(BF16), wglu_ref[...])
    o_ssm = yy[:, :SSM_WIDTH] * _sigmoid(yy[:, SSM_WIDTH:]) * szs_ref[...].astype(F32)
    merged = (mg[:, :d] * _dot(oa_ref[...], wpa_ref[...])
              + mg[:, d:] * _dot(o_ssm.astype(BF16), wps_ref[...]))
    xo = x + gate_ref[0] * _dot(merged.astype(BF16), wout_ref[...])
    ms = jnp.mean(xo * xo, axis=-1, keepdims=True)
    o_ref[...] = xo * lax.rsqrt(ms + NORM_EPS) * fg_ref[...]


def _tail(x2, oa, y3, szs, gain, scale, shift, gate, wm, wglu, wpa, wps, wout, fgain, seq):
    t, d = x2.shape
    tm = PROJ_ROWS
    per_seq = seq // tm
    row = lambda n: pl.BlockSpec((tm, n), lambda i: (i, 0))
    full = lambda a: pl.BlockSpec(a.shape, lambda i: (0,) * a.ndim)
    mod = pl.BlockSpec((1, 1, d), lambda i: (i // per_seq, 0, 0))
    return pl.pallas_call(
        _tail_kernel,
        grid=(t // tm,),
        in_specs=[row(d), row(ATTN_WIDTH), pl.BlockSpec((SSM_HALVES, tm, LANES), lambda i: (0, i, 0)),
                  row(SSM_WIDTH), full(gain), mod, mod, mod,
                  full(wm), full(wglu), full(wpa), full(wps), full(wout), full(fgain)],
        out_specs=row(d),
        out_shape=jax.ShapeDtypeStruct((t, d), F32),
        compiler_params=pltpu.CompilerParams(vmem_limit_bytes=VMEM_LIMIT),
        name="tail",
    )(x2, oa, y3, szs, gain, scale, shift, gate, wm, wglu, wpa, wps, wout, fgain)


def _head_perm():
    idx = np.arange(ATTN_WIDTH)
    h, rem = idx // LANES, idx % LANES
    g, dd = rem // HEAD_DIM, rem % HEAD_DIM
    return (g * HEADS_PER_GROUP + h) * HEAD_DIM + dd


def _rope_tables(seq):
    half = HEAD_DIM // 2
    inv_freq = ROPE_THETA ** (-jnp.arange(half, dtype=F32) / half)
    ang = jnp.arange(seq, dtype=F32)[:, None] * inv_freq[None, :]
    cos, sin = jnp.cos(ang), jnp.sin(ang)
    zero = jnp.zeros_like(sin)
    reps = LANES // HEAD_DIM
    cos_t = jnp.tile(jnp.concatenate([cos, cos], axis=1), (1, reps))
    sa_t = jnp.tile(jnp.concatenate([-sin, zero], axis=1), (1, reps))
    sb_t = jnp.tile(jnp.concatenate([zero, sin], axis=1), (1, reps))
    return cos_t, sa_t, sb_t


def _proj_weight(w_in):
    d = w_in.shape[0]
    o_q, o_kv, o_g, o_z, o_u, o_zs, o_m = 0, 512, 1280, 1304, 1816, 2072, 2328
    perm = _head_perm()
    kv = lambda j: w_in[:, o_kv + j * KV_WIDTH: o_kv + (j + 1) * KV_WIDTH]
    gates = jnp.pad(w_in[:, o_g:o_z], ((0, 0), (0, LANES - 3 * N_HEADS)))
    cols = [w_in[:, o_q:o_kv][:, perm] * (HEAD_DIM ** -0.5),
            kv(0), kv(2), kv(4), kv(1), kv(3), kv(5), gates,
            w_in[:, o_z:o_u][:, perm], w_in[:, o_u:o_zs], w_in[:, o_zs:o_m]]
    w = jnp.concatenate(cols, axis=1)
    assert w.shape == (d, _C_END)
    return w.astype(BF16), w_in[:, o_m:].astype(BF16)


def _attn_constants(seq):
    n_cmp = LANES
    cs = np.arange(n_cmp) * CMP_STRIDE
    ss = np.arange(LANES) * SEL_BLOCK
    ovl = (np.minimum(cs[:, None] + CMP_BLOCK, ss[None, :] + SEL_BLOCK) > np.maximum(cs[:, None], ss[None, :]))
    ovl = ovl & (np.arange(LANES)[None, :] < seq // SEL_BLOCK) & (cs[:, None] + CMP_BLOCK <= seq)
    tq = ATTN_Q
    dbias = np.where(np.arange(tq)[None, :] <= np.arange(tq)[:, None], 0.0, MASK_VALUE).astype(np.float32)
    eg = np.zeros((LANES, 3 * ATTN_WIDTH), np.float32)
    for g in range(N_GROUPS):
        for h in range(HEADS_PER_GROUP):
            for j in range(3):
                base = j * ATTN_WIDTH + h * LANES + g * HEAD_DIM
                eg[(g * HEADS_PER_GROUP + h) * 3 + j, base:base + HEAD_DIM] = 1.0
    span = WINDOW + tq
    q0 = (np.arange(WINDOW // tq + 1) * tq)[:, None, None]
    kp = np.maximum(q0 - WINDOW, 0) + np.arange(span)[None, None, :]
    tt = q0 + np.arange(tq)[None, :, None]
    wbias = np.where((kp <= tt) & (kp > tt - WINDOW), 0.0, MASK_VALUE).astype(np.float32)
    return (jnp.asarray(ovl.T, BF16), jnp.asarray(eg, BF16), jnp.asarray(wbias, F32), jnp.asarray(dbias, F32))


def kernel(x, c, w_ada, b_ada, norm_gain, w_in, pe_cmp_k, w_cmp_k1, b_cmp_k1, w_cmp_k2, pe_cmp_v, w_cmp_v1,
           b_cmp_v1, w_cmp_v2, lam_re, lam_im, log_dt, b_re, b_im, c_re, c_im, d_skip, w_glu, w_proj_attn,
           w_proj_ssm, w_out, final_gain):
    bsz, seq, d = x.shape
    depth = w_in.shape[0]
    assert depth == 1, "the tail kernel fuses the final norm into the (single) layer"
    assert bsz % SSM_BT == 0 and seq % SSM_CHUNK == 0
    assert seq % SELECT_Q == 0 and seq % PROJ_ROWS == 0 and seq >= WINDOW + ATTN_Q and WINDOW % ATTN_Q == 0
    assert (seq - CMP_BLOCK) // CMP_STRIDE + 1 <= LANES and seq // CMP_STRIDE == LANES
    t = bsz * seq
    cos_t, sa_t, sb_t = _rope_tables(seq)
    ovlt, eg, wbias, dbias = _attn_constants(seq)
    perm = _head_perm()
    n_chunks = seq // SSM_CHUNK

    x2 = x.reshape(t, d)
    for l in range(depth):
        mod = _ada(c, w_ada[l], b_ada[l])
        shift, mscale, gate = [m.reshape(bsz, 1, d) for m in jnp.split(mod, 3, axis=-1)]
        gain = norm_gain[l].reshape(1, d)
        w_proj, w_merge = _proj_weight(w_in[l])
        (q, kc_r, ks0, ks1, kw, vc_r, vs0, vs1, vw0, vw1, g3, sz, u, szs) = _proj(
            x2, gain, mscale, shift, cos_t, sa_t, sb_t, w_proj, seq)

        nch = seq // CMP_STRIDE
        kc, vc = _compress(
            kc_r.reshape(bsz, nch, CMP_STRIDE * LANES), vc_r.reshape(bsz, nch, CMP_STRIDE * LANES),
            _compress_params(pe_cmp_k[l], w_cmp_k1[l], b_cmp_k1[l], w_cmp_k2[l]),
            _compress_params(pe_cmp_v[l], w_cmp_v1[l], b_cmp_v1[l], w_cmp_v2[l]))
        maskq, ocmp, gates = _select(q, kc, vc, g3, ovlt, eg, bsz, seq)
        o_attn = _attn(q, maskq, ocmp, ks0, ks1, vs0, vs1, kw, vw0, vw1, gates, sz, wbias, dbias, bsz, seq)

        mi, ws, wo, al = _ssm_params(lam_re[l], lam_im[l], log_dt[l], b_re[l], b_im[l], c_re[l], c_im[l],
                                     d_skip[l])
        y3 = _ssm(u, mi, ws, wo, al, n_chunks)

        x2 = _tail(x2, o_attn, y3, szs, gain, mscale, shift, gate, w_merge, w_glu[l].astype(BF16),
                   w_proj_attn[l][perm].astype(BF16), w_proj_ssm[l].astype(BF16), w_out[l].astype(BF16),
                   final_gain.reshape(1, d) if l == depth - 1 else jnp.ones((1, d), F32), seq)
    return x2.reshape(bsz, seq, d)
```

```python
import functools
import math

import jax
import jax.numpy as jnp
import numpy as np
from jax import lax
from jax.experimental import pallas as pl
from jax.experimental.pallas import tpu as pltpu

F32 = jnp.float32
BF16 = jnp.bfloat16

N_HEADS = 8
N_GROUPS = 2
HEADS_PER_GROUP = N_HEADS // N_GROUPS
HEAD_DIM = 64
ATTN_WIDTH = N_HEADS * HEAD_DIM
KV_WIDTH = N_GROUPS * HEAD_DIM
CMP_BLOCK = 32
CMP_STRIDE = 16
CMP_HIDDEN = 2 * HEAD_DIM
SEL_BLOCK = 64
N_SEL = 16
WINDOW = 512
ROPE_THETA = 10000.0
FORCE_SCORE = 1.0e4
MASK_VALUE = -1.0e30
SSM_WIDTH = 256
SSM_GROUP = 16
SSM_GROUPS = SSM_WIDTH // SSM_GROUP
SSM_STATE = 64
NORM_EPS = 1e-6

LANES = 128
SUBLANES = 8
VMEM_LIMIT = 56 * 1024 * 1024

PROJ_ROWS = 1024
ATTN_Q = 256
ATTN_K = 512
SELECT_Q = 1024
SELECT_SUB = 128
SSM_CHUNK = 8
SSM_HALVES = 2
SSM_BT = 8
SSM_UNROLL = 4


def _dot(a, b):
    return jnp.dot(a, b, preferred_element_type=F32)


def _dot_nt(a, b):
    return lax.dot_general(a, b, (((1,), (1,)), ((), ())), preferred_element_type=F32)


def _dot_hilo(a, b_bf16):
    hi = a.astype(BF16)
    lo = (a - hi.astype(F32)).astype(BF16)
    return _dot(hi, b_bf16) + _dot(lo, b_bf16)


def _sigmoid(x):
    return 1.0 / (1.0 + jnp.exp(-x))


def _silu(x):
    return x * _sigmoid(x)


def _gelu(x):
    return 0.5 * x * (1.0 + jnp.tanh(math.sqrt(2.0 / math.pi) * (x + 0.044715 * (x * x * x))))


def _ada_kernel(c_ref, w_ref, b_ref, o_ref):
    c = c_ref[...]
    o_ref[...] = _dot(_silu(c).astype(BF16), w_ref[...].astype(BF16)) + b_ref[...]


def _ada(c, w, b):
    bsz, d = c.shape
    n = w.shape[1]
    return pl.pallas_call(
        _ada_kernel,
        grid=(n // d,),
        in_specs=[pl.BlockSpec((bsz, d), lambda j: (0, 0)),
                  pl.BlockSpec((d, d), lambda j: (0, j)),
                  pl.BlockSpec((1, d), lambda j: (0, j))],
        out_specs=pl.BlockSpec((bsz, d), lambda j: (0, j)),
        out_shape=jax.ShapeDtypeStruct((bsz, n), F32),
        compiler_params=pltpu.CompilerParams(vmem_limit_bytes=VMEM_LIMIT),
        name="ada",
    )(c, w, b.reshape(1, n))


_C_Q = 0
_C_KC = 512
_C_KS = 640
_C_KW = 768
_C_VC = 896
_C_VS = 1024
_C_VW = 1152
_C_G3 = 1280
_C_Z = 1408
_C_U = 1920
_C_ZS = 2176
_C_END = 2432


def _modulated_norm(x, gain, scale, shift):
    ms = jnp.mean(x * x, axis=-1, keepdims=True)
    return x * lax.rsqrt(ms + NORM_EPS) * gain * (1.0 + scale) + shift


def _proj_kernel(x_ref, gain_ref, scale_ref, shift_ref, cos_ref, sa_ref, sb_ref, w_ref,
                 q_ref, kc_ref, ks0_ref, ks1_ref, kw_ref, vc_ref, vs0_ref, vs1_ref, vw0_ref, vw1_ref,
                 g3_ref, sz_ref, u_ref, szs_ref, stage_ref, *, per_seq):
    x = x_ref[...]
    hb = _modulated_norm(x, gain_ref[...], scale_ref[0], shift_ref[0]).astype(BF16)

    def proj(a, b):
        return _dot(hb, w_ref[:, a:b])

    cos = cos_ref[...]
    sa = sa_ref[...]
    sb = sb_ref[...]

    def rope(t):
        return t * cos + pltpu.roll(t, LANES - 32, 1) * sa + pltpu.roll(t, 32, 1) * sb

    narrow = proj(_C_Q, _C_Z)
    part = lambda a: narrow[:, a:a + LANES]
    for v in range(ATTN_WIDTH // LANES):
        q_ref[:, v * LANES:(v + 1) * LANES] = rope(part(_C_Q + v * LANES)).astype(BF16)
    kw_ref[...] = rope(part(_C_KW)).astype(BF16)
    tm = x.shape[0]
    for out_ref, val in ((kc_ref, rope(part(_C_KC))), (vc_ref, part(_C_VC))):
        stage_ref[...] = val
        for j in range(CMP_STRIDE):
            out_ref[:, j * LANES:(j + 1) * LANES] = stage_ref[pl.ds(j, tm // CMP_STRIDE, stride=CMP_STRIDE), :]
    lane = lax.broadcasted_iota(jnp.int32, (tm, LANES), 1)
    pos = (pl.program_id(0) % per_seq) * tm + lax.broadcasted_iota(jnp.int32, (tm, LANES), 0)
    blk = pos >> int(math.log2(SEL_BLOCK))
    first = lane < HEAD_DIM
    ks = rope(part(_C_KS))
    ks0_ref[...] = jnp.where(first, ks, jnp.where(lane - HEAD_DIM == blk, 1.0, 0.0)).astype(BF16)
    ks1_ref[...] = jnp.where(first, jnp.where(lane == blk, 1.0, 0.0), ks).astype(BF16)
    ones = jnp.ones((tm, LANES), BF16)
    zero = jnp.zeros((tm, LANES), F32)
    for v_val, refs in ((part(_C_VS), (vs0_ref, vs1_ref)), (part(_C_VW), (vw0_ref, vw1_ref))):
        refs[0][:, :LANES] = jnp.where(first, v_val, zero).astype(BF16)
        refs[1][:, :LANES] = jnp.where(first, zero, v_val).astype(BF16)
        refs[0][:, LANES:] = ones
        refs[1][:, LANES:] = ones
    g3_ref[...] = _sigmoid(part(_C_G3))
    sz_ref[...] = _silu(proj(_C_Z, _C_U)).astype(BF16)
    u = proj(_C_U, _C_ZS)
    for hf in range(SSM_HALVES):
        stage_ref[...] = u[:, hf * LANES:(hf + 1) * LANES]
        for j in range(SSM_CHUNK):
            u_ref[hf, :, j * LANES:(j + 1) * LANES] = stage_ref[
                pl.ds(j, tm // SSM_CHUNK, stride=SSM_CHUNK), :].astype(BF16)
    szs_ref[...] = _silu(proj(_C_ZS, _C_END)).astype(BF16)


def _proj(x2, gain, scale, shift, cos_t, sa_t, sb_t, w, seq):
    t, d = x2.shape
    tm = PROJ_ROWS
    per_seq = seq // tm
    row = lambda n: pl.BlockSpec((tm, n), lambda i: (i, 0))
    mod = pl.BlockSpec((1, 1, d), lambda i: (i // per_seq, 0, 0))
    tab = pl.BlockSpec((tm, LANES), lambda i: (i % per_seq, 0))
    cmp_w, ssm_w = CMP_STRIDE * LANES, SSM_CHUNK * LANES
    out_shape = [
        jax.ShapeDtypeStruct((t, ATTN_WIDTH), BF16),
        jax.ShapeDtypeStruct((t // CMP_STRIDE, cmp_w), F32),
        jax.ShapeDtypeStruct((t, LANES), BF16),
        jax.ShapeDtypeStruct((t, LANES), BF16),
        jax.ShapeDtypeStruct((t, LANES), BF16),
        jax.ShapeDtypeStruct((t // CMP_STRIDE, cmp_w), F32),
        jax.ShapeDtypeStruct((t, 2 * LANES), BF16),
        jax.ShapeDtypeStruct((t, 2 * LANES), BF16),
        jax.ShapeDtypeStruct((t, 2 * LANES), BF16),
        jax.ShapeDtypeStruct((t, 2 * LANES), BF16),
        jax.ShapeDtypeStruct((t, LANES), F32),
        jax.ShapeDtypeStruct((t, ATTN_WIDTH), BF16),
        jax.ShapeDtypeStruct((SSM_HALVES, t // SSM_CHUNK, ssm_w), BF16),
        jax.ShapeDtypeStruct((t, SSM_WIDTH), BF16),
    ]
    cmp_rows = pl.BlockSpec((tm // CMP_STRIDE, cmp_w), lambda i: (i, 0))
    out_specs = [row(ATTN_WIDTH), cmp_rows, row(LANES), row(LANES), row(LANES), cmp_rows] + [row(2 * LANES)] * 4 + [row(LANES)] + [
                 row(ATTN_WIDTH),
                 pl.BlockSpec((SSM_HALVES, tm // SSM_CHUNK, ssm_w), lambda i: (0, i, 0)),
                 row(SSM_WIDTH)]
    return pl.pallas_call(
        functools.partial(_proj_kernel, per_seq=per_seq),
        grid=(t // tm,),
        in_specs=[row(d), pl.BlockSpec((1, d), lambda i: (0, 0)), mod, mod, tab, tab, tab,
                  pl.BlockSpec((d, _C_END), lambda i: (0, 0))],
        out_specs=out_specs,
        out_shape=out_shape,
        scratch_shapes=[pltpu.VMEM((tm, LANES), F32)],
        compiler_params=pltpu.CompilerParams(vmem_limit_bytes=VMEM_LIMIT),
        name="proj",
    )(x2, gain, scale, shift, cos_t, sa_t, sb_t, w)


def _compress_kernel(rk_ref, rv_ref, pea_k, peb_k, wa_k, wb_k, b1_k, w2_k,
                     pea_v, peb_v, wa_v, wb_v, b1_v, w2_v, kc_ref, vc_ref):
    def one(r_ref, pea, peb, wa, wb, b1, w2):
        r = r_ref[0]
        first = _dot((r + pea[...]).astype(BF16), wa[...])
        second = _dot((r + peb[...]).astype(BF16), wb[...])
        nxt = pltpu.roll(second, second.shape[0] - 1, 0)
        hid = _gelu(first + nxt + b1[...])
        return _dot(hid.astype(BF16), w2[...])

    kc_ref[0] = one(rk_ref, pea_k, peb_k, wa_k, wb_k, b1_k, w2_k)
    vc_ref[0] = one(rv_ref, pea_v, peb_v, wa_v, wb_v, b1_v, w2_v)


def _compress(rk, rv, pk, pv):
    bsz, nch, width = rk.shape
    full = lambda a: pl.BlockSpec(a.shape, lambda b: (0,) * a.ndim)
    rspec = pl.BlockSpec((1, nch, width), lambda b: (b, 0, 0))
    ospec = pl.BlockSpec((1, nch, LANES), lambda b: (b, 0, 0))
    return pl.pallas_call(
        _compress_kernel,
        grid=(bsz,),
        in_specs=[rspec, rspec] + [full(a) for a in pk] + [full(a) for a in pv],
        out_specs=[ospec, ospec],
        out_shape=[jax.ShapeDtypeStruct((bsz, nch, LANES), F32)] * 2,
        compiler_params=pltpu.CompilerParams(vmem_limit_bytes=VMEM_LIMIT),
        name="compress",
    )(rk, rv, *pk, *pv)


def _compress_params(pe, w1, b1, w2):
    half = CMP_BLOCK // 2
    eye = jnp.eye(N_GROUPS, dtype=F32)
    w1r = w1.reshape(CMP_BLOCK, HEAD_DIM, CMP_HIDDEN)

    def expand(wl):
        return jnp.einsum('ldj,gh->lgdhj', wl, eye).reshape(half * KV_WIDTH, N_GROUPS * CMP_HIDDEN)

    def pe_lanes(p):
        return jnp.broadcast_to(p[:, None, :], (half, N_GROUPS, HEAD_DIM)).reshape(1, half * KV_WIDTH)

    w2e = jnp.einsum('jd,gh->gjhd', w2, eye).reshape(N_GROUPS * CMP_HIDDEN, KV_WIDTH)
    return (pe_lanes(pe[:half]), pe_lanes(pe[half:]),
            expand(w1r[:half]).astype(BF16), expand(w1r[half:]).astype(BF16),
            jnp.tile(b1, N_GROUPS).reshape(1, N_GROUPS * CMP_HIDDEN), w2e.astype(BF16))


def _select_kernel(q_ref, kc_ref, vc_ref, g3_ref, ovlt_ref, egc_ref, mask_ref, ocmp_ref, gate_ref, *, seq):
    ts = SELECT_SUB
    hg = HEADS_PER_GROUP
    rows = hg * ts
    n_sel_blocks = seq // SEL_BLOCK
    sel_shift = int(math.log2(SEL_BLOCK))
    base = pl.program_id(1) * SELECT_Q
    lane = lax.broadcasted_iota(jnp.int32, (ts, LANES), 1)
    row = lax.broadcasted_iota(jnp.int32, (ts, LANES), 0)
    jidx = lax.broadcasted_iota(jnp.int32, (n_sel_blocks, ts), 0)
    qcol = lax.broadcasted_iota(jnp.int32, (n_sel_blocks, ts), 1)
    jloc = lax.broadcasted_iota(jnp.int32, (SUBLANES, ts), 0)
    groups = range(N_GROUPS)
    in_group = [(lane >= HEAD_DIM * g) & (lane < HEAD_DIM * (g + 1)) for g in groups]
    first_half = lane < HEAD_DIM
    zero = jnp.zeros((ts, LANES), BF16)
    kcb = kc_ref[0].astype(BF16)
    vcb = vc_ref[0].astype(BF16)
    subs = range(SELECT_Q // ts)
    units = [(s, g) for s in subs for g in groups]
    sc, p_cmp, o_cmp, imp, maskq = {}, {}, {}, {}, {}

    def scores(u):
        s, g = u
        qh = [q_ref[s * ts:(s + 1) * ts, h * LANES:(h + 1) * LANES] for h in range(hg)]
        qg = jnp.concatenate([jnp.where(in_group[g], x, zero) for x in qh], axis=0)
        sc[u] = _dot_nt(qg, kcb).reshape(hg, ts, LANES)

    def softmax(u):
        s, g = u
        cvalid = (lane * CMP_STRIDE + (CMP_BLOCK - 1)) <= base + s * ts + row
        sv = jnp.where(cvalid[None], sc[u], MASK_VALUE)
        m = jnp.max(sv, axis=-1, keepdims=True)
        e = jnp.where(cvalid[None], jnp.exp(sv - m), 0.0)
        den = jnp.sum(e, axis=-1, keepdims=True)
        p_cmp[u] = e * (1.0 / jnp.where(den > 0.0, den, 1.0))

    def outputs(u):
        o_cmp[u] = _dot(p_cmp[u].reshape(rows, LANES).astype(BF16), vcb)
        psum = p_cmp[u][0]
        for h in range(1, hg):
            psum = psum + p_cmp[u][h]
        p_hi = psum.astype(BF16)
        p_lo = (psum - p_hi.astype(F32)).astype(BF16)
        imp[u] = (_dot_nt(ovlt_ref[...], p_hi) + _dot_nt(ovlt_ref[...], p_lo))[0:n_sel_blocks]

    def rank(u):
        s, g = u
        t_lane = base + s * ts + qcol
        forced = (jidx == 0) | (jidx == (t_lane >> sel_shift))
        future = jidx * SEL_BLOCK > t_lane
        imp_g = jnp.where(forced, FORCE_SCORE, jnp.where(future, -1.0, imp[u]))
        tiles = [imp_g[k * SUBLANES:(k + 1) * SUBLANES] for k in range(n_sel_blocks // SUBLANES)]
        cnts = [jnp.zeros((SUBLANES, ts), jnp.int32) for _ in tiles]
        for i in range(n_sel_blocks):
            other = jnp.broadcast_to(imp_g[i:i + 1, :], (SUBLANES, ts))
            ki, ri = divmod(i, SUBLANES)
            for k, tile_k in enumerate(tiles):
                if k < ki:
                    ahead = other > tile_k
                elif k > ki:
                    ahead = other >= tile_k
                else:
                    ahead = (other > tile_k) | ((other == tile_k) & (jloc > ri))
                cnts[k] = cnts[k] + jnp.where(ahead, 1, 0)
        cnt = jnp.concatenate(cnts, axis=0)
        mask_t = jnp.where(cnt < min(N_SEL, n_sel_blocks), 0.0, MASK_VALUE)
        lo_rows = HEAD_DIM * (1 - g)
        parts = [mask_t, jnp.zeros((LANES - n_sel_blocks - lo_rows, ts), F32)]
        if lo_rows:
            parts = [jnp.zeros((lo_rows, ts), F32)] + parts
        maskq[u] = jnp.concatenate(parts, axis=0).T

    def store(s):
        rs = slice(s * ts, (s + 1) * ts)
        mask_ref[rs, :] = (maskq[s, 0] + maskq[s, 1]).astype(BF16)
        gexp = _dot_hilo(g3_ref[rs, :], egc_ref[...])
        gate_ref[rs, :] = gexp[:, ATTN_WIDTH:]
        for h in range(hg):
            hr = slice(h * ts, (h + 1) * ts)
            both = jnp.where(first_half, o_cmp[s, 0][hr], o_cmp[s, 1][hr])
            ocmp_ref[rs, h * LANES:(h + 1) * LANES] = gexp[:, h * LANES:(h + 1) * LANES] * both

    stages = (scores, softmax, outputs, rank)
    for step in range(len(units) + len(stages) - 1):
        for k, stage in enumerate(stages):
            if 0 <= step - k < len(units):
                s, g = units[step - k]
                stage((s, g))
                if stage is rank and g == N_GROUPS - 1:
                    store(s)


def _select(q, kc, vc, g3, ovlt, egc, bsz, seq):
    tsel = SELECT_Q
    nq = seq // tsel
    nch = kc.shape[1]
    qrow = lambda n: pl.BlockSpec((tsel, n), lambda b, i: (b * nq + i, 0))
    cmp_spec = pl.BlockSpec((1, nch, LANES), lambda b, i: (b, 0, 0))
    full = lambda a: pl.BlockSpec(a.shape, lambda b, i: (0,) * a.ndim)
    return pl.pallas_call(
        functools.partial(_select_kernel, seq=seq),
        grid=(bsz, nq),
        in_specs=[qrow(ATTN_WIDTH), cmp_spec, cmp_spec, qrow(LANES), full(ovlt), full(egc)],
        out_specs=[qrow(LANES), qrow(ATTN_WIDTH), qrow(2 * ATTN_WIDTH)],
        out_shape=[jax.ShapeDtypeStruct((bsz * seq, LANES), BF16),
                   jax.ShapeDtypeStruct((bsz * seq, ATTN_WIDTH), F32),
                   jax.ShapeDtypeStruct((bsz * seq, 2 * ATTN_WIDTH), F32)],
        compiler_params=pltpu.CompilerParams(vmem_limit_bytes=VMEM_LIMIT),
        name="select",
    )(q, kc, vc, g3, ovlt, egc)


def _attn_kernel(q_ref, mask_ref, ocmp_ref, ks0_ref, ks1_ref, vs0_ref, vs1_ref, kw_ref, vw0_ref, vw1_ref,
                 gate_ref, sz_ref, wbias_ref, dbias_ref, o_ref, slc_ref, win_ref, *, seq):
    tq, tk = ATTN_Q, ATTN_K
    hg = HEADS_PER_GROUP
    rows = hg * tq
    q0 = pl.program_id(1) * tq
    lane = lax.broadcasted_iota(jnp.int32, (tq, LANES), 1)

    span = WINDOW + tq
    start = pl.multiple_of(jnp.maximum(q0 - WINDOW, 0), tq)
    wbias = wbias_ref[0]
    width = ATTN_WIDTH
    ks_refs, vs_refs, vw_refs = (ks0_ref, ks1_ref), (vs0_ref, vs1_ref), (vw0_ref, vw1_ref)

    groups = range(N_GROUPS)
    in_group = [(lane >= HEAD_DIM * g) & (lane < HEAD_DIM * (g + 1)) for g in groups]
    zero = jnp.zeros((tq, LANES), BF16)
    q_heads = [q_ref[:, h * LANES:(h + 1) * LANES] for h in range(hg)]
    qg = [jnp.concatenate([jnp.where(in_group[g], qh, zero) for qh in q_heads], axis=0) for g in groups]
    maskq = mask_ref[...]
    qsel = [jnp.concatenate([jnp.where(in_group[g], qh, maskq) for qh in q_heads], axis=0) for g in groups]

    n_tiles = q0 // tk + 1
    for n in range(1, seq // tk + 1):
        @pl.when(n_tiles == n)
        def _(n=n):
            klen = n * tk
            s2 = [_dot_nt(qsel[g], ks_refs[g][0:klen, :]) for g in groups]
            s3 = [_dot_nt(qg[g], kw_ref[pl.ds(start, span), :]).reshape(hg, tq, span) for g in groups]
            for g in groups:
                last = (s2[g][:, klen - tk:].reshape(hg, tq, tk) + dbias_ref[0][None]).reshape(rows, tk)
                sg = last if n == 1 else jnp.concatenate([s2[g][:, :klen - tk], last], axis=1)
                m2 = jnp.max(sg, axis=-1, keepdims=True)
                p2 = jnp.exp(sg - m2).astype(BF16)
                slc_ref[g] = _dot(p2, vs_refs[g][0:klen, :])
            for g in groups:
                sw = s3[g] + wbias[None]
                m3 = jnp.max(sw, axis=-1, keepdims=True)
                p3 = jnp.exp(sw - m3).astype(BF16).reshape(rows, span)
                win_ref[g] = _dot(p3, vw_refs[g][pl.ds(start, span), :])

    o_slc, o_win = [], []
    for g in groups:
        o_slc.append(slc_ref[g, :, :LANES] * (1.0 / slc_ref[g, :, LANES:]))
        o_win.append(win_ref[g, :, :LANES] * (1.0 / win_ref[g, :, LANES:]))

    first_half = lane < HEAD_DIM
    for h in range(hg):
        rws = slice(h * tq, (h + 1) * tq)
        cols = slice(h * LANES, (h + 1) * LANES)
        acc = ocmp_ref[:, cols]
        for j, branch in enumerate((o_slc, o_win)):
            both = jnp.where(first_half, branch[0][rws], branch[1][rws])
            acc = acc + gate_ref[:, j * width + h * LANES: j * width + (h + 1) * LANES] * both
        o_ref[:, cols] = (acc * sz_ref[:, cols].astype(F32)).astype(BF16)


def _attn(q, maskq, ocmp, ks0, ks1, vs0, vs1, kw, vw0, vw1, gates, sz, wbias, dbias, bsz, seq):
    tq, tk = ATTN_Q, ATTN_K
    nq = seq // tq
    qrow = lambda n: pl.BlockSpec((tq, n), lambda b, i: (b * nq + i, 0))
    per_seq = pl.BlockSpec((seq, LANES), lambda b, i: (b, 0))
    val_seq = pl.BlockSpec((seq, 2 * LANES), lambda b, i: (b, 0))
    diag = pl.BlockSpec((1, tq, tk), lambda b, i: (i % (tk // tq), 0, 0))
    wspec = pl.BlockSpec((1,) + wbias.shape[1:], lambda b, i: (jnp.minimum(i, wbias.shape[0] - 1), 0, 0))
    return pl.pallas_call(
        functools.partial(_attn_kernel, seq=seq),
        grid=(bsz, nq),
        in_specs=[qrow(ATTN_WIDTH), qrow(LANES), qrow(ATTN_WIDTH), per_seq, per_seq, val_seq, val_seq, per_seq,
                  val_seq, val_seq]
                 + [qrow(2 * ATTN_WIDTH), qrow(ATTN_WIDTH), wspec, diag],
        out_specs=qrow(ATTN_WIDTH),
        out_shape=jax.ShapeDtypeStruct((bsz * seq, ATTN_WIDTH), BF16),
        scratch_shapes=[pltpu.VMEM((N_GROUPS, HEADS_PER_GROUP * tq, 2 * LANES), F32)] * 2,
        compiler_params=pltpu.CompilerParams(vmem_limit_bytes=VMEM_LIMIT),
        name="attn",
    )(q, maskq, ocmp, ks0, ks1, vs0, vs1, kw, vw0, vw1, gates, sz, wbias, dbias)


def _ssm_kernel(u_ref, mi_ref, ws_ref, wo_ref, al_ref, y_ref, sx_ref, *, n_chunks):
    u = u_ref[0]
    n_tiles = sx_ref.shape[0]
    nt = n_tiles // 2
    tile = lambda j: slice(j * LANES, (j + 1) * LANES)
    for j in range(n_tiles):
        sx_ref[j] = _dot(u, ws_ref[0, :, tile(j)])
    a_re = [al_ref[0, 0:1, tile(j)] for j in range(nt)]
    a_im = [al_ref[0, 1:2, tile(j)] for j in range(nt)]
    bt = SSM_BT

    def body(k, carry):
        rows = pl.ds(k, bt, stride=n_chunks)
        new = []
        for j in range(nt):
            x_re, x_im = carry[j], carry[nt + j]
            s_re = sx_ref[j, rows, :]
            s_im = sx_ref[nt + j, rows, :]
            sx_ref[j, rows, :] = x_re
            sx_ref[nt + j, rows, :] = x_im
            new.append((a_re[j] * x_re - a_im[j] * x_im + s_re, a_re[j] * x_im + a_im[j] * x_re + s_im))
        return tuple([n[0] for n in new] + [n[1] for n in new])

    zero = jnp.zeros((bt, LANES), F32)
    lax.fori_loop(0, n_chunks, body, (zero,) * n_tiles, unroll=SSM_UNROLL)
    width = u.shape[1]
    ctile = 2 * LANES
    y_cols = [_dot(u[:, :(c + 1) * ctile], mi_ref[0, :(c + 1) * ctile, c * ctile:(c + 1) * ctile])
              for c in range(width // ctile)]
    xs = jnp.concatenate([sx_ref[j].astype(BF16) for j in range(n_tiles)], axis=1)
    y = jnp.concatenate(y_cols, axis=1) + _dot(xs, wo_ref[0])
    for j in range(SSM_CHUNK):
        y_ref[0, pl.ds(j, y.shape[0], stride=SSM_CHUNK), :] = y[:, tile(j)]


def _ssm(u3, mi, ws, wo, al, n_chunks):
    halves, rows, width = u3.shape
    r = SSM_BT * n_chunks
    wspec = lambda a: pl.BlockSpec((1,) + a.shape[1:], lambda hf, i: (hf, 0, 0))
    return pl.pallas_call(
        functools.partial(_ssm_kernel, n_chunks=n_chunks),
        grid=(halves, rows // r),
        in_specs=[pl.BlockSpec((1, r, width), lambda hf, i: (hf, i, 0)),
                  wspec(mi), wspec(ws), wspec(wo), wspec(al)],
        out_specs=pl.BlockSpec((1, r * SSM_CHUNK, LANES), lambda hf, i: (hf, i, 0)),
        out_shape=jax.ShapeDtypeStruct((halves, rows * SSM_CHUNK, LANES), F32),
        scratch_shapes=[pltpu.VMEM((ws.shape[2] // LANES, r, LANES), F32)],
        compiler_params=pltpu.CompilerParams(vmem_limit_bytes=VMEM_LIMIT),
        name="ssm",
    )(u3, mi, ws, wo, al)


def _ssm_params(lam_re, lam_im, log_dt, b_re, b_im, c_re, c_im, d_skip):
    L = SSM_CHUNK
    G, P, C = SSM_GROUPS, SSM_STATE, SSM_GROUP
    gh = G // SSM_HALVES
    dt = jnp.exp(log_dt)[:, None]
    lr, li = lam_re, lam_im
    mag = jnp.exp(lr * dt)
    ab_re = mag * jnp.cos(li * dt)
    ab_im = mag * jnp.sin(li * dt)
    nr, ni = ab_re - 1.0, ab_im
    den = lr * lr + li * li
    cr = ((nr * lr + ni * li) / den)[..., None]
    ci = ((ni * lr - nr * li) / den)[..., None]
    bb_re = cr * b_re - ci * b_im
    bb_im = cr * b_im + ci * b_re
    pr, pi = [jnp.ones_like(ab_re)], [jnp.zeros_like(ab_im)]
    for _ in range(L):
        pr.append(pr[-1] * ab_re - pi[-1] * ab_im)
        pi.append(pr[-2] * ab_im + pi[-1] * ab_re)
    pw_re = jnp.stack(pr)
    pw_im = jnp.stack(pi)
    ca_re = jnp.einsum('gcp,dgp->dgcp', c_re, pw_re[:L]) - jnp.einsum('gcp,dgp->dgcp', c_im, pw_im[:L])
    ca_im = jnp.einsum('gcp,dgp->dgcp', c_re, pw_im[:L]) + jnp.einsum('gcp,dgp->dgcp', c_im, pw_re[:L])
    kern = jnp.einsum('dgop,gpi->dgoi', ca_re, bb_re) - jnp.einsum('dgop,gpi->dgoi', ca_im, bb_im)
    lag = np.arange(L)[None, :] - np.arange(L)[:, None]
    lag_or_zero_block = np.where(lag >= 0, lag, L)
    e_re, e_im = pw_re[:L][::-1], pw_im[:L][::-1]
    ws_re = jnp.einsum('sgp,gpi->sgip', e_re, bb_re) - jnp.einsum('sgp,gpi->sgip', e_im, bb_im)
    ws_im = jnp.einsum('sgp,gpi->sgip', e_re, bb_im) + jnp.einsum('sgp,gpi->sgip', e_im, bb_re)
    o_re, o_im = pw_re[1:L + 1], pw_im[1:L + 1]
    wo_re = jnp.einsum('gop,tgp->tgop', c_re, o_re) - jnp.einsum('gop,tgp->tgop', c_im, o_im)
    wo_im = -(jnp.einsum('gop,tgp->tgop', c_re, o_im) + jnp.einsum('gop,tgp->tgop', c_im, o_re))

    def group_diag(a, rows_per_group, cols_per_group):
        tiled = jnp.tile(a, (1, gh))
        rg = (lax.broadcasted_iota(jnp.int32, tiled.shape, 0) // rows_per_group) % gh
        cg = lax.broadcasted_iota(jnp.int32, tiled.shape, 1) // cols_per_group
        return jnp.where(rg == cg, tiled, 0.0)

    mi_h, ws_h, wo_h, al_h = [], [], [], []
    for hf in range(SSM_HALVES):
        gs = slice(hf * gh, (hf + 1) * gh)
        k_gi_o = kern[:, gs].transpose(0, 1, 3, 2).reshape(L * gh * C, C)
        blocks = group_diag(k_gi_o, C, C).reshape(L, gh * C, gh * C)
        blocks = blocks.at[0].add(jnp.diag(d_skip[hf * gh * C:(hf + 1) * gh * C]))
        blocks = jnp.concatenate([blocks, jnp.zeros((1, gh * C, gh * C), F32)], axis=0)
        mi_h.append(blocks[lag_or_zero_block].transpose(0, 2, 1, 3).reshape(L * gh * C, L * gh * C))
        wsr = group_diag(ws_re[:, gs].reshape(L * gh * C, P), C, P)
        wsi = group_diag(ws_im[:, gs].reshape(L * gh * C, P), C, P)
        ws_h.append(jnp.concatenate([wsr, wsi], axis=1))
        wor = group_diag(wo_re[:, gs].reshape(L * gh * C, P), C, P).T
        woi = group_diag(wo_im[:, gs].reshape(L * gh * C, P), C, P).T
        wo_h.append(jnp.concatenate([wor, woi], axis=0))
        al_h.append(jnp.stack([pw_re[L, gs].reshape(gh * P), pw_im[L, gs].reshape(gh * P)]))
    return (jnp.stack(mi_h).astype(BF16), jnp.stack(ws_h).astype(BF16), jnp.stack(wo_h).astype(BF16),
            jnp.stack(al_h))


def _tail_kernel(x_ref, oa_ref, y_ref, szs_ref, gain_ref, scale_ref, shift_ref, gate_ref, wm_ref, wglu_ref,
                 wpa_ref, wps_ref, wout_ref, fg_ref, o_ref):
    x = x_ref[...]
    d = x.shape[1]
    hb = _modulated_norm(x, gain_ref[...], scale_ref[0], shift_ref[0]).astype(BF16)
    mg = _sigmoid(_dot(hb, wm_ref[...]))
    y = jnp.concatenate([y_ref[0], y_ref[1]], axis=-1)
    yy = _dot(_gelu(y).astype(BF16), wglu_ref[...])
    o_ssm = yy[:, :SSM_WIDTH] * _sigmoid(yy[:, SSM_WIDTH:]) * szs_ref[...].astype(F32)
    merged = (mg[:, :d] * _dot(oa_ref[...], wpa_ref[...])
              + mg[:, d:] * _dot(o_ssm.astype(BF16), wps_ref[...]))
    xo = x + gate_ref[0] * _dot(merged.astype(BF16), wout_ref[...])
    ms = jnp.mean(xo * xo, axis=-1, keepdims=True)
    o_ref[...] = xo * lax.rsqrt(ms + NORM_EPS) * fg_ref[...]


def _tail(x2, oa, y3, szs, gain, scale, shift, gate, wm, wglu, wpa, wps, wout, fgain, seq):
    t, d = x2.shape
    tm = PROJ_ROWS
    per_seq = seq // tm
    row = lambda n: pl.BlockSpec((tm, n), lambda i: (i, 0))
    full = lambda a: pl.BlockSpec(a.shape, lambda i: (0,) * a.ndim)
    mod = pl.BlockSpec((1, 1, d), lambda i: (i // per_seq, 0, 0))
    return pl.pallas_call(
        _tail_kernel,
        grid=(t // tm,),
        in_specs=[row(d), row(ATTN_WIDTH), pl.BlockSpec((SSM_HALVES, tm, LANES), lambda i: (0, i, 0)),
                  row(SSM_WIDTH), full(gain), mod, mod, mod,
                  full(wm), full(wglu), full(wpa), full(wps), full(wout), full(fgain)],
        out_specs=row(d),
        out_shape=jax.ShapeDtypeStruct((t, d), F32),
        compiler_params=pltpu.CompilerParams(vmem_limit_bytes=VMEM_LIMIT),
        name="tail",
    )(x2, oa, y3, szs, gain, scale, shift, gate, wm, wglu, wpa, wps, wout, fgain)


def _head_perm():
    idx = np.arange(ATTN_WIDTH)
    h, rem = idx // LANES, idx % LANES
    g, dd = rem // HEAD_DIM, rem % HEAD_DIM
    return (g * HEADS_PER_GROUP + h) * HEAD_DIM + dd


def _rope_tables(seq):
    half = HEAD_DIM // 2
    inv_freq = ROPE_THETA ** (-jnp.arange(half, dtype=F32) / half)
    ang = jnp.arange(seq, dtype=F32)[:, None] * inv_freq[None, :]
    cos, sin = jnp.cos(ang), jnp.sin(ang)
    zero = jnp.zeros_like(sin)
    reps = LANES // HEAD_DIM
    cos_t = jnp.tile(jnp.concatenate([cos, cos], axis=1), (1, reps))
    sa_t = jnp.tile(jnp.concatenate([-sin, zero], axis=1), (1, reps))
    sb_t = jnp.tile(jnp.concatenate([zero, sin], axis=1), (1, reps))
    return cos_t, sa_t, sb_t


def _proj_weight(w_in):
    d = w_in.shape[0]
    o_q, o_kv, o_g, o_z, o_u, o_zs, o_m = 0, 512, 1280, 1304, 1816, 2072, 2328
    perm = _head_perm()
    kv = lambda j: w_in[:, o_kv + j * KV_WIDTH: o_kv + (j + 1) * KV_WIDTH]
    gates = jnp.pad(w_in[:, o_g:o_z], ((0, 0), (0, LANES - 3 * N_HEADS)))
    cols = [w_in[:, o_q:o_kv][:, perm] * (HEAD_DIM ** -0.5),
            kv(0), kv(2), kv(4), kv(1), kv(3), kv(5), gates,
            w_in[:, o_z:o_u][:, perm], w_in[:, o_u:o_zs], w_in[:, o_zs:o_m]]
    w = jnp.concatenate(cols, axis=1)
    assert w.shape == (d, _C_END)
    return w.astype(BF16), w_in[:, o_m:].astype(BF16)


def _attn_constants(seq):
    n_cmp = LANES
    cs = np.arange(n_cmp) * CMP_STRIDE
    ss = np.arange(LANES) * SEL_BLOCK
    ovl = (np.minimum(cs[:, None] + CMP_BLOCK, ss[None, :] + SEL_BLOCK) > np.maximum(cs[:, None], ss[None, :]))
    ovl = ovl & (np.arange(LANES)[None, :] < seq // SEL_BLOCK) & (cs[:, None] + CMP_BLOCK <= seq)
    tq, tk = ATTN_Q, ATTN_K
    v = np.arange(tk // tq)[:, None, None]
    r = np.arange(tq)[None, :, None]
    cc = np.arange(tk)[None, None, :]
    dbias = np.where(cc <= v * tq + r, 0.0, MASK_VALUE).astype(np.float32)
    eg = np.zeros((LANES, 3 * ATTN_WIDTH), np.float32)
    for g in range(N_GROUPS):
        for h in range(HEADS_PER_GROUP):
            for j in range(3):
                base = j * ATTN_WIDTH + h * LANES + g * HEAD_DIM
                eg[(g * HEADS_PER_GROUP + h) * 3 + j, base:base + HEAD_DIM] = 1.0
    span = WINDOW + tq
    q0 = (np.arange(WINDOW // tq + 1) * tq)[:, None, None]
    kp = np.maximum(q0 - WINDOW, 0) + np.arange(span)[None, None, :]
    tt = q0 + np.arange(tq)[None, :, None]
    wbias = np.where((kp <= tt) & (kp > tt - WINDOW), 0.0, MASK_VALUE).astype(np.float32)
    return (jnp.asarray(ovl.T, BF16), jnp.asarray(eg, BF16), jnp.asarray(wbias, F32), jnp.asarray(dbias, F32))


def kernel(x, c, w_ada, b_ada, norm_gain, w_in, pe_cmp_k, w_cmp_k1, b_cmp_k1, w_cmp_k2, pe_cmp_v, w_cmp_v1,
           b_cmp_v1, w_cmp_v2, lam_re, lam_im, log_dt, b_re, b_im, c_re, c_im, d_skip, w_glu, w_proj_attn,
           w_proj_ssm, w_out, final_gain):
    bsz, seq, d = x.shape
    depth = w_in.shape[0]
    assert depth == 1, "the tail kernel fuses the final norm into the (single) layer"
    assert bsz % SSM_BT == 0 and seq % SSM_CHUNK == 0
    assert seq % ATTN_K == 0 and seq % SELECT_Q == 0 and seq % PROJ_ROWS == 0 and seq >= WINDOW + ATTN_Q
    assert (seq - CMP_BLOCK) // CMP_STRIDE + 1 <= LANES and seq // CMP_STRIDE == LANES
    t = bsz * seq
    cos_t, sa_t, sb_t = _rope_tables(seq)
    ovlt, eg, wbias, dbias = _attn_constants(seq)
    perm = _head_perm()
    n_chunks = seq // SSM_CHUNK

    x2 = x.reshape(t, d)
    for l in range(depth):
        mod = _ada(c, w_ada[l], b_ada[l])
        shift, mscale, gate = [m.reshape(bsz, 1, d) for m in jnp.split(mod, 3, axis=-1)]
        gain = norm_gain[l].reshape(1, d)
        w_proj, w_merge = _proj_weight(w_in[l])
        (q, kc_r, ks0, ks1, kw, vc_r, vs0, vs1, vw0, vw1, g3, sz, u, szs) = _proj(
            x2, gain, mscale, shift, cos_t, sa_t, sb_t, w_proj, seq)

        nch = seq // CMP_STRIDE
        kc, vc = _compress(
            kc_r.reshape(bsz, nch, CMP_STRIDE * LANES), vc_r.reshape(bsz, nch, CMP_STRIDE * LANES),
            _compress_params(pe_cmp_k[l], w_cmp_k1[l], b_cmp_k1[l], w_cmp_k2[l]),
            _compress_params(pe_cmp_v[l], w_cmp_v1[l], b_cmp_v1[l], w_cmp_v2[l]))
        maskq, ocmp, gates = _select(q, kc, vc, g3, ovlt, eg, bsz, seq)
        o_attn = _attn(q, maskq, ocmp, ks0, ks1, vs0, vs1, kw, vw0, vw1, gates, sz, wbias, dbias, bsz, seq)

        mi, ws, wo, al = _ssm_params(lam_re[l], lam_im[l], log_dt[l], b_re[l], b_im[l], c_re[l], c_im[l],
                                     d_skip[l])
        y3 = _ssm(u, mi, ws, wo, al, n_chunks)

        x2 = _tail(x2, o_attn, y3, szs, gain, mscale, shift, gate, w_merge, w_glu[l].astype(BF16),
                   w_proj_attn[l][perm].astype(BF16), w_proj_ssm[l].astype(BF16), w_out[l].astype(BF16),
                   final_gain.reshape(1, d) if l == depth - 1 else jnp.ones((1, d), F32), seq)
    return x2.reshape(bsz, seq, d)
```

```python
import functools
import math

import jax
import jax.numpy as jnp
import numpy as np
from jax import lax
from jax.experimental import pallas as pl
from jax.experimental.pallas import tpu as pltpu

F32 = jnp.float32
BF16 = jnp.bfloat16

N_HEADS = 8
N_GROUPS = 2
HEADS_PER_GROUP = N_HEADS // N_GROUPS
HEAD_DIM = 64
ATTN_WIDTH = N_HEADS * HEAD_DIM
KV_WIDTH = N_GROUPS * HEAD_DIM
CMP_BLOCK = 32
CMP_STRIDE = 16
CMP_HIDDEN = 2 * HEAD_DIM
SEL_BLOCK = 64
N_SEL = 16
WINDOW = 512
ROPE_THETA = 10000.0
FORCE_SCORE = 1.0e4
MASK_VALUE = -1.0e30
SSM_WIDTH = 256
SSM_GROUP = 16
SSM_GROUPS = SSM_WIDTH // SSM_GROUP
SSM_STATE = 64
NORM_EPS = 1e-6

LANES = 128
SUBLANES = 8
VMEM_LIMIT = 56 * 1024 * 1024

PROJ_ROWS = 1024
ATTN_Q = 256
ATTN_K = 512
SELECT_Q = 2048
SELECT_SUB = 128
SSM_CHUNK = 8
SSM_HALVES = 2
SSM_BT = 8
SSM_UNROLL = 4


def _dot(a, b):
    return jnp.dot(a, b, preferred_element_type=F32)


def _dot_nt(a, b):
    return lax.dot_general(a, b, (((1,), (1,)), ((), ())), preferred_element_type=F32)


def _dot_hilo(a, b_bf16):
    hi = a.astype(BF16)
    lo = (a - hi.astype(F32)).astype(BF16)
    return _dot(hi, b_bf16) + _dot(lo, b_bf16)


def _sigmoid(x):
    return 1.0 / (1.0 + jnp.exp(-x))


def _silu(x):
    return x * _sigmoid(x)


def _gelu(x):
    return 0.5 * x * (1.0 + jnp.tanh(math.sqrt(2.0 / math.pi) * (x + 0.044715 * (x * x * x))))


def _ada_kernel(c_ref, w_ref, b_ref, o_ref):
    c = c_ref[...]
    o_ref[...] = _dot(_silu(c).astype(BF16), w_ref[...].astype(BF16)) + b_ref[...]


def _ada(c, w, b):
    bsz, d = c.shape
    n = w.shape[1]
    return pl.pallas_call(
        _ada_kernel,
        grid=(n // d,),
        in_specs=[pl.BlockSpec((bsz, d), lambda j: (0, 0)),
                  pl.BlockSpec((d, d), lambda j: (0, j)),
                  pl.BlockSpec((1, d), lambda j: (0, j))],
        out_specs=pl.BlockSpec((bsz, d), lambda j: (0, j)),
        out_shape=jax.ShapeDtypeStruct((bsz, n), F32),
        compiler_params=pltpu.CompilerParams(vmem_limit_bytes=VMEM_LIMIT),
        name="ada",
    )(c, w, b.reshape(1, n))


_C_Q = 0
_C_KC = 512
_C_KS = 640
_C_KW = 768
_C_VC = 896
_C_VS = 1024
_C_VW = 1152
_C_G3 = 1280
_C_Z = 1408
_C_U = 1920
_C_ZS = 2176
_C_END = 2432


def _modulated_norm(x, gain, scale, shift):
    ms = jnp.mean(x * x, axis=-1, keepdims=True)
    return x * lax.rsqrt(ms + NORM_EPS) * gain * (1.0 + scale) + shift


def _proj_kernel(x_ref, gain_ref, scale_ref, shift_ref, cos_ref, sa_ref, sb_ref, w_ref,
                 q_ref, kc_ref, ks0_ref, ks1_ref, kw_ref, vc_ref, vs0_ref, vs1_ref, vw0_ref, vw1_ref,
                 g3_ref, sz_ref, u_ref, szs_ref, stage_ref, *, per_seq):
    x = x_ref[...]
    hb = _modulated_norm(x, gain_ref[...], scale_ref[0], shift_ref[0]).astype(BF16)

    def proj(a, b):
        return _dot(hb, w_ref[:, a:b])

    cos = cos_ref[...]
    sa = sa_ref[...]
    sb = sb_ref[...]

    def rope(t):
        return t * cos + pltpu.roll(t, LANES - 32, 1) * sa + pltpu.roll(t, 32, 1) * sb

    narrow = proj(_C_Q, _C_Z)
    part = lambda a: narrow[:, a:a + LANES]
    for v in range(ATTN_WIDTH // LANES):
        q_ref[:, v * LANES:(v + 1) * LANES] = rope(part(_C_Q + v * LANES)).astype(BF16)
    kw_ref[...] = rope(part(_C_KW)).astype(BF16)
    tm = x.shape[0]
    for out_ref, val in ((kc_ref, rope(part(_C_KC))), (vc_ref, part(_C_VC))):
        stage_ref[...] = val
        for j in range(CMP_STRIDE):
            out_ref[:, j * LANES:(j + 1) * LANES] = stage_ref[pl.ds(j, tm // CMP_STRIDE, stride=CMP_STRIDE), :]
    lane = lax.broadcasted_iota(jnp.int32, (tm, LANES), 1)
    pos = (pl.program_id(0) % per_seq) * tm + lax.broadcasted_iota(jnp.int32, (tm, LANES), 0)
    blk = pos >> int(math.log2(SEL_BLOCK))
    first = lane < HEAD_DIM
    ks = rope(part(_C_KS))
    ks0_ref[...] = jnp.where(first, ks, jnp.where(lane - HEAD_DIM == blk, 1.0, 0.0)).astype(BF16)
    ks1_ref[...] = jnp.where(first, jnp.where(lane == blk, 1.0, 0.0), ks).astype(BF16)
    ones = jnp.ones((tm, LANES), BF16)
    zero = jnp.zeros((tm, LANES), F32)
    for v_val, refs in ((part(_C_VS), (vs0_ref, vs1_ref)), (part(_C_VW), (vw0_ref, vw1_ref))):
        refs[0][:, :LANES] = jnp.where(first, v_val, zero).astype(BF16)
        refs[1][:, :LANES] = jnp.where(first, zero, v_val).astype(BF16)
        refs[0][:, LANES:] = ones
        refs[1][:, LANES:] = ones
    g3_ref[...] = _sigmoid(part(_C_G3))
    sz_ref[...] = _silu(proj(_C_Z, _C_U)).astype(BF16)
    u = proj(_C_U, _C_ZS)
    for hf in range(SSM_HALVES):
        stage_ref[...] = u[:, hf * LANES:(hf + 1) * LANES]
        for j in range(SSM_CHUNK):
            u_ref[hf, :, j * LANES:(j + 1) * LANES] = stage_ref[
                pl.ds(j, tm // SSM_CHUNK, stride=SSM_CHUNK), :].astype(BF16)
    szs_ref[...] = _silu(proj(_C_ZS, _C_END)).astype(BF16)


def _proj(x2, gain, scale, shift, cos_t, sa_t, sb_t, w, seq):
    t, d = x2.shape
    tm = PROJ_ROWS
    per_seq = seq // tm
    row = lambda n: pl.BlockSpec((tm, n), lambda i: (i, 0))
    mod = pl.BlockSpec((1, 1, d), lambda i: (i // per_seq, 0, 0))
    tab = pl.BlockSpec((tm, LANES), lambda i: (i % per_seq, 0))
    cmp_w, ssm_w = CMP_STRIDE * LANES, SSM_CHUNK * LANES
    out_shape = [
        jax.ShapeDtypeStruct((t, ATTN_WIDTH), BF16),
        jax.ShapeDtypeStruct((t // CMP_STRIDE, cmp_w), F32),
        jax.ShapeDtypeStruct((t, LANES), BF16),
        jax.ShapeDtypeStruct((t, LANES), BF16),
        jax.ShapeDtypeStruct((t, LANES), BF16),
        jax.ShapeDtypeStruct((t // CMP_STRIDE, cmp_w), F32),
        jax.ShapeDtypeStruct((t, 2 * LANES), BF16),
        jax.ShapeDtypeStruct((t, 2 * LANES), BF16),
        jax.ShapeDtypeStruct((t, 2 * LANES), BF16),
        jax.ShapeDtypeStruct((t, 2 * LANES), BF16),
        jax.ShapeDtypeStruct((t, LANES), F32),
        jax.ShapeDtypeStruct((t, ATTN_WIDTH), BF16),
        jax.ShapeDtypeStruct((SSM_HALVES, t // SSM_CHUNK, ssm_w), BF16),
        jax.ShapeDtypeStruct((t, SSM_WIDTH), BF16),
    ]
    cmp_rows = pl.BlockSpec((tm // CMP_STRIDE, cmp_w), lambda i: (i, 0))
    out_specs = [row(ATTN_WIDTH), cmp_rows, row(LANES), row(LANES), row(LANES), cmp_rows] + [row(2 * LANES)] * 4 + [row(LANES)] + [
                 row(ATTN_WIDTH),
                 pl.BlockSpec((SSM_HALVES, tm // SSM_CHUNK, ssm_w), lambda i: (0, i, 0)),
                 row(SSM_WIDTH)]
    return pl.pallas_call(
        functools.partial(_proj_kernel, per_seq=per_seq),
        grid=(t // tm,),
        in_specs=[row(d), pl.BlockSpec((1, d), lambda i: (0, 0)), mod, mod, tab, tab, tab,
                  pl.BlockSpec((d, _C_END), lambda i: (0, 0))],
        out_specs=out_specs,
        out_shape=out_shape,
        scratch_shapes=[pltpu.VMEM((tm, LANES), F32)],
        compiler_params=pltpu.CompilerParams(vmem_limit_bytes=VMEM_LIMIT),
        name="proj",
    )(x2, gain, scale, shift, cos_t, sa_t, sb_t, w)


def _compress_kernel(rk_ref, rv_ref, pea_k, peb_k, wa_k, wb_k, b1_k, w2_k,
                     pea_v, peb_v, wa_v, wb_v, b1_v, w2_v, kc_ref, vc_ref):
    def one(r_ref, pea, peb, wa, wb, b1, w2):
        r = r_ref[0]
        first = _dot((r + pea[...]).astype(BF16), wa[...])
        second = _dot((r + peb[...]).astype(BF16), wb[...])
        nxt = pltpu.roll(second, second.shape[0] - 1, 0)
        hid = _gelu(first + nxt + b1[...])
        return _dot(hid.astype(BF16), w2[...])

    kc_ref[0] = one(rk_ref, pea_k, peb_k, wa_k, wb_k, b1_k, w2_k)
    vc_ref[0] = one(rv_ref, pea_v, peb_v, wa_v, wb_v, b1_v, w2_v)


def _compress(rk, rv, pk, pv):
    bsz, nch, width = rk.shape
    full = lambda a: pl.BlockSpec(a.shape, lambda b: (0,) * a.ndim)
    rspec = pl.BlockSpec((1, nch, width), lambda b: (b, 0, 0))
    ospec = pl.BlockSpec((1, nch, LANES), lambda b: (b, 0, 0))
    return pl.pallas_call(
        _compress_kernel,
        grid=(bsz,),
        in_specs=[rspec, rspec] + [full(a) for a in pk] + [full(a) for a in pv],
        out_specs=[ospec, ospec],
        out_shape=[jax.ShapeDtypeStruct((bsz, nch, LANES), F32)] * 2,
        compiler_params=pltpu.CompilerParams(vmem_limit_bytes=VMEM_LIMIT),
        name="compress",
    )(rk, rv, *pk, *pv)


def _compress_params(pe, w1, b1, w2):
    half = CMP_BLOCK // 2
    eye = jnp.eye(N_GROUPS, dtype=F32)
    w1r = w1.reshape(CMP_BLOCK, HEAD_DIM, CMP_HIDDEN)

    def expand(wl):
        return jnp.einsum('ldj,gh->lgdhj', wl, eye).reshape(half * KV_WIDTH, N_GROUPS * CMP_HIDDEN)

    def pe_lanes(p):
        return jnp.broadcast_to(p[:, None, :], (half, N_GROUPS, HEAD_DIM)).reshape(1, half * KV_WIDTH)

    w2e = jnp.einsum('jd,gh->gjhd', w2, eye).reshape(N_GROUPS * CMP_HIDDEN, KV_WIDTH)
    return (pe_lanes(pe[:half]), pe_lanes(pe[half:]),
            expand(w1r[:half]).astype(BF16), expand(w1r[half:]).astype(BF16),
            jnp.tile(b1, N_GROUPS).reshape(1, N_GROUPS * CMP_HIDDEN), w2e.astype(BF16))


def _select_kernel(q_ref, kc_ref, vc_ref, g3_ref, ovlt_ref, egc_ref, mask_ref, ocmp_ref, gate_ref, *, seq):
    ts = SELECT_SUB
    hg = HEADS_PER_GROUP
    rows = hg * ts
    n_sel_blocks = seq // SEL_BLOCK
    sel_shift = int(math.log2(SEL_BLOCK))
    base = pl.program_id(1) * SELECT_Q
    lane = lax.broadcasted_iota(jnp.int32, (ts, LANES), 1)
    row = lax.broadcasted_iota(jnp.int32, (ts, LANES), 0)
    jidx = lax.broadcasted_iota(jnp.int32, (n_sel_blocks, ts), 0)
    qcol = lax.broadcasted_iota(jnp.int32, (n_sel_blocks, ts), 1)
    jloc = lax.broadcasted_iota(jnp.int32, (SUBLANES, ts), 0)
    groups = range(N_GROUPS)
    in_group = [(lane >= HEAD_DIM * g) & (lane < HEAD_DIM * (g + 1)) for g in groups]
    first_half = lane < HEAD_DIM
    zero = jnp.zeros((ts, LANES), BF16)
    kcb = kc_ref[0].astype(BF16)
    vcb = vc_ref[0].astype(BF16)
    subs = range(SELECT_Q // ts)
    units = [(s, g) for s in subs for g in groups]
    sc, p_cmp, o_cmp, imp, maskq = {}, {}, {}, {}, {}

    def scores(u):
        s, g = u
        qh = [q_ref[s * ts:(s + 1) * ts, h * LANES:(h + 1) * LANES] for h in range(hg)]
        qg = jnp.concatenate([jnp.where(in_group[g], x, zero) for x in qh], axis=0)
        sc[u] = _dot_nt(qg, kcb).reshape(hg, ts, LANES)

    def softmax(u):
        s, g = u
        cvalid = (lane * CMP_STRIDE + (CMP_BLOCK - 1)) <= base + s * ts + row
        sv = jnp.where(cvalid[None], sc[u], MASK_VALUE)
        m = jnp.max(sv, axis=-1, keepdims=True)
        e = jnp.where(cvalid[None], jnp.exp(sv - m), 0.0)
        den = jnp.sum(e, axis=-1, keepdims=True)
        p_cmp[u] = e * (1.0 / jnp.where(den > 0.0, den, 1.0))

    def outputs(u):
        o_cmp[u] = _dot(p_cmp[u].reshape(rows, LANES).astype(BF16), vcb)
        psum = p_cmp[u][0]
        for h in range(1, hg):
            psum = psum + p_cmp[u][h]
        p_hi = psum.astype(BF16)
        p_lo = (psum - p_hi.astype(F32)).astype(BF16)
        imp[u] = (_dot_nt(ovlt_ref[...], p_hi) + _dot_nt(ovlt_ref[...], p_lo))[0:n_sel_blocks]

    def rank(u):
        s, g = u
        t_lane = base + s * ts + qcol
        forced = (jidx == 0) | (jidx == (t_lane >> sel_shift))
        future = jidx * SEL_BLOCK > t_lane
        imp_g = jnp.where(forced, FORCE_SCORE, jnp.where(future, -1.0, imp[u]))
        tiles = [imp_g[k * SUBLANES:(k + 1) * SUBLANES] for k in range(n_sel_blocks // SUBLANES)]
        cnts = [jnp.zeros((SUBLANES, ts), jnp.int32) for _ in tiles]
        for i in range(n_sel_blocks):
            other = jnp.broadcast_to(imp_g[i:i + 1, :], (SUBLANES, ts))
            ki, ri = divmod(i, SUBLANES)
            for k, tile_k in enumerate(tiles):
                if k < ki:
                    ahead = other > tile_k
                elif k > ki:
                    ahead = other >= tile_k
                else:
                    ahead = (other > tile_k) | ((other == tile_k) & (jloc > ri))
                cnts[k] = cnts[k] + jnp.where(ahead, 1, 0)
        cnt = jnp.concatenate(cnts, axis=0)
        mask_t = jnp.where(cnt < min(N_SEL, n_sel_blocks), 0.0, MASK_VALUE)
        lo_rows = HEAD_DIM * (1 - g)
        parts = [mask_t, jnp.zeros((LANES - n_sel_blocks - lo_rows, ts), F32)]
        if lo_rows:
            parts = [jnp.zeros((lo_rows, ts), F32)] + parts
        maskq[u] = jnp.concatenate(parts, axis=0).T

    def store(s):
        rs = slice(s * ts, (s + 1) * ts)
        mask_ref[rs, :] = (maskq[s, 0] + maskq[s, 1]).astype(BF16)
        gexp = _dot_hilo(g3_ref[rs, :], egc_ref[...])
        gate_ref[rs, :] = gexp[:, ATTN_WIDTH:]
        for h in range(hg):
            hr = slice(h * ts, (h + 1) * ts)
            both = jnp.where(first_half, o_cmp[s, 0][hr], o_cmp[s, 1][hr])
            ocmp_ref[rs, h * LANES:(h + 1) * LANES] = gexp[:, h * LANES:(h + 1) * LANES] * both

    stages = (scores, softmax, outputs, rank)
    for step in range(len(units) + len(stages) - 1):
        for k, stage in enumerate(stages):
            if 0 <= step - k < len(units):
                s, g = units[step - k]
                stage((s, g))
                if stage is rank and g == N_GROUPS - 1:
                    store(s)


def _select(q, kc, vc, g3, ovlt, egc, bsz, seq):
    tsel = SELECT_Q
    nq = seq // tsel
    nch = kc.shape[1]
    qrow = lambda n: pl.BlockSpec((tsel, n), lambda b, i: (b * nq + i, 0))
    cmp_spec = pl.BlockSpec((1, nch, LANES), lambda b, i: (b, 0, 0))
    full = lambda a: pl.BlockSpec(a.shape, lambda b, i: (0,) * a.ndim)
    return pl.pallas_call(
        functools.partial(_select_kernel, seq=seq),
        grid=(bsz, nq),
        in_specs=[qrow(ATTN_WIDTH), cmp_spec, cmp_spec, qrow(LANES), full(ovlt), full(egc)],
        out_specs=[qrow(LANES), qrow(ATTN_WIDTH), qrow(2 * ATTN_WIDTH)],
        out_shape=[jax.ShapeDtypeStruct((bsz * seq, LANES), BF16),
                   jax.ShapeDtypeStruct((bsz * seq, ATTN_WIDTH), F32),
                   jax.ShapeDtypeStruct((bsz * seq, 2 * ATTN_WIDTH), F32)],
        compiler_params=pltpu.CompilerParams(vmem_limit_bytes=VMEM_LIMIT),
        name="select",
    )(q, kc, vc, g3, ovlt, egc)


def _attn_kernel(q_ref, mask_ref, ocmp_ref, ks0_ref, ks1_ref, vs0_ref, vs1_ref, kw_ref, vw0_ref, vw1_ref,
                 gate_ref, sz_ref, wbias_ref, dbias_ref, o_ref, slc_ref, win_ref, *, seq):
    tq, tk = ATTN_Q, ATTN_K
    hg = HEADS_PER_GROUP
    rows = hg * tq
    q0 = pl.program_id(1) * tq
    lane = lax.broadcasted_iota(jnp.int32, (tq, LANES), 1)

    span = WINDOW + tq
    start = pl.multiple_of(jnp.maximum(q0 - WINDOW, 0), tq)
    wbias = wbias_ref[0]
    width = ATTN_WIDTH
    ks_refs, vs_refs, vw_refs = (ks0_ref, ks1_ref), (vs0_ref, vs1_ref), (vw0_ref, vw1_ref)

    groups = range(N_GROUPS)
    in_group = [(lane >= HEAD_DIM * g) & (lane < HEAD_DIM * (g + 1)) for g in groups]
    zero = jnp.zeros((tq, LANES), BF16)
    q_heads = [q_ref[:, h * LANES:(h + 1) * LANES] for h in range(hg)]
    qg = [jnp.concatenate([jnp.where(in_group[g], qh, zero) for qh in q_heads], axis=0) for g in groups]
    maskq = mask_ref[...]
    qsel = [jnp.concatenate([jnp.where(in_group[g], qh, maskq) for qh in q_heads], axis=0) for g in groups]

    n_tiles = q0 // tk + 1
    for n in range(1, seq // tk + 1):
        @pl.when(n_tiles == n)
        def _(n=n):
            klen = n * tk
            s2 = [_dot_nt(qsel[g], ks_refs[g][0:klen, :]) for g in groups]
            s3 = [_dot_nt(qg[g], kw_ref[pl.ds(start, span), :]).reshape(hg, tq, span) for g in groups]
            for g in groups:
                last = (s2[g][:, klen - tk:].reshape(hg, tq, tk) + dbias_ref[0][None]).reshape(rows, tk)
                sg = last if n == 1 else jnp.concatenate([s2[g][:, :klen - tk], last], axis=1)
                m2 = jnp.max(sg, axis=-1, keepdims=True)
                p2 = jnp.exp(sg - m2).astype(BF16)
                slc_ref[g] = _dot(p2, vs_refs[g][0:klen, :])
            for g in groups:
                sw = s3[g] + wbias[None]
                m3 = jnp.max(sw, axis=-1, keepdims=True)
                p3 = jnp.exp(sw - m3).astype(BF16).reshape(rows, span)
                win_ref[g] = _dot(p3, vw_refs[g][pl.ds(start, span), :])

    o_slc, o_win = [], []
    for g in groups:
        o_slc.append(slc_ref[g, :, :LANES] * (1.0 / slc_ref[g, :, LANES:]))
        o_win.append(win_ref[g, :, :LANES] * (1.0 / win_ref[g, :, LANES:]))

    first_half = lane < HEAD_DIM
    for h in range(hg):
        rws = slice(h * tq, (h + 1) * tq)
        cols = slice(h * LANES, (h + 1) * LANES)
        acc = ocmp_ref[:, cols]
        for j, branch in enumerate((o_slc, o_win)):
            both = jnp.where(first_half, branch[0][rws], branch[1][rws])
            acc = acc + gate_ref[:, j * width + h * LANES: j * width + (h + 1) * LANES] * both
        o_ref[:, cols] = (acc * sz_ref[:, cols].astype(F32)).astype(BF16)


def _attn(q, maskq, ocmp, ks0, ks1, vs0, vs1, kw, vw0, vw1, gates, sz, wbias, dbias, bsz, seq):
    tq, tk = ATTN_Q, ATTN_K
    nq = seq // tq
    qrow = lambda n: pl.BlockSpec((tq, n), lambda b, i: (b * nq + i, 0))
    per_seq = pl.BlockSpec((seq, LANES), lambda b, i: (b, 0))
    val_seq = pl.BlockSpec((seq, 2 * LANES), lambda b, i: (b, 0))
    diag = pl.BlockSpec((1, tq, tk), lambda b, i: (i % (tk // tq), 0, 0))
    wspec = pl.BlockSpec((1,) + wbias.shape[1:], lambda b, i: (jnp.minimum(i, wbias.shape[0] - 1), 0, 0))
    return pl.pallas_call(
        functools.partial(_attn_kernel, seq=seq),
        grid=(bsz, nq),
        in_specs=[qrow(ATTN_WIDTH), qrow(LANES), qrow(ATTN_WIDTH), per_seq, per_seq, val_seq, val_seq, per_seq,
                  val_seq, val_seq]
                 + [qrow(2 * ATTN_WIDTH), qrow(ATTN_WIDTH), wspec, diag],
        out_specs=qrow(ATTN_WIDTH),
        out_shape=jax.ShapeDtypeStruct((bsz * seq, ATTN_WIDTH), BF16),
        scratch_shapes=[pltpu.VMEM((N_GROUPS, HEADS_PER_GROUP * tq, 2 * LANES), F32)] * 2,
        compiler_params=pltpu.CompilerParams(vmem_limit_bytes=VMEM_LIMIT),
        name="attn",
    )(q, maskq, ocmp, ks0, ks1, vs0, vs1, kw, vw0, vw1, gates, sz, wbias, dbias)


def _ssm_kernel(u_ref, mi_ref, ws_ref, wo_ref, al_ref, y_ref, sx_ref, *, n_chunks):
    u = u_ref[0]
    n_tiles = sx_ref.shape[0]
    nt = n_tiles // 2
    tile = lambda j: slice(j * LANES, (j + 1) * LANES)
    for j in range(n_tiles):
        sx_ref[j] = _dot(u, ws_ref[0, :, tile(j)])
    a_re = [al_ref[0, 0:1, tile(j)] for j in range(nt)]
    a_im = [al_ref[0, 1:2, tile(j)] for j in range(nt)]
    bt = SSM_BT

    def body(k, carry):
        rows = pl.ds(k, bt, stride=n_chunks)
        new = []
        for j in range(nt):
            x_re, x_im = carry[j], carry[nt + j]
            s_re = sx_ref[j, rows, :]
            s_im = sx_ref[nt + j, rows, :]
            sx_ref[j, rows, :] = x_re
            sx_ref[nt + j, rows, :] = x_im
            new.append((a_re[j] * x_re - a_im[j] * x_im + s_re, a_re[j] * x_im + a_im[j] * x_re + s_im))
        return tuple([n[0] for n in new] + [n[1] for n in new])

    zero = jnp.zeros((bt, LANES), F32)
    lax.fori_loop(0, n_chunks, body, (zero,) * n_tiles, unroll=SSM_UNROLL)
    width = u.shape[1]
    ctile = 2 * LANES
    y_cols = [_dot(u[:, :(c + 1) * ctile], mi_ref[0, :(c + 1) * ctile, c * ctile:(c + 1) * ctile])
              for c in range(width // ctile)]
    xs = jnp.concatenate([sx_ref[j].astype(BF16) for j in range(n_tiles)], axis=1)
    y = jnp.concatenate(y_cols, axis=1) + _dot(xs, wo_ref[0])
    for j in range(SSM_CHUNK):
        y_ref[0, pl.ds(j, y.shape[0], stride=SSM_CHUNK), :] = y[:, tile(j)]


def _ssm(u3, mi, ws, wo, al, n_chunks):
    halves, rows, width = u3.shape
    r = SSM_BT * n_chunks
    wspec = lambda a: pl.BlockSpec((1,) + a.shape[1:], lambda hf, i: (hf, 0, 0))
    return pl.pallas_call(
        functools.partial(_ssm_kernel, n_chunks=n_chunks),
        grid=(halves, rows // r),
        in_specs=[pl.BlockSpec((1, r, width), lambda hf, i: (hf, i, 0)),
                  wspec(mi), wspec(ws), wspec(wo), wspec(al)],
        out_specs=pl.BlockSpec((1, r * SSM_CHUNK, LANES), lambda hf, i: (hf, i, 0)),
        out_shape=jax.ShapeDtypeStruct((halves, rows * SSM_CHUNK, LANES), F32),
        scratch_shapes=[pltpu.VMEM((ws.shape[2] // LANES, r, LANES), F32)],
        compiler_params=pltpu.CompilerParams(vmem_limit_bytes=VMEM_LIMIT),
        name="ssm",
    )(u3, mi, ws, wo, al)


def _ssm_params(lam_re, lam_im, log_dt, b_re, b_im, c_re, c_im, d_skip):
    L = SSM_CHUNK
    G, P, C = SSM_GROUPS, SSM_STATE, SSM_GROUP
    gh = G // SSM_HALVES
    dt = jnp.exp(log_dt)[:, None]
    lr, li = lam_re, lam_im
    mag = jnp.exp(lr * dt)
    ab_re = mag * jnp.cos(li * dt)
    ab_im = mag * jnp.sin(li * dt)
    nr, ni = ab_re - 1.0, ab_im
    den = lr * lr + li * li
    cr = ((nr * lr + ni * li) / den)[..., None]
    ci = ((ni * lr - nr * li) / den)[..., None]
    bb_re = cr * b_re - ci * b_im
    bb_im = cr * b_im + ci * b_re
    pr, pi = [jnp.ones_like(ab_re)], [jnp.zeros_like(ab_im)]
    for _ in range(L):
        pr.append(pr[-1] * ab_re - pi[-1] * ab_im)
        pi.append(pr[-2] * ab_im + pi[-1] * ab_re)
    pw_re = jnp.stack(pr)
    pw_im = jnp.stack(pi)
    ca_re = jnp.einsum('gcp,dgp->dgcp', c_re, pw_re[:L]) - jnp.einsum('gcp,dgp->dgcp', c_im, pw_im[:L])
    ca_im = jnp.einsum('gcp,dgp->dgcp', c_re, pw_im[:L]) + jnp.einsum('gcp,dgp->dgcp', c_im, pw_re[:L])
    kern = jnp.einsum('dgop,gpi->dgoi', ca_re, bb_re) - jnp.einsum('dgop,gpi->dgoi', ca_im, bb_im)
    lag = np.arange(L)[None, :] - np.arange(L)[:, None]
    lag_or_zero_block = np.where(lag >= 0, lag, L)
    e_re, e_im = pw_re[:L][::-1], pw_im[:L][::-1]
    ws_re = jnp.einsum('sgp,gpi->sgip', e_re, bb_re) - jnp.einsum('sgp,gpi->sgip', e_im, bb_im)
    ws_im = jnp.einsum('sgp,gpi->sgip', e_re, bb_im) + jnp.einsum('sgp,gpi->sgip', e_im, bb_re)
    o_re, o_im = pw_re[1:L + 1], pw_im[1:L + 1]
    wo_re = jnp.einsum('gop,tgp->tgop', c_re, o_re) - jnp.einsum('gop,tgp->tgop', c_im, o_im)
    wo_im = -(jnp.einsum('gop,tgp->tgop', c_re, o_im) + jnp.einsum('gop,tgp->tgop', c_im, o_re))

    def group_diag(a, rows_per_group, cols_per_group):
        tiled = jnp.tile(a, (1, gh))
        rg = (lax.broadcasted_iota(jnp.int32, tiled.shape, 0) // rows_per_group) % gh
        cg = lax.broadcasted_iota(jnp.int32, tiled.shape, 1) // cols_per_group
        return jnp.where(rg == cg, tiled, 0.0)

    mi_h, ws_h, wo_h, al_h = [], [], [], []
    for hf in range(SSM_HALVES):
        gs = slice(hf * gh, (hf + 1) * gh)
        k_gi_o = kern[:, gs].transpose(0, 1, 3, 2).reshape(L * gh * C, C)
        blocks = group_diag(k_gi_o, C, C).reshape(L, gh * C, gh * C)
        blocks = blocks.at[0].add(jnp.diag(d_skip[hf * gh * C:(hf + 1) * gh * C]))
        blocks = jnp.concatenate([blocks, jnp.zeros((1, gh * C, gh * C), F32)], axis=0)
        mi_h.append(blocks[lag_or_zero_block].transpose(0, 2, 1, 3).reshape(L * gh * C, L * gh * C))
        wsr = group_diag(ws_re[:, gs].reshape(L * gh * C, P), C, P)
        wsi = group_diag(ws_im[:, gs].reshape(L * gh * C, P), C, P)
        ws_h.append(jnp.concatenate([wsr, wsi], axis=1))
        wor = group_diag(wo_re[:, gs].reshape(L * gh * C, P), C, P).T
        woi = group_diag(wo_im[:, gs].reshape(L * gh * C, P), C, P).T
        wo_h.append(jnp.concatenate([wor, woi], axis=0))
        al_h.append(jnp.stack([pw_re[L, gs].reshape(gh * P), pw_im[L, gs].reshape(gh * P)]))
    return (jnp.stack(mi_h).astype(BF16), jnp.stack(ws_h).astype(BF16), jnp.stack(wo_h).astype(BF16),
            jnp.stack(al_h))


def _tail_kernel(x_ref, oa_ref, y_ref, szs_ref, gain_ref, scale_ref, shift_ref, gate_ref, wm_ref, wglu_ref,
                 wpa_ref, wps_ref, wout_ref, fg_ref, o_ref):
    x = x_ref[...]
    d = x.shape[1]
    hb = _modulated_norm(x, gain_ref[...], scale_ref[0], shift_ref[0]).astype(BF16)
    mg = _sigmoid(_dot(hb, wm_ref[...]))
    y = jnp.concatenate([y_ref[0], y_ref[1]], axis=-1)
    yy = _dot(_gelu(y).astype(BF16), wglu_ref[...])
    o_ssm = yy[:, :SSM_WIDTH] * _sigmoid(yy[:, SSM_WIDTH:]) * szs_ref[...].astype(F32)
    merged = (mg[:, :d] * _dot(oa_ref[...], wpa_ref[...])
              + mg[:, d:] * _dot(o_ssm.astype(BF16), wps_ref[...]))
    xo = x + gate_ref[0] * _dot(merged.astype(BF16), wout_ref[...])
    ms = jnp.mean(xo * xo, axis=-1, keepdims=True)
    o_ref[...] = xo * lax.rsqrt(ms + NORM_EPS) * fg_ref[...]


def _tail(x2, oa, y3, szs, gain, scale, shift, gate, wm, wglu, wpa, wps, wout, fgain, seq):
    t, d = x2.shape
    tm = PROJ_ROWS
    per_seq = seq // tm
    row = lambda n: pl.BlockSpec((tm, n), lambda i: (i, 0))
    full = lambda a: pl.BlockSpec(a.shape, lambda i: (0,) * a.ndim)
    mod = pl.BlockSpec((1, 1, d), lambda i: (i // per_seq, 0, 0))
    return pl.pallas_call(
        _tail_kernel,
        grid=(t // tm,),
        in_specs=[row(d), row(ATTN_WIDTH), pl.BlockSpec((SSM_HALVES, tm, LANES), lambda i: (0, i, 0)),
                  row(SSM_WIDTH), full(gain), mod, mod, mod,
                  full(wm), full(wglu), full(wpa), full(wps), full(wout), full(fgain)],
        out_specs=row(d),
        out_shape=jax.ShapeDtypeStruct((t, d), F32),
        compiler_params=pltpu.CompilerParams(vmem_limit_bytes=VMEM_LIMIT),
        name="tail",
    )(x2, oa, y3, szs, gain, scale, shift, gate, wm, wglu, wpa, wps, wout, fgain)


def _rope_tables(seq):
    half = HEAD_DIM // 2
    inv_freq = ROPE_THETA ** (-jnp.arange(half, dtype=F32) / half)
    ang = jnp.arange(seq, dtype=F32)[:, None] * inv_freq[None, :]
    cos, sin = jnp.cos(ang), jnp.sin(ang)
    zero = jnp.zeros_like(sin)
    reps = LANES // HEAD_DIM
    cos_t = jnp.tile(jnp.concatenate([cos, cos], axis=1), (1, reps))
    sa_t = jnp.tile(jnp.concatenate([-sin, zero], axis=1), (1, reps))
    sb_t = jnp.tile(jnp.concatenate([zero, sin], axis=1), (1, reps))
    return cos_t, sa_t, sb_t


def _heads_to_lanes(w, axis):
    shape = w.shape
    split = shape[:axis] + (N_GROUPS, HEADS_PER_GROUP, HEAD_DIM) + shape[axis + 1:]
    return jnp.swapaxes(w.reshape(split), axis, axis + 1).reshape(shape)


def _proj_weight(w_in):
    d = w_in.shape[0]
    o_q, o_kv, o_g, o_z, o_u, o_zs, o_m = 0, 512, 1280, 1304, 1816, 2072, 2328
    kv = lambda j: w_in[:, o_kv + j * KV_WIDTH: o_kv + (j + 1) * KV_WIDTH]
    gates = jnp.pad(w_in[:, o_g:o_z], ((0, 0), (0, LANES - 3 * N_HEADS)))
    cols = [_heads_to_lanes(w_in[:, o_q:o_kv], 1) * (HEAD_DIM ** -0.5),
            kv(0), kv(2), kv(4), kv(1), kv(3), kv(5), gates,
            _heads_to_lanes(w_in[:, o_z:o_u], 1), w_in[:, o_u:o_zs], w_in[:, o_zs:o_m]]
    w = jnp.concatenate(cols, axis=1)
    assert w.shape == (d, _C_END)
    return w.astype(BF16), w_in[:, o_m:].astype(BF16)


def _attn_constants(seq):
    n_cmp = LANES
    cs = np.arange(n_cmp) * CMP_STRIDE
    ss = np.arange(LANES) * SEL_BLOCK
    ovl = (np.minimum(cs[:, None] + CMP_BLOCK, ss[None, :] + SEL_BLOCK) > np.maximum(cs[:, None], ss[None, :]))
    ovl = ovl & (np.arange(LANES)[None, :] < seq // SEL_BLOCK) & (cs[:, None] + CMP_BLOCK <= seq)
    tq, tk = ATTN_Q, ATTN_K
    v = np.arange(tk // tq)[:, None, None]
    r = np.arange(tq)[None, :, None]
    cc = np.arange(tk)[None, None, :]
    dbias = np.where(cc <= v * tq + r, 0.0, MASK_VALUE).astype(np.float32)
    eg = np.zeros((LANES, 3 * ATTN_WIDTH), np.float32)
    for g in range(N_GROUPS):
        for h in range(HEADS_PER_GROUP):
            for j in range(3):
                base = j * ATTN_WIDTH + h * LANES + g * HEAD_DIM
                eg[(g * HEADS_PER_GROUP + h) * 3 + j, base:base + HEAD_DIM] = 1.0
    span = WINDOW + tq
    q0 = (np.arange(WINDOW // tq + 1) * tq)[:, None, None]
    kp = np.maximum(q0 - WINDOW, 0) + np.arange(span)[None, None, :]
    tt = q0 + np.arange(tq)[None, :, None]
    wbias = np.where((kp <= tt) & (kp > tt - WINDOW), 0.0, MASK_VALUE).astype(np.float32)
    return (jnp.asarray(ovl.T, BF16), jnp.asarray(eg, BF16), jnp.asarray(wbias, F32), jnp.asarray(dbias, F32))


def kernel(x, c, w_ada, b_ada, norm_gain, w_in, pe_cmp_k, w_cmp_k1, b_cmp_k1, w_cmp_k2, pe_cmp_v, w_cmp_v1,
           b_cmp_v1, w_cmp_v2, lam_re, lam_im, log_dt, b_re, b_im, c_re, c_im, d_skip, w_glu, w_proj_attn,
           w_proj_ssm, w_out, final_gain):
    bsz, seq, d = x.shape
    depth = w_in.shape[0]
    assert depth == 1, "the tail kernel fuses the final norm into the (single) layer"
    assert bsz % SSM_BT == 0 and seq % SSM_CHUNK == 0
    assert seq % ATTN_K == 0 and seq % SELECT_Q == 0 and seq % PROJ_ROWS == 0 and seq >= WINDOW + ATTN_Q
    assert (seq - CMP_BLOCK) // CMP_STRIDE + 1 <= LANES and seq // CMP_STRIDE == LANES
    t = bsz * seq
    cos_t, sa_t, sb_t = _rope_tables(seq)
    ovlt, eg, wbias, dbias = _attn_constants(seq)
    n_chunks = seq // SSM_CHUNK

    x2 = x.reshape(t, d)
    for l in range(depth):
        mod = _ada(c, w_ada[l], b_ada[l])
        shift, mscale, gate = [m.reshape(bsz, 1, d) for m in jnp.split(mod, 3, axis=-1)]
        gain = norm_gain[l].reshape(1, d)
        w_proj, w_merge = _proj_weight(w_in[l])
        (q, kc_r, ks0, ks1, kw, vc_r, vs0, vs1, vw0, vw1, g3, sz, u, szs) = _proj(
            x2, gain, mscale, shift, cos_t, sa_t, sb_t, w_proj, seq)

        nch = seq // CMP_STRIDE
        kc, vc = _compress(
            kc_r.reshape(bsz, nch, CMP_STRIDE * LANES), vc_r.reshape(bsz, nch, CMP_STRIDE * LANES),
            _compress_params(pe_cmp_k[l], w_cmp_k1[l], b_cmp_k1[l], w_cmp_k2[l]),
            _compress_params(pe_cmp_v[l], w_cmp_v1[l], b_cmp_v1[l], w_cmp_v2[l]))
        maskq, ocmp, gates = _select(q, kc, vc, g3, ovlt, eg, bsz, seq)
        o_attn = _attn(q, maskq, ocmp, ks0, ks1, vs0, vs1, kw, vw0, vw1, gates, sz, wbias, dbias, bsz, seq)

        mi, ws, wo, al = _ssm_params(lam_re[l], lam_im[l], log_dt[l], b_re[l], b_im[l], c_re[l], c_im[l],
                                     d_skip[l])
        y3 = _ssm(u, mi, ws, wo, al, n_chunks)

        x2 = _tail(x2, o_attn, y3, szs, gain, mscale, shift, gate, w_merge, w_glu[l].astype(BF16),
                   _heads_to_lanes(w_proj_attn[l], 0).astype(BF16), w_proj_ssm[l].astype(BF16),
                   w_out[l].astype(BF16),
                   final_gain.reshape(1, d) if l == depth - 1 else jnp.ones((1, d), F32), seq)
    return x2.reshape(bsz, seq, d)
```

```python
import functools
import math

import jax
import jax.numpy as jnp
import numpy as np
from jax import lax
from jax.experimental import pallas as pl
from jax.experimental.pallas import tpu as pltpu

F32 = jnp.float32
BF16 = jnp.bfloat16

N_HEADS = 8
N_GROUPS = 2
HEADS_PER_GROUP = N_HEADS // N_GROUPS
HEAD_DIM = 64
ATTN_WIDTH = N_HEADS * HEAD_DIM
KV_WIDTH = N_GROUPS * HEAD_DIM
CMP_BLOCK = 32
CMP_STRIDE = 16
CMP_HIDDEN = 2 * HEAD_DIM
SEL_BLOCK = 64
N_SEL = 16
WINDOW = 512
ROPE_THETA = 10000.0
FORCE_SCORE = 1.0e4
MASK_VALUE = -1.0e30
SSM_WIDTH = 256
SSM_GROUP = 16
SSM_GROUPS = SSM_WIDTH // SSM_GROUP
SSM_STATE = 64
NORM_EPS = 1e-6

LANES = 128
SUBLANES = 8
VMEM_LIMIT = 56 * 1024 * 1024

PROJ_ROWS = 1024
ATTN_Q = 256
ATTN_K = 512
SELECT_Q = 2048
SELECT_SUB = 128
SSM_CHUNK = 8
SSM_HALVES = 2
SSM_BT = 8
SSM_UNROLL = 4


def _dot(a, b):
    return jnp.dot(a, b, preferred_element_type=F32)


def _dot_nt(a, b):
    return lax.dot_general(a, b, (((1,), (1,)), ((), ())), preferred_element_type=F32)


def _dot_hilo(a, b_bf16):
    hi = a.astype(BF16)
    lo = (a - hi.astype(F32)).astype(BF16)
    return _dot(hi, b_bf16) + _dot(lo, b_bf16)


def _sigmoid(x):
    return 1.0 / (1.0 + jnp.exp(-x))


def _silu(x):
    return x * _sigmoid(x)


def _gelu(x):
    return 0.5 * x * (1.0 + jnp.tanh(math.sqrt(2.0 / math.pi) * (x + 0.044715 * (x * x * x))))


def _ada_kernel(c_ref, w_ref, b_ref, o_ref):
    c = c_ref[...]
    o_ref[...] = _dot(_silu(c).astype(BF16), w_ref[...].astype(BF16)) + b_ref[...]


def _ada(c, w, b):
    bsz, d = c.shape
    n = w.shape[1]
    return pl.pallas_call(
        _ada_kernel,
        grid=(n // d,),
        in_specs=[pl.BlockSpec((bsz, d), lambda j: (0, 0)),
                  pl.BlockSpec((d, d), lambda j: (0, j)),
                  pl.BlockSpec((1, d), lambda j: (0, j))],
        out_specs=pl.BlockSpec((bsz, d), lambda j: (0, j)),
        out_shape=jax.ShapeDtypeStruct((bsz, n), F32),
        compiler_params=pltpu.CompilerParams(vmem_limit_bytes=VMEM_LIMIT),
        name="ada",
    )(c, w, b.reshape(1, n))


_C_Q = 0
_C_KC = 512
_C_KS = 640
_C_KW = 768
_C_VC = 896
_C_VS = 1024
_C_VW = 1152
_C_G3 = 1280
_C_Z = 1408
_C_U = 1920
_C_ZS = 2176
_C_END = 2432


def _modulated_norm(x, gain, scale, shift):
    ms = jnp.mean(x * x, axis=-1, keepdims=True)
    return x * lax.rsqrt(ms + NORM_EPS) * gain * (1.0 + scale) + shift


def _proj_kernel(x_ref, gain_ref, scale_ref, shift_ref, cos_ref, sa_ref, sb_ref, w_ref,
                 q_ref, kc_ref, ks0_ref, ks1_ref, kw_ref, vc_ref, vs0_ref, vs1_ref, vw0_ref, vw1_ref,
                 g3_ref, sz_ref, u_ref, szs_ref, stage_ref, *, per_seq):
    x = x_ref[...]
    hb = _modulated_norm(x, gain_ref[...], scale_ref[0], shift_ref[0]).astype(BF16)

    def proj(a, b):
        return _dot(hb, w_ref[:, a:b])

    cos = cos_ref[...]
    sa = sa_ref[...]
    sb = sb_ref[...]

    def rope(t):
        return t * cos + pltpu.roll(t, LANES - 32, 1) * sa + pltpu.roll(t, 32, 1) * sb

    narrow = proj(_C_Q, _C_Z)
    part = lambda a: narrow[:, a:a + LANES]
    for v in range(ATTN_WIDTH // LANES):
        q_ref[:, v * LANES:(v + 1) * LANES] = rope(part(_C_Q + v * LANES)).astype(BF16)
    kw_ref[...] = rope(part(_C_KW)).astype(BF16)
    tm = x.shape[0]
    for out_ref, val in ((kc_ref, rope(part(_C_KC))), (vc_ref, part(_C_VC))):
        stage_ref[...] = val
        for j in range(CMP_STRIDE):
            out_ref[:, j * LANES:(j + 1) * LANES] = stage_ref[pl.ds(j, tm // CMP_STRIDE, stride=CMP_STRIDE), :]
    lane = lax.broadcasted_iota(jnp.int32, (tm, LANES), 1)
    pos = (pl.program_id(0) % per_seq) * tm + lax.broadcasted_iota(jnp.int32, (tm, LANES), 0)
    blk = pos >> int(math.log2(SEL_BLOCK))
    first = lane < HEAD_DIM
    ks = rope(part(_C_KS))
    ks0_ref[...] = jnp.where(first, ks, jnp.where(lane - HEAD_DIM == blk, 1.0, 0.0)).astype(BF16)
    ks1_ref[...] = jnp.where(first, jnp.where(lane == blk, 1.0, 0.0), ks).astype(BF16)
    ones = jnp.ones((tm, LANES), BF16)
    zero = jnp.zeros((tm, LANES), F32)
    for v_val, refs in ((part(_C_VS), (vs0_ref, vs1_ref)), (part(_C_VW), (vw0_ref, vw1_ref))):
        refs[0][:, :LANES] = jnp.where(first, v_val, zero).astype(BF16)
        refs[1][:, :LANES] = jnp.where(first, zero, v_val).astype(BF16)
        refs[0][:, LANES:] = ones
        refs[1][:, LANES:] = ones
    g3_ref[...] = _sigmoid(part(_C_G3))
    sz_ref[...] = _silu(proj(_C_Z, _C_U)).astype(BF16)
    u = proj(_C_U, _C_ZS)
    for hf in range(SSM_HALVES):
        stage_ref[...] = u[:, hf * LANES:(hf + 1) * LANES]
        for j in range(SSM_CHUNK):
            u_ref[hf, :, j * LANES:(j + 1) * LANES] = stage_ref[
                pl.ds(j, tm // SSM_CHUNK, stride=SSM_CHUNK), :].astype(BF16)
    szs_ref[...] = _silu(proj(_C_ZS, _C_END)).astype(BF16)


def _proj(x2, gain, scale, shift, cos_t, sa_t, sb_t, w, seq):
    t, d = x2.shape
    tm = PROJ_ROWS
    per_seq = seq // tm
    row = lambda n: pl.BlockSpec((tm, n), lambda i: (i, 0))
    mod = pl.BlockSpec((1, 1, d), lambda i: (i // per_seq, 0, 0))
    tab = pl.BlockSpec((tm, LANES), lambda i: (i % per_seq, 0))
    cmp_w, ssm_w = CMP_STRIDE * LANES, SSM_CHUNK * LANES
    out_shape = [
        jax.ShapeDtypeStruct((t, ATTN_WIDTH), BF16),
        jax.ShapeDtypeStruct((t // CMP_STRIDE, cmp_w), F32),
        jax.ShapeDtypeStruct((t, LANES), BF16),
        jax.ShapeDtypeStruct((t, LANES), BF16),
        jax.ShapeDtypeStruct((t, LANES), BF16),
        jax.ShapeDtypeStruct((t // CMP_STRIDE, cmp_w), F32),
        jax.ShapeDtypeStruct((t, 2 * LANES), BF16),
        jax.ShapeDtypeStruct((t, 2 * LANES), BF16),
        jax.ShapeDtypeStruct((t, 2 * LANES), BF16),
        jax.ShapeDtypeStruct((t, 2 * LANES), BF16),
        jax.ShapeDtypeStruct((t, LANES), F32),
        jax.ShapeDtypeStruct((t, ATTN_WIDTH), BF16),
        jax.ShapeDtypeStruct((SSM_HALVES, t // SSM_CHUNK, ssm_w), BF16),
        jax.ShapeDtypeStruct((t, SSM_WIDTH), BF16),
    ]
    cmp_rows = pl.BlockSpec((tm // CMP_STRIDE, cmp_w), lambda i: (i, 0))
    out_specs = [row(ATTN_WIDTH), cmp_rows, row(LANES), row(LANES), row(LANES), cmp_rows] + [row(2 * LANES)] * 4 + [row(LANES)] + [
                 row(ATTN_WIDTH),
                 pl.BlockSpec((SSM_HALVES, tm // SSM_CHUNK, ssm_w), lambda i: (0, i, 0)),
                 row(SSM_WIDTH)]
    return pl.pallas_call(
        functools.partial(_proj_kernel, per_seq=per_seq),
        grid=(t // tm,),
        in_specs=[row(d), pl.BlockSpec((1, d), lambda i: (0, 0)), mod, mod, tab, tab, tab,
                  pl.BlockSpec((d, _C_END), lambda i: (0, 0))],
        out_specs=out_specs,
        out_shape=out_shape,
        scratch_shapes=[pltpu.VMEM((tm, LANES), F32)],
        compiler_params=pltpu.CompilerParams(vmem_limit_bytes=VMEM_LIMIT),
        name="proj",
    )(x2, gain, scale, shift, cos_t, sa_t, sb_t, w)


def _compress_kernel(rk_ref, rv_ref, pea_k, peb_k, wa_k, wb_k, b1_k, w2_k,
                     pea_v, peb_v, wa_v, wb_v, b1_v, w2_v, kc_ref, vc_ref):
    def one(r_ref, pea, peb, wa, wb, b1, w2):
        r = r_ref[0]
        first = _dot((r + pea[...]).astype(BF16), wa[...])
        second = _dot((r + peb[...]).astype(BF16), wb[...])
        nxt = pltpu.roll(second, second.shape[0] - 1, 0)
        hid = _gelu(first + nxt + b1[...])
        return _dot(hid.astype(BF16), w2[...])

    kc_ref[0] = one(rk_ref, pea_k, peb_k, wa_k, wb_k, b1_k, w2_k)
    vc_ref[0] = one(rv_ref, pea_v, peb_v, wa_v, wb_v, b1_v, w2_v)


def _compress(rk, rv, pk, pv):
    bsz, nch, width = rk.shape
    full = lambda a: pl.BlockSpec(a.shape, lambda b: (0,) * a.ndim)
    rspec = pl.BlockSpec((1, nch, width), lambda b: (b, 0, 0))
    ospec = pl.BlockSpec((1, nch, LANES), lambda b: (b, 0, 0))
    return pl.pallas_call(
        _compress_kernel,
        grid=(bsz,),
        in_specs=[rspec, rspec] + [full(a) for a in pk] + [full(a) for a in pv],
        out_specs=[ospec, ospec],
        out_shape=[jax.ShapeDtypeStruct((bsz, nch, LANES), F32)] * 2,
        compiler_params=pltpu.CompilerParams(vmem_limit_bytes=VMEM_LIMIT),
        name="compress",
    )(rk, rv, *pk, *pv)


def _compress_params(pe, w1, b1, w2):
    half = CMP_BLOCK // 2
    eye = jnp.eye(N_GROUPS, dtype=F32)
    w1r = w1.reshape(CMP_BLOCK, HEAD_DIM, CMP_HIDDEN)

    def expand(wl):
        return jnp.einsum('ldj,gh->lgdhj', wl, eye).reshape(half * KV_WIDTH, N_GROUPS * CMP_HIDDEN)

    def pe_lanes(p):
        return jnp.broadcast_to(p[:, None, :], (half, N_GROUPS, HEAD_DIM)).reshape(1, half * KV_WIDTH)

    w2e = jnp.einsum('jd,gh->gjhd', w2, eye).reshape(N_GROUPS * CMP_HIDDEN, KV_WIDTH)
    return (pe_lanes(pe[:half]), pe_lanes(pe[half:]),
            expand(w1r[:half]).astype(BF16), expand(w1r[half:]).astype(BF16),
            jnp.tile(b1, N_GROUPS).reshape(1, N_GROUPS * CMP_HIDDEN), w2e.astype(BF16))


def _select_kernel(q_ref, kc_ref, vc_ref, g3_ref, ovlt_ref, egc_ref, mask_ref, ocmp_ref, gate_ref, *, seq):
    ts = SELECT_SUB
    hg = HEADS_PER_GROUP
    rows = hg * ts
    n_sel_blocks = seq // SEL_BLOCK
    sel_shift = int(math.log2(SEL_BLOCK))
    base = pl.program_id(1) * SELECT_Q
    lane = lax.broadcasted_iota(jnp.int32, (ts, LANES), 1)
    row = lax.broadcasted_iota(jnp.int32, (ts, LANES), 0)
    jidx = lax.broadcasted_iota(jnp.int32, (n_sel_blocks, ts), 0)
    qcol = lax.broadcasted_iota(jnp.int32, (n_sel_blocks, ts), 1)
    jloc = lax.broadcasted_iota(jnp.int32, (SUBLANES, ts), 0)
    groups = range(N_GROUPS)
    in_group = [(lane >= HEAD_DIM * g) & (lane < HEAD_DIM * (g + 1)) for g in groups]
    first_half = lane < HEAD_DIM
    zero = jnp.zeros((ts, LANES), BF16)
    kcb = kc_ref[0].astype(BF16)
    vcb = vc_ref[0].astype(BF16)
    subs = range(SELECT_Q // ts)
    units = [(s, g) for s in subs for g in groups]
    sc, p_cmp, o_cmp, imp, maskq = {}, {}, {}, {}, {}

    def scores(u):
        s, g = u
        qh = [q_ref[s * ts:(s + 1) * ts, h * LANES:(h + 1) * LANES] for h in range(hg)]
        qg = jnp.concatenate([jnp.where(in_group[g], x, zero) for x in qh], axis=0)
        sc[u] = _dot_nt(qg, kcb).reshape(hg, ts, LANES)

    def softmax(u):
        s, g = u
        cvalid = (lane * CMP_STRIDE + (CMP_BLOCK - 1)) <= base + s * ts + row
        sv = jnp.where(cvalid[None], sc[u], MASK_VALUE)
        m = jnp.max(sv, axis=-1, keepdims=True)
        e = jnp.where(cvalid[None], jnp.exp(sv - m), 0.0)
        den = jnp.sum(e, axis=-1, keepdims=True)
        p_cmp[u] = e * (1.0 / jnp.where(den > 0.0, den, 1.0))

    def outputs(u):
        o_cmp[u] = _dot(p_cmp[u].reshape(rows, LANES).astype(BF16), vcb)
        psum = p_cmp[u][0]
        for h in range(1, hg):
            psum = psum + p_cmp[u][h]
        p_hi = psum.astype(BF16)
        p_lo = (psum - p_hi.astype(F32)).astype(BF16)
        imp[u] = (_dot_nt(ovlt_ref[...], p_hi) + _dot_nt(ovlt_ref[...], p_lo))[0:n_sel_blocks]

    def rank(u):
        s, g = u
        t_lane = base + s * ts + qcol
        forced = (jidx == 0) | (jidx == (t_lane >> sel_shift))
        future = jidx * SEL_BLOCK > t_lane
        imp_g = jnp.where(forced, FORCE_SCORE, jnp.where(future, -1.0, imp[u]))
        tiles = [imp_g[k * SUBLANES:(k + 1) * SUBLANES] for k in range(n_sel_blocks // SUBLANES)]
        cnts = [jnp.zeros((SUBLANES, ts), jnp.int32) for _ in tiles]
        for i in range(n_sel_blocks):
            other = jnp.broadcast_to(imp_g[i:i + 1, :], (SUBLANES, ts))
            ki, ri = divmod(i, SUBLANES)
            for k, tile_k in enumerate(tiles):
                if k < ki:
                    ahead = other > tile_k
                elif k > ki:
                    ahead = other >= tile_k
                else:
                    ahead = (other > tile_k) | ((other == tile_k) & (jloc > ri))
                cnts[k] = cnts[k] + jnp.where(ahead, 1, 0)
        cnt = jnp.concatenate(cnts, axis=0)
        mask_t = jnp.where(cnt < min(N_SEL, n_sel_blocks), 0.0, MASK_VALUE)
        lo_rows = HEAD_DIM * (1 - g)
        parts = [mask_t, jnp.zeros((LANES - n_sel_blocks - lo_rows, ts), F32)]
        if lo_rows:
            parts = [jnp.zeros((lo_rows, ts), F32)] + parts
        maskq[u] = jnp.concatenate(parts, axis=0).T

    def store(s):
        rs = slice(s * ts, (s + 1) * ts)
        mask_ref[rs, :] = (maskq[s, 0] + maskq[s, 1]).astype(BF16)
        gexp = _dot_hilo(g3_ref[rs, :], egc_ref[...])
        gate_ref[rs, :] = gexp[:, ATTN_WIDTH:]
        for h in range(hg):
            hr = slice(h * ts, (h + 1) * ts)
            both = jnp.where(first_half, o_cmp[s, 0][hr], o_cmp[s, 1][hr])
            ocmp_ref[rs, h * LANES:(h + 1) * LANES] = gexp[:, h * LANES:(h + 1) * LANES] * both

    stages = (scores, softmax, outputs, rank)
    for step in range(len(units) + len(stages) - 1):
        for k, stage in enumerate(stages):
            if 0 <= step - k < len(units):
                s, g = units[step - k]
                stage((s, g))
                if stage is rank and g == N_GROUPS - 1:
                    store(s)


def _select(q, kc, vc, g3, ovlt, egc, bsz, seq):
    tsel = SELECT_Q
    nq = seq // tsel
    nch = kc.shape[1]
    qrow = lambda n: pl.BlockSpec((tsel, n), lambda b, i: (b * nq + i, 0))
    cmp_spec = pl.BlockSpec((1, nch, LANES), lambda b, i: (b, 0, 0))
    full = lambda a: pl.BlockSpec(a.shape, lambda b, i: (0,) * a.ndim)
    return pl.pallas_call(
        functools.partial(_select_kernel, seq=seq),
        grid=(bsz, nq),
        in_specs=[qrow(ATTN_WIDTH), cmp_spec, cmp_spec, qrow(LANES), full(ovlt), full(egc)],
        out_specs=[qrow(LANES), qrow(ATTN_WIDTH), qrow(2 * ATTN_WIDTH)],
        out_shape=[jax.ShapeDtypeStruct((bsz * seq, LANES), BF16),
                   jax.ShapeDtypeStruct((bsz * seq, ATTN_WIDTH), F32),
                   jax.ShapeDtypeStruct((bsz * seq, 2 * ATTN_WIDTH), F32)],
        compiler_params=pltpu.CompilerParams(vmem_limit_bytes=VMEM_LIMIT),
        name="select",
    )(q, kc, vc, g3, ovlt, egc)


def _attn_kernel(q_ref, mask_ref, ocmp_ref, ks0_ref, ks1_ref, vs0_ref, vs1_ref, kw_ref, vw0_ref, vw1_ref,
                 gate_ref, sz_ref, wbias_ref, dbias_ref, o_ref, slc_ref, win_ref, *, seq):
    tq, tk = ATTN_Q, ATTN_K
    hg = HEADS_PER_GROUP
    rows = hg * tq
    q0 = pl.program_id(1) * tq
    lane = lax.broadcasted_iota(jnp.int32, (tq, LANES), 1)

    span = WINDOW + tq
    start = pl.multiple_of(jnp.maximum(q0 - WINDOW, 0), tq)
    wbias = wbias_ref[0]
    width = ATTN_WIDTH
    ks_refs, vs_refs, vw_refs = (ks0_ref, ks1_ref), (vs0_ref, vs1_ref), (vw0_ref, vw1_ref)

    groups = range(N_GROUPS)
    in_group = [(lane >= HEAD_DIM * g) & (lane < HEAD_DIM * (g + 1)) for g in groups]
    zero = jnp.zeros((tq, LANES), BF16)
    q_heads = [q_ref[:, h * LANES:(h + 1) * LANES] for h in range(hg)]
    qg = [jnp.concatenate([jnp.where(in_group[g], qh, zero) for qh in q_heads], axis=0) for g in groups]
    maskq = mask_ref[...]
    qsel = [jnp.concatenate([jnp.where(in_group[g], qh, maskq) for qh in q_heads], axis=0) for g in groups]

    n_tiles = q0 // tk + 1
    for n in range(1, seq // tk + 1):
        @pl.when(n_tiles == n)
        def _(n=n):
            klen = n * tk
            s2 = [_dot_nt(qsel[g], ks_refs[g][0:klen, :]) for g in groups]
            s3 = [_dot_nt(qg[g], kw_ref[pl.ds(start, span), :]).reshape(hg, tq, span) for g in groups]
            for g in groups:
                last = (s2[g][:, klen - tk:].reshape(hg, tq, tk) + dbias_ref[0][None]).reshape(rows, tk)
                sg = last if n == 1 else jnp.concatenate([s2[g][:, :klen - tk], last], axis=1)
                m2 = jnp.max(sg, axis=-1, keepdims=True)
                p2 = jnp.exp(sg - m2).astype(BF16)
                slc_ref[g] = _dot(p2, vs_refs[g][0:klen, :])
            for g in groups:
                sw = s3[g] + wbias[None]
                m3 = jnp.max(sw, axis=-1, keepdims=True)
                p3 = jnp.exp(sw - m3).astype(BF16).reshape(rows, span)
                win_ref[g] = _dot(p3, vw_refs[g][pl.ds(start, span), :])

    o_slc, o_win = [], []
    for g in groups:
        o_slc.append(slc_ref[g, :, :LANES] * (1.0 / slc_ref[g, :, LANES:]))
        o_win.append(win_ref[g, :, :LANES] * (1.0 / win_ref[g, :, LANES:]))

    first_half = lane < HEAD_DIM
    for h in range(hg):
        rws = slice(h * tq, (h + 1) * tq)
        cols = slice(h * LANES, (h + 1) * LANES)
        acc = ocmp_ref[:, cols]
        for j, branch in enumerate((o_slc, o_win)):
            both = jnp.where(first_half, branch[0][rws], branch[1][rws])
            acc = acc + gate_ref[:, j * width + h * LANES: j * width + (h + 1) * LANES] * both
        o_ref[:, cols] = (acc * sz_ref[:, cols].astype(F32)).astype(BF16)


def _attn(q, maskq, ocmp, ks0, ks1, vs0, vs1, kw, vw0, vw1, gates, sz, wbias, dbias, bsz, seq):
    tq, tk = ATTN_Q, ATTN_K
    nq = seq // tq
    qrow = lambda n: pl.BlockSpec((tq, n), lambda b, i: (b * nq + i, 0))
    per_seq = pl.BlockSpec((seq, LANES), lambda b, i: (b, 0))
    val_seq = pl.BlockSpec((seq, 2 * LANES), lambda b, i: (b, 0))
    diag = pl.BlockSpec((1, tq, tk), lambda b, i: (i % (tk // tq), 0, 0))
    wspec = pl.BlockSpec((1,) + wbias.shape[1:], lambda b, i: (jnp.minimum(i, wbias.shape[0] - 1), 0, 0))
    return pl.pallas_call(
        functools.partial(_attn_kernel, seq=seq),
        grid=(bsz, nq),
        in_specs=[qrow(ATTN_WIDTH), qrow(LANES), qrow(ATTN_WIDTH), per_seq, per_seq, val_seq, val_seq, per_seq,
                  val_seq, val_seq]
                 + [qrow(2 * ATTN_WIDTH), qrow(ATTN_WIDTH), wspec, diag],
        out_specs=qrow(ATTN_WIDTH),
        out_shape=jax.ShapeDtypeStruct((bsz * seq, ATTN_WIDTH), BF16),
        scratch_shapes=[pltpu.VMEM((N_GROUPS, HEADS_PER_GROUP * tq, 2 * LANES), F32)] * 2,
        compiler_params=pltpu.CompilerParams(vmem_limit_bytes=VMEM_LIMIT),
        name="attn",
    )(q, maskq, ocmp, ks0, ks1, vs0, vs1, kw, vw0, vw1, gates, sz, wbias, dbias)


def _ssm_kernel(u_ref, mi_ref, ws_ref, wo_ref, al_ref, y_ref, sx_ref, *, n_chunks):
    u = u_ref[0]
    n_tiles = sx_ref.shape[0]
    nt = n_tiles // 2
    tile = lambda j: slice(j * LANES, (j + 1) * LANES)
    bt = SSM_BT
    pitch = sx_ref.shape[1] // bt
    seq_rows = lambda b: pl.ds(b * pitch, n_chunks)
    for j in range(n_tiles):
        inj = _dot(u, ws_ref[0, :, tile(j)])
        for b in range(bt):
            sx_ref[j, seq_rows(b), :] = inj[b * n_chunks:(b + 1) * n_chunks]
    a_re = [al_ref[0, 0:1, tile(j)] for j in range(nt)]
    a_im = [al_ref[0, 1:2, tile(j)] for j in range(nt)]

    def body(k, carry):
        rows = pl.ds(k, bt, stride=pitch)
        new = []
        for j in range(nt):
            x_re, x_im = carry[j], carry[nt + j]
            s_re = sx_ref[j, rows, :]
            s_im = sx_ref[nt + j, rows, :]
            sx_ref[j, rows, :] = x_re
            sx_ref[nt + j, rows, :] = x_im
            new.append((a_re[j] * x_re - a_im[j] * x_im + s_re, a_re[j] * x_im + a_im[j] * x_re + s_im))
        return tuple([n[0] for n in new] + [n[1] for n in new])

    zero = jnp.zeros((bt, LANES), F32)
    lax.fori_loop(0, n_chunks, body, (zero,) * n_tiles, unroll=SSM_UNROLL)
    width = u.shape[1]
    ctile = 2 * LANES
    y_cols = [_dot(u[:, :(c + 1) * ctile], mi_ref[0, :(c + 1) * ctile, c * ctile:(c + 1) * ctile])
              for c in range(width // ctile)]
    xs = jnp.concatenate(
        [jnp.concatenate([sx_ref[j, seq_rows(b), :] for b in range(bt)], axis=0).astype(BF16)
         for j in range(n_tiles)], axis=1)
    y = jnp.concatenate(y_cols, axis=1) + _dot(xs, wo_ref[0])
    for j in range(SSM_CHUNK):
        y_ref[0, pl.ds(j, y.shape[0], stride=SSM_CHUNK), :] = y[:, tile(j)]


def _ssm(u3, mi, ws, wo, al, n_chunks):
    halves, rows, width = u3.shape
    r = SSM_BT * n_chunks
    wspec = lambda a: pl.BlockSpec((1,) + a.shape[1:], lambda hf, i: (hf, 0, 0))
    return pl.pallas_call(
        functools.partial(_ssm_kernel, n_chunks=n_chunks),
        grid=(halves, rows // r),
        in_specs=[pl.BlockSpec((1, r, width), lambda hf, i: (hf, i, 0)),
                  wspec(mi), wspec(ws), wspec(wo), wspec(al)],
        out_specs=pl.BlockSpec((1, r * SSM_CHUNK, LANES), lambda hf, i: (hf, i, 0)),
        out_shape=jax.ShapeDtypeStruct((halves, rows * SSM_CHUNK, LANES), F32),
        scratch_shapes=[pltpu.VMEM((ws.shape[2] // LANES, SSM_BT * (n_chunks + SUBLANES), LANES), F32)],
        compiler_params=pltpu.CompilerParams(vmem_limit_bytes=VMEM_LIMIT),
        name="ssm",
    )(u3, mi, ws, wo, al)


def _ssm_params(lam_re, lam_im, log_dt, b_re, b_im, c_re, c_im, d_skip):
    L = SSM_CHUNK
    G, P, C = SSM_GROUPS, SSM_STATE, SSM_GROUP
    gh = G // SSM_HALVES
    dt = jnp.exp(log_dt)[:, None]
    lr, li = lam_re, lam_im
    mag = jnp.exp(lr * dt)
    ab_re = mag * jnp.cos(li * dt)
    ab_im = mag * jnp.sin(li * dt)
    nr, ni = ab_re - 1.0, ab_im
    den = lr * lr + li * li
    cr = ((nr * lr + ni * li) / den)[..., None]
    ci = ((ni * lr - nr * li) / den)[..., None]
    bb_re = cr * b_re - ci * b_im
    bb_im = cr * b_im + ci * b_re
    pr, pi = [jnp.ones_like(ab_re)], [jnp.zeros_like(ab_im)]
    for _ in range(L):
        pr.append(pr[-1] * ab_re - pi[-1] * ab_im)
        pi.append(pr[-2] * ab_im + pi[-1] * ab_re)
    pw_re = jnp.stack(pr)
    pw_im = jnp.stack(pi)
    ca_re = jnp.einsum('gcp,dgp->dgcp', c_re, pw_re[:L]) - jnp.einsum('gcp,dgp->dgcp', c_im, pw_im[:L])
    ca_im = jnp.einsum('gcp,dgp->dgcp', c_re, pw_im[:L]) + jnp.einsum('gcp,dgp->dgcp', c_im, pw_re[:L])
    kern = jnp.einsum('dgop,gpi->dgoi', ca_re, bb_re) - jnp.einsum('dgop,gpi->dgoi', ca_im, bb_im)
    lag = np.arange(L)[None, :] - np.arange(L)[:, None]
    lag_or_zero_block = np.where(lag >= 0, lag, L)
    e_re, e_im = pw_re[:L][::-1], pw_im[:L][::-1]
    ws_re = jnp.einsum('sgp,gpi->sgip', e_re, bb_re) - jnp.einsum('sgp,gpi->sgip', e_im, bb_im)
    ws_im = jnp.einsum('sgp,gpi->sgip', e_re, bb_im) + jnp.einsum('sgp,gpi->sgip', e_im, bb_re)
    o_re, o_im = pw_re[1:L + 1], pw_im[1:L + 1]
    wo_re = jnp.einsum('gop,tgp->tgop', c_re, o_re) - jnp.einsum('gop,tgp->tgop', c_im, o_im)
    wo_im = -(jnp.einsum('gop,tgp->tgop', c_re, o_im) + jnp.einsum('gop,tgp->tgop', c_im, o_re))

    def group_diag(a, rows_per_group, cols_per_group):
        tiled = jnp.tile(a, (1, gh))
        rg = (lax.broadcasted_iota(jnp.int32, tiled.shape, 0) // rows_per_group) % gh
        cg = lax.broadcasted_iota(jnp.int32, tiled.shape, 1) // cols_per_group
        return jnp.where(rg == cg, tiled, 0.0)

    mi_h, ws_h, wo_h, al_h = [], [], [], []
    for hf in range(SSM_HALVES):
        gs = slice(hf * gh, (hf + 1) * gh)
        k_gi_o = kern[:, gs].transpose(0, 1, 3, 2).reshape(L * gh * C, C)
        blocks = group_diag(k_gi_o, C, C).reshape(L, gh * C, gh * C)
        blocks = blocks.at[0].add(jnp.diag(d_skip[hf * gh * C:(hf + 1) * gh * C]))
        blocks = jnp.concatenate([blocks, jnp.zeros((1, gh * C, gh * C), F32)], axis=0)
        mi_h.append(blocks[lag_or_zero_block].transpose(0, 2, 1, 3).reshape(L * gh * C, L * gh * C))
        wsr = group_diag(ws_re[:, gs].reshape(L * gh * C, P), C, P)
        wsi = group_diag(ws_im[:, gs].reshape(L * gh * C, P), C, P)
        ws_h.append(jnp.concatenate([wsr, wsi], axis=1))
        wor = group_diag(wo_re[:, gs].reshape(L * gh * C, P), C, P).T
        woi = group_diag(wo_im[:, gs].reshape(L * gh * C, P), C, P).T
        wo_h.append(jnp.concatenate([wor, woi], axis=0))
        al_h.append(jnp.stack([pw_re[L, gs].reshape(gh * P), pw_im[L, gs].reshape(gh * P)]))
    return (jnp.stack(mi_h).astype(BF16), jnp.stack(ws_h).astype(BF16), jnp.stack(wo_h).astype(BF16),
            jnp.stack(al_h))


def _tail_kernel(x_ref, oa_ref, y_ref, szs_ref, gain_ref, scale_ref, shift_ref, gate_ref, wm_ref, wglu_ref,
                 wpa_ref, wps_ref, wout_ref, fg_ref, o_ref):
    d = x_ref.shape[1]
    half = x_ref.shape[0] // 2
    halves = (slice(0, half), slice(half, 2 * half))
    staged = {}

    def merge_gates(r):
        x = x_ref[r, :]
        hb = _modulated_norm(x, gain_ref[...], scale_ref[0], shift_ref[0]).astype(BF16)
        staged[r.start] = (x, _sigmoid(_dot(hb, wm_ref[...])))

    def merge_and_project(r):
        x, mg = staged[r.start]
        y = jnp.concatenate([y_ref[0, r, :], y_ref[1, r, :]], axis=-1)
        yy = _dot(_gelu(y).astype(BF16), wglu_ref[...])
        o_ssm = yy[:, :SSM_WIDTH] * _sigmoid(yy[:, SSM_WIDTH:]) * szs_ref[r, :].astype(F32)
        merged = (mg[:, :d] * _dot(oa_ref[r, :], wpa_ref[...])
                  + mg[:, d:] * _dot(o_ssm.astype(BF16), wps_ref[...]))
        xo = x + gate_ref[0] * _dot(merged.astype(BF16), wout_ref[...])
        ms = jnp.mean(xo * xo, axis=-1, keepdims=True)
        o_ref[r, :] = xo * lax.rsqrt(ms + NORM_EPS) * fg_ref[...]

    merge_gates(halves[0])
    merge_gates(halves[1])
    merge_and_project(halves[0])
    merge_and_project(halves[1])


def _tail(x2, oa, y3, szs, gain, scale, shift, gate, wm, wglu, wpa, wps, wout, fgain, seq):
    t, d = x2.shape
    tm = PROJ_ROWS
    per_seq = seq // tm
    row = lambda n: pl.BlockSpec((tm, n), lambda i: (i, 0))
    full = lambda a: pl.BlockSpec(a.shape, lambda i: (0,) * a.ndim)
    mod = pl.BlockSpec((1, 1, d), lambda i: (i // per_seq, 0, 0))
    return pl.pallas_call(
        _tail_kernel,
        grid=(t // tm,),
        in_specs=[row(d), row(ATTN_WIDTH), pl.BlockSpec((SSM_HALVES, tm, LANES), lambda i: (0, i, 0)),
                  row(SSM_WIDTH), full(gain), mod, mod, mod,
                  full(wm), full(wglu), full(wpa), full(wps), full(wout), full(fgain)],
        out_specs=row(d),
        out_shape=jax.ShapeDtypeStruct((t, d), F32),
        compiler_params=pltpu.CompilerParams(vmem_limit_bytes=VMEM_LIMIT),
        name="tail",
    )(x2, oa, y3, szs, gain, scale, shift, gate, wm, wglu, wpa, wps, wout, fgain)


def _rope_tables(seq):
    half = HEAD_DIM // 2
    inv_freq = ROPE_THETA ** (-jnp.arange(half, dtype=F32) / half)
    ang = jnp.arange(seq, dtype=F32)[:, None] * inv_freq[None, :]
    cos, sin = jnp.cos(ang), jnp.sin(ang)
    zero = jnp.zeros_like(sin)
    reps = LANES // HEAD_DIM
    cos_t = jnp.tile(jnp.concatenate([cos, cos], axis=1), (1, reps))
    sa_t = jnp.tile(jnp.concatenate([-sin, zero], axis=1), (1, reps))
    sb_t = jnp.tile(jnp.concatenate([zero, sin], axis=1), (1, reps))
    return cos_t, sa_t, sb_t


def _heads_to_lanes(w, axis):
    shape = w.shape
    split = shape[:axis] + (N_GROUPS, HEADS_PER_GROUP, HEAD_DIM) + shape[axis + 1:]
    return jnp.swapaxes(w.reshape(split), axis, axis + 1).reshape(shape)


def _proj_weight(w_in):
    d = w_in.shape[0]
    o_q, o_kv, o_g, o_z, o_u, o_zs, o_m = 0, 512, 1280, 1304, 1816, 2072, 2328
    kv = lambda j: w_in[:, o_kv + j * KV_WIDTH: o_kv + (j + 1) * KV_WIDTH]
    gates = jnp.pad(w_in[:, o_g:o_z], ((0, 0), (0, LANES - 3 * N_HEADS)))
    cols = [_heads_to_lanes(w_in[:, o_q:o_kv], 1) * (HEAD_DIM ** -0.5),
            kv(0), kv(2), kv(4), kv(1), kv(3), kv(5), gates,
            _heads_to_lanes(w_in[:, o_z:o_u], 1), w_in[:, o_u:o_zs], w_in[:, o_zs:o_m]]
    w = jnp.concatenate(cols, axis=1)
    assert w.shape == (d, _C_END)
    return w.astype(BF16), w_in[:, o_m:].astype(BF16)


def _attn_constants(seq):
    n_cmp = LANES
    cs = np.arange(n_cmp) * CMP_STRIDE
    ss = np.arange(LANES) * SEL_BLOCK
    ovl = (np.minimum(cs[:, None] + CMP_BLOCK, ss[None, :] + SEL_BLOCK) > np.maximum(cs[:, None], ss[None, :]))
    ovl = ovl & (np.arange(LANES)[None, :] < seq // SEL_BLOCK) & (cs[:, None] + CMP_BLOCK <= seq)
    tq, tk = ATTN_Q, ATTN_K
    v = np.arange(tk // tq)[:, None, None]
    r = np.arange(tq)[None, :, None]
    cc = np.arange(tk)[None, None, :]
    dbias = np.where(cc <= v * tq + r, 0.0, MASK_VALUE).astype(np.float32)
    eg = np.zeros((LANES, 3 * ATTN_WIDTH), np.float32)
    for g in range(N_GROUPS):
        for h in range(HEADS_PER_GROUP):
            for j in range(3):
                base = j * ATTN_WIDTH + h * LANES + g * HEAD_DIM
                eg[(g * HEADS_PER_GROUP + h) * 3 + j, base:base + HEAD_DIM] = 1.0
    span = WINDOW + tq
    q0 = (np.arange(WINDOW // tq + 1) * tq)[:, None, None]
    kp = np.maximum(q0 - WINDOW, 0) + np.arange(span)[None, None, :]
    tt = q0 + np.arange(tq)[None, :, None]
    wbias = np.where((kp <= tt) & (kp > tt - WINDOW), 0.0, MASK_VALUE).astype(np.float32)
    return (jnp.asarray(ovl.T, BF16), jnp.asarray(eg, BF16), jnp.asarray(wbias, F32), jnp.asarray(dbias, F32))


def kernel(x, c, w_ada, b_ada, norm_gain, w_in, pe_cmp_k, w_cmp_k1, b_cmp_k1, w_cmp_k2, pe_cmp_v, w_cmp_v1,
           b_cmp_v1, w_cmp_v2, lam_re, lam_im, log_dt, b_re, b_im, c_re, c_im, d_skip, w_glu, w_proj_attn,
           w_proj_ssm, w_out, final_gain):
    bsz, seq, d = x.shape
    depth = w_in.shape[0]
    assert depth == 1, "the tail kernel fuses the final norm into the (single) layer"
    assert bsz % SSM_BT == 0 and seq % SSM_CHUNK == 0
    assert seq % ATTN_K == 0 and seq % SELECT_Q == 0 and seq % PROJ_ROWS == 0 and seq >= WINDOW + ATTN_Q
    assert (seq - CMP_BLOCK) // CMP_STRIDE + 1 <= LANES and seq // CMP_STRIDE == LANES
    t = bsz * seq
    cos_t, sa_t, sb_t = _rope_tables(seq)
    ovlt, eg, wbias, dbias = _attn_constants(seq)
    n_chunks = seq // SSM_CHUNK

    x2 = x.reshape(t, d)
    for l in range(depth):
        mod = _ada(c, w_ada[l], b_ada[l])
        shift, mscale, gate = [m.reshape(bsz, 1, d) for m in jnp.split(mod, 3, axis=-1)]
        gain = norm_gain[l].reshape(1, d)
        w_proj, w_merge = _proj_weight(w_in[l])
        (q, kc_r, ks0, ks1, kw, vc_r, vs0, vs1, vw0, vw1, g3, sz, u, szs) = _proj(
            x2, gain, mscale, shift, cos_t, sa_t, sb_t, w_proj, seq)

        nch = seq // CMP_STRIDE
        kc, vc = _compress(
            kc_r.reshape(bsz, nch, CMP_STRIDE * LANES), vc_r.reshape(bsz, nch, CMP_STRIDE * LANES),
            _compress_params(pe_cmp_k[l], w_cmp_k1[l], b_cmp_k1[l], w_cmp_k2[l]),
            _compress_params(pe_cmp_v[l], w_cmp_v1[l], b_cmp_v1[l], w_cmp_v2[l]))
        maskq, ocmp, gates = _select(q, kc, vc, g3, ovlt, eg, bsz, seq)
        o_attn = _attn(q, maskq, ocmp, ks0, ks1, vs0, vs1, kw, vw0, vw1, gates, sz, wbias, dbias, bsz, seq)

        mi, ws, wo, al = _ssm_params(lam_re[l], lam_im[l], log_dt[l], b_re[l], b_im[l], c_re[l], c_im[l],
                                     d_skip[l])
        y3 = _ssm(u, mi, ws, wo, al, n_chunks)

        x2 = _tail(x2, o_attn, y3, szs, gain, mscale, shift, gate, w_merge, w_glu[l].astype(BF16),
                   _heads_to_lanes(w_proj_attn[l], 0).astype(BF16), w_proj_ssm[l].astype(BF16),
                   w_out[l].astype(BF16),
                   final_gain.reshape(1, d) if l == depth - 1 else jnp.ones((1, d), F32), seq)
    return x2.reshape(bsz, seq, d)
```

```python
import functools
import math

import jax
import jax.numpy as jnp
import numpy as np
from jax import lax
from jax.experimental import pallas as pl
from jax.experimental.pallas import tpu as pltpu

F32 = jnp.float32
BF16 = jnp.bfloat16

N_HEADS = 8
N_GROUPS = 2
HEADS_PER_GROUP = N_HEADS // N_GROUPS
HEAD_DIM = 64
ATTN_WIDTH = N_HEADS * HEAD_DIM
KV_WIDTH = N_GROUPS * HEAD_DIM
CMP_BLOCK = 32
CMP_STRIDE = 16
CMP_HIDDEN = 2 * HEAD_DIM
SEL_BLOCK = 64
N_SEL = 16
WINDOW = 512
ROPE_THETA = 10000.0
FORCE_SCORE = 1.0e4
MASK_VALUE = -1.0e30
SSM_WIDTH = 256
SSM_GROUP = 16
SSM_GROUPS = SSM_WIDTH // SSM_GROUP
SSM_STATE = 64
NORM_EPS = 1e-6

LANES = 128
SUBLANES = 8
VMEM_LIMIT = 56 * 1024 * 1024

PROJ_ROWS = 1024
ATTN_Q = 256
ATTN_K = 512
SELECT_Q = 2048
SELECT_SUB = 128
SSM_CHUNK = 8
SSM_HALVES = 2
SSM_BT = 8
SSM_UNROLL = 4


def _dot(a, b):
    return jnp.dot(a, b, preferred_element_type=F32)


def _dot_nt(a, b):
    return lax.dot_general(a, b, (((1,), (1,)), ((), ())), preferred_element_type=F32)


def _dot_hilo(a, b_bf16):
    hi = a.astype(BF16)
    lo = (a - hi.astype(F32)).astype(BF16)
    return _dot(hi, b_bf16) + _dot(lo, b_bf16)


def _sigmoid(x):
    return 1.0 / (1.0 + jnp.exp(-x))


def _silu(x):
    return x * _sigmoid(x)


def _gelu(x):
    return 0.5 * x * (1.0 + jnp.tanh(math.sqrt(2.0 / math.pi) * (x + 0.044715 * (x * x * x))))


def _ada_kernel(c_ref, w_ref, b_ref, o_ref):
    c = c_ref[...]
    o_ref[...] = _dot(_silu(c).astype(BF16), w_ref[...].astype(BF16)) + b_ref[...]


def _ada(c, w, b):
    bsz, d = c.shape
    n = w.shape[1]
    return pl.pallas_call(
        _ada_kernel,
        grid=(n // d,),
        in_specs=[pl.BlockSpec((bsz, d), lambda j: (0, 0)),
                  pl.BlockSpec((d, d), lambda j: (0, j)),
                  pl.BlockSpec((1, d), lambda j: (0, j))],
        out_specs=pl.BlockSpec((bsz, d), lambda j: (0, j)),
        out_shape=jax.ShapeDtypeStruct((bsz, n), F32),
        compiler_params=pltpu.CompilerParams(vmem_limit_bytes=VMEM_LIMIT),
        name="ada",
    )(c, w, b.reshape(1, n))


_C_Q = 0
_C_KC = 512
_C_KS = 640
_C_KW = 768
_C_VC = 896
_C_VS = 1024
_C_VW = 1152
_C_G3 = 1280
_C_Z = 1408
_C_U = 1920
_C_ZS = 2176
_C_END = 2432


def _modulated_norm(x, gain, scale, shift):
    ms = jnp.mean(x * x, axis=-1, keepdims=True)
    return x * lax.rsqrt(ms + NORM_EPS) * gain * (1.0 + scale) + shift


def _proj_kernel(x_ref, gain_ref, scale_ref, shift_ref, cos_ref, sa_ref, sb_ref, w_ref,
                 q_ref, kc_ref, ks0_ref, ks1_ref, kw_ref, vc_ref, vs0_ref, vs1_ref, vw0_ref, vw1_ref,
                 g3_ref, sz_ref, u_ref, szs_ref, stage_ref, *, per_seq):
    x = x_ref[...]
    hb = _modulated_norm(x, gain_ref[...], scale_ref[0], shift_ref[0]).astype(BF16)

    def proj(a, b):
        return _dot(hb, w_ref[:, a:b])

    cos = cos_ref[...]
    sa = sa_ref[...]
    sb = sb_ref[...]

    def rope(t):
        return t * cos + pltpu.roll(t, LANES - 32, 1) * sa + pltpu.roll(t, 32, 1) * sb

    narrow = proj(_C_Q, _C_Z)
    part = lambda a: narrow[:, a:a + LANES]
    for v in range(ATTN_WIDTH // LANES):
        q_ref[:, v * LANES:(v + 1) * LANES] = rope(part(_C_Q + v * LANES)).astype(BF16)
    kw_ref[...] = rope(part(_C_KW)).astype(BF16)
    tm = x.shape[0]
    for out_ref, val in ((kc_ref, rope(part(_C_KC))), (vc_ref, part(_C_VC))):
        stage_ref[...] = val
        for j in range(CMP_STRIDE):
            out_ref[:, j * LANES:(j + 1) * LANES] = stage_ref[pl.ds(j, tm // CMP_STRIDE, stride=CMP_STRIDE), :]
    lane = lax.broadcasted_iota(jnp.int32, (tm, LANES), 1)
    pos = (pl.program_id(0) % per_seq) * tm + lax.broadcasted_iota(jnp.int32, (tm, LANES), 0)
    blk = pos >> int(math.log2(SEL_BLOCK))
    first = lane < HEAD_DIM
    ks = rope(part(_C_KS))
    ks0_ref[...] = jnp.where(first, ks, jnp.where(lane - HEAD_DIM == blk, 1.0, 0.0)).astype(BF16)
    ks1_ref[...] = jnp.where(first, jnp.where(lane == blk, 1.0, 0.0), ks).astype(BF16)
    ones = jnp.ones((tm, LANES), BF16)
    zero = jnp.zeros((tm, LANES), F32)
    for v_val, refs in ((part(_C_VS), (vs0_ref, vs1_ref)), (part(_C_VW), (vw0_ref, vw1_ref))):
        refs[0][:, :LANES] = jnp.where(first, v_val, zero).astype(BF16)
        refs[1][:, :LANES] = jnp.where(first, zero, v_val).astype(BF16)
        refs[0][:, LANES:] = ones
        refs[1][:, LANES:] = ones
    g3_ref[...] = _sigmoid(part(_C_G3))
    sz_ref[...] = _silu(proj(_C_Z, _C_U)).astype(BF16)
    u = proj(_C_U, _C_ZS)
    for hf in range(SSM_HALVES):
        stage_ref[...] = u[:, hf * LANES:(hf + 1) * LANES]
        for j in range(SSM_CHUNK):
            u_ref[hf, :, j * LANES:(j + 1) * LANES] = stage_ref[
                pl.ds(j, tm // SSM_CHUNK, stride=SSM_CHUNK), :].astype(BF16)
    szs_ref[...] = _silu(proj(_C_ZS, _C_END)).astype(BF16)


def _proj(x2, gain, scale, shift, cos_t, sa_t, sb_t, w, seq):
    t, d = x2.shape
    tm = PROJ_ROWS
    per_seq = seq // tm
    row = lambda n: pl.BlockSpec((tm, n), lambda i: (i, 0))
    mod = pl.BlockSpec((1, 1, d), lambda i: (i // per_seq, 0, 0))
    tab = pl.BlockSpec((tm, LANES), lambda i: (i % per_seq, 0))
    cmp_w, ssm_w = CMP_STRIDE * LANES, SSM_CHUNK * LANES
    out_shape = [
        jax.ShapeDtypeStruct((t, ATTN_WIDTH), BF16),
        jax.ShapeDtypeStruct((t // CMP_STRIDE, cmp_w), F32),
        jax.ShapeDtypeStruct((t, LANES), BF16),
        jax.ShapeDtypeStruct((t, LANES), BF16),
        jax.ShapeDtypeStruct((t, LANES), BF16),
        jax.ShapeDtypeStruct((t // CMP_STRIDE, cmp_w), F32),
        jax.ShapeDtypeStruct((t, 2 * LANES), BF16),
        jax.ShapeDtypeStruct((t, 2 * LANES), BF16),
        jax.ShapeDtypeStruct((t, 2 * LANES), BF16),
        jax.ShapeDtypeStruct((t, 2 * LANES), BF16),
        jax.ShapeDtypeStruct((t, LANES), F32),
        jax.ShapeDtypeStruct((t, ATTN_WIDTH), BF16),
        jax.ShapeDtypeStruct((SSM_HALVES, t // SSM_CHUNK, ssm_w), BF16),
        jax.ShapeDtypeStruct((t, SSM_WIDTH), BF16),
    ]
    cmp_rows = pl.BlockSpec((tm // CMP_STRIDE, cmp_w), lambda i: (i, 0))
    out_specs = [row(ATTN_WIDTH), cmp_rows, row(LANES), row(LANES), row(LANES), cmp_rows] + [row(2 * LANES)] * 4 + [row(LANES)] + [
                 row(ATTN_WIDTH),
                 pl.BlockSpec((SSM_HALVES, tm // SSM_CHUNK, ssm_w), lambda i: (0, i, 0)),
                 row(SSM_WIDTH)]
    return pl.pallas_call(
        functools.partial(_proj_kernel, per_seq=per_seq),
        grid=(t // tm,),
        in_specs=[row(d), pl.BlockSpec((1, d), lambda i: (0, 0)), mod, mod, tab, tab, tab,
                  pl.BlockSpec((d, _C_END), lambda i: (0, 0))],
        out_specs=out_specs,
        out_shape=out_shape,
        scratch_shapes=[pltpu.VMEM((tm, LANES), F32)],
        compiler_params=pltpu.CompilerParams(vmem_limit_bytes=VMEM_LIMIT),
        name="proj",
    )(x2, gain, scale, shift, cos_t, sa_t, sb_t, w)


def _compress_blocks(r_ref, pea, peb, wa, wb, b1, w2):
    r = r_ref[0]
    first = _dot((r + pea[...]).astype(BF16), wa[...])
    second = _dot((r + peb[...]).astype(BF16), wb[...])
    nxt = pltpu.roll(second, second.shape[0] - 1, 0)
    hid = _gelu(first + nxt + b1[...])
    return _dot(hid.astype(BF16), w2[...])


def _compress_params(pe, w1, b1, w2):
    half = CMP_BLOCK // 2
    eye = jnp.eye(N_GROUPS, dtype=F32)
    w1r = w1.reshape(CMP_BLOCK, HEAD_DIM, CMP_HIDDEN)

    def expand(wl):
        return jnp.einsum('ldj,gh->lgdhj', wl, eye).reshape(half * KV_WIDTH, N_GROUPS * CMP_HIDDEN)

    def pe_lanes(p):
        return jnp.broadcast_to(p[:, None, :], (half, N_GROUPS, HEAD_DIM)).reshape(1, half * KV_WIDTH)

    w2e = jnp.einsum('jd,gh->gjhd', w2, eye).reshape(N_GROUPS * CMP_HIDDEN, KV_WIDTH)
    return (pe_lanes(pe[:half]), pe_lanes(pe[half:]),
            expand(w1r[:half]).astype(BF16), expand(w1r[half:]).astype(BF16),
            jnp.tile(b1, N_GROUPS).reshape(1, N_GROUPS * CMP_HIDDEN), w2e.astype(BF16))


def _select_kernel(q_ref, rk_ref, rv_ref, pea_k, peb_k, wa_k, wb_k, b1_k, w2_k, pea_v, peb_v, wa_v, wb_v, b1_v,
                   w2_v, g3_ref, ovlt_ref, egc_ref, mask_ref, ocmp_ref, gate_ref, *, seq):
    ts = SELECT_SUB
    hg = HEADS_PER_GROUP
    rows = hg * ts
    n_sel_blocks = seq // SEL_BLOCK
    sel_shift = int(math.log2(SEL_BLOCK))
    base = pl.program_id(1) * SELECT_Q
    lane = lax.broadcasted_iota(jnp.int32, (ts, LANES), 1)
    row = lax.broadcasted_iota(jnp.int32, (ts, LANES), 0)
    jidx = lax.broadcasted_iota(jnp.int32, (n_sel_blocks, ts), 0)
    qcol = lax.broadcasted_iota(jnp.int32, (n_sel_blocks, ts), 1)
    jloc = lax.broadcasted_iota(jnp.int32, (SUBLANES, ts), 0)
    groups = range(N_GROUPS)
    in_group = [(lane >= HEAD_DIM * g) & (lane < HEAD_DIM * (g + 1)) for g in groups]
    first_half = lane < HEAD_DIM
    zero = jnp.zeros((ts, LANES), BF16)
    kcb = _compress_blocks(rk_ref, pea_k, peb_k, wa_k, wb_k, b1_k, w2_k).astype(BF16)
    vcb = _compress_blocks(rv_ref, pea_v, peb_v, wa_v, wb_v, b1_v, w2_v).astype(BF16)
    subs = range(SELECT_Q // ts)
    units = [(s, g) for s in subs for g in groups]
    sc, p_cmp, o_cmp, imp, maskq = {}, {}, {}, {}, {}

    def scores(u):
        s, g = u
        qh = [q_ref[s * ts:(s + 1) * ts, h * LANES:(h + 1) * LANES] for h in range(hg)]
        qg = jnp.concatenate([jnp.where(in_group[g], x, zero) for x in qh], axis=0)
        sc[u] = _dot_nt(qg, kcb).reshape(hg, ts, LANES)

    def softmax(u):
        s, g = u
        cvalid = (lane * CMP_STRIDE + (CMP_BLOCK - 1)) <= base + s * ts + row
        sv = jnp.where(cvalid[None], sc[u], MASK_VALUE)
        m = jnp.max(sv, axis=-1, keepdims=True)
        e = jnp.where(cvalid[None], jnp.exp(sv - m), 0.0)
        den = jnp.sum(e, axis=-1, keepdims=True)
        p_cmp[u] = e * (1.0 / jnp.where(den > 0.0, den, 1.0))

    def outputs(u):
        o_cmp[u] = _dot(p_cmp[u].reshape(rows, LANES).astype(BF16), vcb)
        psum = p_cmp[u][0]
        for h in range(1, hg):
            psum = psum + p_cmp[u][h]
        p_hi = psum.astype(BF16)
        p_lo = (psum - p_hi.astype(F32)).astype(BF16)
        imp[u] = (_dot_nt(ovlt_ref[...], p_hi) + _dot_nt(ovlt_ref[...], p_lo))[0:n_sel_blocks]

    def rank(u):
        s, g = u
        t_lane = base + s * ts + qcol
        forced = (jidx == 0) | (jidx == (t_lane >> sel_shift))
        future = jidx * SEL_BLOCK > t_lane
        imp_g = jnp.where(forced, FORCE_SCORE, jnp.where(future, -1.0, imp[u]))
        tiles = [imp_g[k * SUBLANES:(k + 1) * SUBLANES] for k in range(n_sel_blocks // SUBLANES)]
        cnts = [jnp.zeros((SUBLANES, ts), jnp.int32) for _ in tiles]
        for i in range(n_sel_blocks):
            other = jnp.broadcast_to(imp_g[i:i + 1, :], (SUBLANES, ts))
            ki, ri = divmod(i, SUBLANES)
            for k, tile_k in enumerate(tiles):
                if k < ki:
                    ahead = other > tile_k
                elif k > ki:
                    ahead = other >= tile_k
                else:
                    ahead = (other > tile_k) | ((other == tile_k) & (jloc > ri))
                cnts[k] = cnts[k] + jnp.where(ahead, 1, 0)
        cnt = jnp.concatenate(cnts, axis=0)
        mask_t = jnp.where(cnt < min(N_SEL, n_sel_blocks), 0.0, MASK_VALUE)
        lo_rows = HEAD_DIM * (1 - g)
        parts = [mask_t, jnp.zeros((LANES - n_sel_blocks - lo_rows, ts), F32)]
        if lo_rows:
            parts = [jnp.zeros((lo_rows, ts), F32)] + parts
        maskq[u] = jnp.concatenate(parts, axis=0).T

    def store(s):
        rs = slice(s * ts, (s + 1) * ts)
        mask_ref[rs, :] = (maskq[s, 0] + maskq[s, 1]).astype(BF16)
        gexp = _dot_hilo(g3_ref[rs, :], egc_ref[...])
        gate_ref[rs, :] = gexp[:, ATTN_WIDTH:]
        for h in range(hg):
            hr = slice(h * ts, (h + 1) * ts)
            both = jnp.where(first_half, o_cmp[s, 0][hr], o_cmp[s, 1][hr])
            ocmp_ref[rs, h * LANES:(h + 1) * LANES] = gexp[:, h * LANES:(h + 1) * LANES] * both

    stages = (scores, softmax, outputs, rank)
    for step in range(len(units) + len(stages) - 1):
        for k, stage in enumerate(stages):
            if 0 <= step - k < len(units):
                s, g = units[step - k]
                stage((s, g))
                if stage is rank and g == N_GROUPS - 1:
                    store(s)


def _select(q, rk, rv, pk, pv, g3, ovlt, egc, bsz, seq):
    tsel = SELECT_Q
    nq = seq // tsel
    assert nq == 1, "the compression MLP runs once per select step, i.e. once per sequence"
    qrow = lambda n: pl.BlockSpec((tsel, n), lambda b, i: (b * nq + i, 0))
    cmp_spec = pl.BlockSpec((1,) + rk.shape[1:], lambda b, i: (b, 0, 0))
    full = lambda a: pl.BlockSpec(a.shape, lambda b, i: (0,) * a.ndim)
    return pl.pallas_call(
        functools.partial(_select_kernel, seq=seq),
        grid=(bsz, nq),
        in_specs=[qrow(ATTN_WIDTH), cmp_spec, cmp_spec] + [full(a) for a in pk] + [full(a) for a in pv]
                 + [qrow(LANES), full(ovlt), full(egc)],
        out_specs=[qrow(LANES), qrow(ATTN_WIDTH), qrow(2 * ATTN_WIDTH)],
        out_shape=[jax.ShapeDtypeStruct((bsz * seq, LANES), BF16),
                   jax.ShapeDtypeStruct((bsz * seq, ATTN_WIDTH), F32),
                   jax.ShapeDtypeStruct((bsz * seq, 2 * ATTN_WIDTH), F32)],
        compiler_params=pltpu.CompilerParams(vmem_limit_bytes=VMEM_LIMIT),
        name="select",
    )(q, rk, rv, *pk, *pv, g3, ovlt, egc)


def _attn_kernel(q_ref, mask_ref, ocmp_ref, ks0_ref, ks1_ref, vs0_ref, vs1_ref, kw_ref, vw0_ref, vw1_ref,
                 gate_ref, sz_ref, wbias_ref, dbias_ref, o_ref, slc_ref, win_ref, *, seq):
    tq, tk = ATTN_Q, ATTN_K
    hg = HEADS_PER_GROUP
    rows = hg * tq
    q0 = pl.program_id(1) * tq
    lane = lax.broadcasted_iota(jnp.int32, (tq, LANES), 1)

    span = WINDOW + tq
    start = pl.multiple_of(jnp.maximum(q0 - WINDOW, 0), tq)
    wbias = wbias_ref[0]
    width = ATTN_WIDTH
    ks_refs, vs_refs, vw_refs = (ks0_ref, ks1_ref), (vs0_ref, vs1_ref), (vw0_ref, vw1_ref)

    groups = range(N_GROUPS)
    in_group = [(lane >= HEAD_DIM * g) & (lane < HEAD_DIM * (g + 1)) for g in groups]
    zero = jnp.zeros((tq, LANES), BF16)
    q_heads = [q_ref[:, h * LANES:(h + 1) * LANES] for h in range(hg)]
    qg = [jnp.concatenate([jnp.where(in_group[g], qh, zero) for qh in q_heads], axis=0) for g in groups]
    maskq = mask_ref[...]
    qsel = [jnp.concatenate([jnp.where(in_group[g], qh, maskq) for qh in q_heads], axis=0) for g in groups]

    n_tiles = q0 // tk + 1
    for n in range(1, seq // tk + 1):
        @pl.when(n_tiles == n)
        def _(n=n):
            klen = n * tk
            s2 = [_dot_nt(qsel[g], ks_refs[g][0:klen, :]) for g in groups]
            s3 = [_dot_nt(qg[g], kw_ref[pl.ds(start, span), :]).reshape(hg, tq, span) for g in groups]
            for g in groups:
                last = (s2[g][:, klen - tk:].reshape(hg, tq, tk) + dbias_ref[0][None]).reshape(rows, tk)
                sg = last if n == 1 else jnp.concatenate([s2[g][:, :klen - tk], last], axis=1)
                m2 = jnp.max(sg, axis=-1, keepdims=True)
                p2 = jnp.exp(sg - m2).astype(BF16)
                slc_ref[g] = _dot(p2, vs_refs[g][0:klen, :])
            for g in groups:
                sw = s3[g] + wbias[None]
                m3 = jnp.max(sw, axis=-1, keepdims=True)
                p3 = jnp.exp(sw - m3).astype(BF16).reshape(rows, span)
                win_ref[g] = _dot(p3, vw_refs[g][pl.ds(start, span), :])

    o_slc, o_win = [], []
    for g in groups:
        o_slc.append(slc_ref[g, :, :LANES] * (1.0 / slc_ref[g, :, LANES:]))
        o_win.append(win_ref[g, :, :LANES] * (1.0 / win_ref[g, :, LANES:]))

    first_half = lane < HEAD_DIM
    for h in range(hg):
        rws = slice(h * tq, (h + 1) * tq)
        cols = slice(h * LANES, (h + 1) * LANES)
        acc = ocmp_ref[:, cols]
        for j, branch in enumerate((o_slc, o_win)):
            both = jnp.where(first_half, branch[0][rws], branch[1][rws])
            acc = acc + gate_ref[:, j * width + h * LANES: j * width + (h + 1) * LANES] * both
        o_ref[:, cols] = (acc * sz_ref[:, cols].astype(F32)).astype(BF16)


def _attn(q, maskq, ocmp, ks0, ks1, vs0, vs1, kw, vw0, vw1, gates, sz, wbias, dbias, bsz, seq):
    tq, tk = ATTN_Q, ATTN_K
    nq = seq // tq
    qrow = lambda n: pl.BlockSpec((tq, n), lambda b, i: (b * nq + i, 0))
    per_seq = pl.BlockSpec((seq, LANES), lambda b, i: (b, 0))
    val_seq = pl.BlockSpec((seq, 2 * LANES), lambda b, i: (b, 0))
    diag = pl.BlockSpec((1, tq, tk), lambda b, i: (i % (tk // tq), 0, 0))
    wspec = pl.BlockSpec((1,) + wbias.shape[1:], lambda b, i: (jnp.minimum(i, wbias.shape[0] - 1), 0, 0))
    return pl.pallas_call(
        functools.partial(_attn_kernel, seq=seq),
        grid=(bsz, nq),
        in_specs=[qrow(ATTN_WIDTH), qrow(LANES), qrow(ATTN_WIDTH), per_seq, per_seq, val_seq, val_seq, per_seq,
                  val_seq, val_seq]
                 + [qrow(2 * ATTN_WIDTH), qrow(ATTN_WIDTH), wspec, diag],
        out_specs=qrow(ATTN_WIDTH),
        out_shape=jax.ShapeDtypeStruct((bsz * seq, ATTN_WIDTH), BF16),
        scratch_shapes=[pltpu.VMEM((N_GROUPS, HEADS_PER_GROUP * tq, 2 * LANES), F32)] * 2,
        compiler_params=pltpu.CompilerParams(vmem_limit_bytes=VMEM_LIMIT),
        name="attn",
    )(q, maskq, ocmp, ks0, ks1, vs0, vs1, kw, vw0, vw1, gates, sz, wbias, dbias)


def _ssm_kernel(u_ref, mi_ref, ws_ref, wo_ref, al_ref, y_ref, sx_ref, *, n_chunks):
    u = u_ref[0]
    n_tiles = sx_ref.shape[0]
    nt = n_tiles // 2
    tile = lambda j: slice(j * LANES, (j + 1) * LANES)
    bt = SSM_BT
    pitch = sx_ref.shape[1] // bt
    seq_rows = lambda b: pl.ds(b * pitch, n_chunks)
    for j in range(n_tiles):
        inj = _dot(u, ws_ref[0, :, tile(j)])
        for b in range(bt):
            sx_ref[j, seq_rows(b), :] = inj[b * n_chunks:(b + 1) * n_chunks]
    a_re = [al_ref[0, 0:1, tile(j)] for j in range(nt)]
    a_im = [al_ref[0, 1:2, tile(j)] for j in range(nt)]

    def body(k, carry):
        rows = pl.ds(k, bt, stride=pitch)
        new = []
        for j in range(nt):
            x_re, x_im = carry[j], carry[nt + j]
            s_re = sx_ref[j, rows, :]
            s_im = sx_ref[nt + j, rows, :]
            sx_ref[j, rows, :] = x_re
            sx_ref[nt + j, rows, :] = x_im
            new.append((a_re[j] * x_re - a_im[j] * x_im + s_re, a_re[j] * x_im + a_im[j] * x_re + s_im))
        return tuple([n[0] for n in new] + [n[1] for n in new])

    zero = jnp.zeros((bt, LANES), F32)
    lax.fori_loop(0, n_chunks, body, (zero,) * n_tiles, unroll=SSM_UNROLL)
    width = u.shape[1]
    ctile = 2 * LANES
    y_cols = [_dot(u[:, :(c + 1) * ctile], mi_ref[0, :(c + 1) * ctile, c * ctile:(c + 1) * ctile])
              for c in range(width // ctile)]
    xs = jnp.concatenate(
        [jnp.concatenate([sx_ref[j, seq_rows(b), :] for b in range(bt)], axis=0).astype(BF16)
         for j in range(n_tiles)], axis=1)
    y = jnp.concatenate(y_cols, axis=1) + _dot(xs, wo_ref[0])
    for j in range(SSM_CHUNK):
        y_ref[0, pl.ds(j, y.shape[0], stride=SSM_CHUNK), :] = y[:, tile(j)]


def _ssm(u3, mi, ws, wo, al, n_chunks):
    halves, rows, width = u3.shape
    r = SSM_BT * n_chunks
    wspec = lambda a: pl.BlockSpec((1,) + a.shape[1:], lambda hf, i: (hf, 0, 0))
    return pl.pallas_call(
        functools.partial(_ssm_kernel, n_chunks=n_chunks),
        grid=(halves, rows // r),
        in_specs=[pl.BlockSpec((1, r, width), lambda hf, i: (hf, i, 0)),
                  wspec(mi), wspec(ws), wspec(wo), wspec(al)],
        out_specs=pl.BlockSpec((1, r * SSM_CHUNK, LANES), lambda hf, i: (hf, i, 0)),
        out_shape=jax.ShapeDtypeStruct((halves, rows * SSM_CHUNK, LANES), F32),
        scratch_shapes=[pltpu.VMEM((ws.shape[2] // LANES, SSM_BT * (n_chunks + SUBLANES), LANES), F32)],
        compiler_params=pltpu.CompilerParams(vmem_limit_bytes=VMEM_LIMIT),
        name="ssm",
    )(u3, mi, ws, wo, al)


def _ssm_params(lam_re, lam_im, log_dt, b_re, b_im, c_re, c_im, d_skip):
    L = SSM_CHUNK
    G, P, C = SSM_GROUPS, SSM_STATE, SSM_GROUP
    gh = G // SSM_HALVES
    dt = jnp.exp(log_dt)[:, None]
    lr, li = lam_re, lam_im
    mag = jnp.exp(lr * dt)
    ab_re = mag * jnp.cos(li * dt)
    ab_im = mag * jnp.sin(li * dt)
    nr, ni = ab_re - 1.0, ab_im
    den = lr * lr + li * li
    cr = ((nr * lr + ni * li) / den)[..., None]
    ci = ((ni * lr - nr * li) / den)[..., None]
    bb_re = cr * b_re - ci * b_im
    bb_im = cr * b_im + ci * b_re
    pr, pi = [jnp.ones_like(ab_re)], [jnp.zeros_like(ab_im)]
    for _ in range(L):
        pr.append(pr[-1] * ab_re - pi[-1] * ab_im)
        pi.append(pr[-2] * ab_im + pi[-1] * ab_re)
    pw_re = jnp.stack(pr)
    pw_im = jnp.stack(pi)
    ca_re = jnp.einsum('gcp,dgp->dgcp', c_re, pw_re[:L]) - jnp.einsum('gcp,dgp->dgcp', c_im, pw_im[:L])
    ca_im = jnp.einsum('gcp,dgp->dgcp', c_re, pw_im[:L]) + jnp.einsum('gcp,dgp->dgcp', c_im, pw_re[:L])
    kern = jnp.einsum('dgop,gpi->dgoi', ca_re, bb_re) - jnp.einsum('dgop,gpi->dgoi', ca_im, bb_im)
    lag = np.arange(L)[None, :] - np.arange(L)[:, None]
    lag_or_zero_block = np.where(lag >= 0, lag, L)
    e_re, e_im = pw_re[:L][::-1], pw_im[:L][::-1]
    ws_re = jnp.einsum('sgp,gpi->sgip', e_re, bb_re) - jnp.einsum('sgp,gpi->sgip', e_im, bb_im)
    ws_im = jnp.einsum('sgp,gpi->sgip', e_re, bb_im) + jnp.einsum('sgp,gpi->sgip', e_im, bb_re)
    o_re, o_im = pw_re[1:L + 1], pw_im[1:L + 1]
    wo_re = jnp.einsum('gop,tgp->tgop', c_re, o_re) - jnp.einsum('gop,tgp->tgop', c_im, o_im)
    wo_im = -(jnp.einsum('gop,tgp->tgop', c_re, o_im) + jnp.einsum('gop,tgp->tgop', c_im, o_re))

    def group_diag(a, rows_per_group, cols_per_group):
        tiled = jnp.tile(a, (1, gh))
        rg = (lax.broadcasted_iota(jnp.int32, tiled.shape, 0) // rows_per_group) % gh
        cg = lax.broadcasted_iota(jnp.int32, tiled.shape, 1) // cols_per_group
        return jnp.where(rg == cg, tiled, 0.0)

    mi_h, ws_h, wo_h, al_h = [], [], [], []
    for hf in range(SSM_HALVES):
        gs = slice(hf * gh, (hf + 1) * gh)
        k_gi_o = kern[:, gs].transpose(0, 1, 3, 2).reshape(L * gh * C, C)
        blocks = group_diag(k_gi_o, C, C).reshape(L, gh * C, gh * C)
        blocks = blocks.at[0].add(jnp.diag(d_skip[hf * gh * C:(hf + 1) * gh * C]))
        blocks = jnp.concatenate([blocks, jnp.zeros((1, gh * C, gh * C), F32)], axis=0)
        mi_h.append(blocks[lag_or_zero_block].transpose(0, 2, 1, 3).reshape(L * gh * C, L * gh * C))
        wsr = group_diag(ws_re[:, gs].reshape(L * gh * C, P), C, P)
        wsi = group_diag(ws_im[:, gs].reshape(L * gh * C, P), C, P)
        ws_h.append(jnp.concatenate([wsr, wsi], axis=1))
        wor = group_diag(wo_re[:, gs].reshape(L * gh * C, P), C, P).T
        woi = group_diag(wo_im[:, gs].reshape(L * gh * C, P), C, P).T
        wo_h.append(jnp.concatenate([wor, woi], axis=0))
        al_h.append(jnp.stack([pw_re[L, gs].reshape(gh * P), pw_im[L, gs].reshape(gh * P)]))
    return (jnp.stack(mi_h).astype(BF16), jnp.stack(ws_h).astype(BF16), jnp.stack(wo_h).astype(BF16),
            jnp.stack(al_h))


def _tail_kernel(x_ref, oa_ref, y_ref, szs_ref, gain_ref, scale_ref, shift_ref, gate_ref, wm_ref, wglu_ref,
                 wpa_ref, wps_ref, wout_ref, fg_ref, o_ref):
    d = x_ref.shape[1]
    half = x_ref.shape[0] // 2
    halves = (slice(0, half), slice(half, 2 * half))
    staged = {}

    def merge_gates(r):
        x = x_ref[r, :]
        hb = _modulated_norm(x, gain_ref[...], scale_ref[0], shift_ref[0]).astype(BF16)
        staged[r.start] = (x, _sigmoid(_dot(hb, wm_ref[...])))

    def merge_and_project(r):
        x, mg = staged[r.start]
        y = jnp.concatenate([y_ref[0, r, :], y_ref[1, r, :]], axis=-1)
        yy = _dot(_gelu(y).astype(BF16), wglu_ref[...])
        o_ssm = yy[:, :SSM_WIDTH] * _sigmoid(yy[:, SSM_WIDTH:]) * szs_ref[r, :].astype(F32)
        merged = (mg[:, :d] * _dot(oa_ref[r, :], wpa_ref[...])
                  + mg[:, d:] * _dot(o_ssm.astype(BF16), wps_ref[...]))
        xo = x + gate_ref[0] * _dot(merged.astype(BF16), wout_ref[...])
        ms = jnp.mean(xo * xo, axis=-1, keepdims=True)
        o_ref[r, :] = xo * lax.rsqrt(ms + NORM_EPS) * fg_ref[...]

    merge_gates(halves[0])
    merge_gates(halves[1])
    merge_and_project(halves[0])
    merge_and_project(halves[1])


def _tail(x2, oa, y3, szs, gain, scale, shift, gate, wm, wglu, wpa, wps, wout, fgain, seq):
    t, d = x2.shape
    tm = PROJ_ROWS
    per_seq = seq // tm
    row = lambda n: pl.BlockSpec((tm, n), lambda i: (i, 0))
    full = lambda a: pl.BlockSpec(a.shape, lambda i: (0,) * a.ndim)
    mod = pl.BlockSpec((1, 1, d), lambda i: (i // per_seq, 0, 0))
    return pl.pallas_call(
        _tail_kernel,
        grid=(t // tm,),
        in_specs=[row(d), row(ATTN_WIDTH), pl.BlockSpec((SSM_HALVES, tm, LANES), lambda i: (0, i, 0)),
                  row(SSM_WIDTH), full(gain), mod, mod, mod,
                  full(wm), full(wglu), full(wpa), full(wps), full(wout), full(fgain)],
        out_specs=row(d),
        out_shape=jax.ShapeDtypeStruct((t, d), F32),
        compiler_params=pltpu.CompilerParams(vmem_limit_bytes=VMEM_LIMIT),
        name="tail",
    )(x2, oa, y3, szs, gain, scale, shift, gate, wm, wglu, wpa, wps, wout, fgain)


def _rope_tables(seq):
    half = HEAD_DIM // 2
    inv_freq = ROPE_THETA ** (-jnp.arange(half, dtype=F32) / half)
    ang = jnp.arange(seq, dtype=F32)[:, None] * inv_freq[None, :]
    cos, sin = jnp.cos(ang), jnp.sin(ang)
    zero = jnp.zeros_like(sin)
    reps = LANES // HEAD_DIM
    cos_t = jnp.tile(jnp.concatenate([cos, cos], axis=1), (1, reps))
    sa_t = jnp.tile(jnp.concatenate([-sin, zero], axis=1), (1, reps))
    sb_t = jnp.tile(jnp.concatenate([zero, sin], axis=1), (1, reps))
    return cos_t, sa_t, sb_t


def _heads_to_lanes(w, axis):
    shape = w.shape
    split = shape[:axis] + (N_GROUPS, HEADS_PER_GROUP, HEAD_DIM) + shape[axis + 1:]
    return jnp.swapaxes(w.reshape(split), axis, axis + 1).reshape(shape)


def _proj_weight(w_in):
    d = w_in.shape[0]
    o_q, o_kv, o_g, o_z, o_u, o_zs, o_m = 0, 512, 1280, 1304, 1816, 2072, 2328
    kv = lambda j: w_in[:, o_kv + j * KV_WIDTH: o_kv + (j + 1) * KV_WIDTH]
    gates = jnp.pad(w_in[:, o_g:o_z], ((0, 0), (0, LANES - 3 * N_HEADS)))
    cols = [_heads_to_lanes(w_in[:, o_q:o_kv], 1) * (HEAD_DIM ** -0.5),
            kv(0), kv(2), kv(4), kv(1), kv(3), kv(5), gates,
            _heads_to_lanes(w_in[:, o_z:o_u], 1), w_in[:, o_u:o_zs], w_in[:, o_zs:o_m]]
    w = jnp.concatenate(cols, axis=1)
    assert w.shape == (d, _C_END)
    return w.astype(BF16), w_in[:, o_m:].astype(BF16)


def _attn_constants(seq):
    n_cmp = LANES
    cs = np.arange(n_cmp) * CMP_STRIDE
    ss = np.arange(LANES) * SEL_BLOCK
    ovl = (np.minimum(cs[:, None] + CMP_BLOCK, ss[None, :] + SEL_BLOCK) > np.maximum(cs[:, None], ss[None, :]))
    ovl = ovl & (np.arange(LANES)[None, :] < seq // SEL_BLOCK) & (cs[:, None] + CMP_BLOCK <= seq)
    tq, tk = ATTN_Q, ATTN_K
    v = np.arange(tk // tq)[:, None, None]
    r = np.arange(tq)[None, :, None]
    cc = np.arange(tk)[None, None, :]
    dbias = np.where(cc <= v * tq + r, 0.0, MASK_VALUE).astype(np.float32)
    eg = np.zeros((LANES, 3 * ATTN_WIDTH), np.float32)
    for g in range(N_GROUPS):
        for h in range(HEADS_PER_GROUP):
            for j in range(3):
                base = j * ATTN_WIDTH + h * LANES + g * HEAD_DIM
                eg[(g * HEADS_PER_GROUP + h) * 3 + j, base:base + HEAD_DIM] = 1.0
    span = WINDOW + tq
    q0 = (np.arange(WINDOW // tq + 1) * tq)[:, None, None]
    kp = np.maximum(q0 - WINDOW, 0) + np.arange(span)[None, None, :]
    tt = q0 + np.arange(tq)[None, :, None]
    wbias = np.where((kp <= tt) & (kp > tt - WINDOW), 0.0, MASK_VALUE).astype(np.float32)
    return (jnp.asarray(ovl.T, BF16), jnp.asarray(eg, BF16), jnp.asarray(wbias, F32), jnp.asarray(dbias, F32))


def kernel(x, c, w_ada, b_ada, norm_gain, w_in, pe_cmp_k, w_cmp_k1, b_cmp_k1, w_cmp_k2, pe_cmp_v, w_cmp_v1,
           b_cmp_v1, w_cmp_v2, lam_re, lam_im, log_dt, b_re, b_im, c_re, c_im, d_skip, w_glu, w_proj_attn,
           w_proj_ssm, w_out, final_gain):
    bsz, seq, d = x.shape
    depth = w_in.shape[0]
    assert depth == 1, "the tail kernel fuses the final norm into the (single) layer"
    assert bsz % SSM_BT == 0 and seq % SSM_CHUNK == 0
    assert seq % ATTN_K == 0 and seq % SELECT_Q == 0 and seq % PROJ_ROWS == 0 and seq >= WINDOW + ATTN_Q
    assert (seq - CMP_BLOCK) // CMP_STRIDE + 1 <= LANES and seq // CMP_STRIDE == LANES
    t = bsz * seq
    cos_t, sa_t, sb_t = _rope_tables(seq)
    ovlt, eg, wbias, dbias = _attn_constants(seq)
    n_chunks = seq // SSM_CHUNK

    x2 = x.reshape(t, d)
    for l in range(depth):
        mod = _ada(c, w_ada[l], b_ada[l])
        shift, mscale, gate = [m.reshape(bsz, 1, d) for m in jnp.split(mod, 3, axis=-1)]
        gain = norm_gain[l].reshape(1, d)
        w_proj, w_merge = _proj_weight(w_in[l])
        (q, kc_r, ks0, ks1, kw, vc_r, vs0, vs1, vw0, vw1, g3, sz, u, szs) = _proj(
            x2, gain, mscale, shift, cos_t, sa_t, sb_t, w_proj, seq)

        nch = seq // CMP_STRIDE
        maskq, ocmp, gates = _select(
            q, kc_r.reshape(bsz, nch, CMP_STRIDE * LANES), vc_r.reshape(bsz, nch, CMP_STRIDE * LANES),
            _compress_params(pe_cmp_k[l], w_cmp_k1[l], b_cmp_k1[l], w_cmp_k2[l]),
            _compress_params(pe_cmp_v[l], w_cmp_v1[l], b_cmp_v1[l], w_cmp_v2[l]), g3, ovlt, eg, bsz, seq)
        o_attn = _attn(q, maskq, ocmp, ks0, ks1, vs0, vs1, kw, vw0, vw1, gates, sz, wbias, dbias, bsz, seq)

        mi, ws, wo, al = _ssm_params(lam_re[l], lam_im[l], log_dt[l], b_re[l], b_im[l], c_re[l], c_im[l],
                                     d_skip[l])
        y3 = _ssm(u, mi, ws, wo, al, n_chunks)

        x2 = _tail(x2, o_attn, y3, szs, gain, mscale, shift, gate, w_merge, w_glu[l].astype(BF16),
                   _heads_to_lanes(w_proj_attn[l], 0).astype(BF16), w_proj_ssm[l].astype(BF16),
                   w_out[l].astype(BF16),
                   final_gain.reshape(1, d) if l == depth - 1 else jnp.ones((1, d), F32), seq)
    return x2.reshape(bsz, seq, d)
```

```python
import functools
import math

import jax
import jax.numpy as jnp
import numpy as np
from jax import lax
from jax.experimental import pallas as pl
from jax.experimental.pallas import tpu as pltpu

F32 = jnp.float32
BF16 = jnp.bfloat16

N_HEADS = 8
N_GROUPS = 2
HEADS_PER_GROUP = N_HEADS // N_GROUPS
HEAD_DIM = 64
ATTN_WIDTH = N_HEADS * HEAD_DIM
KV_WIDTH = N_GROUPS * HEAD_DIM
CMP_BLOCK = 32
CMP_STRIDE = 16
CMP_HIDDEN = 2 * HEAD_DIM
SEL_BLOCK = 64
N_SEL = 16
WINDOW = 512
ROPE_THETA = 10000.0
FORCE_SCORE = 1.0e4
MASK_VALUE = -1.0e30
SSM_WIDTH = 256
SSM_GROUP = 16
SSM_GROUPS = SSM_WIDTH // SSM_GROUP
SSM_STATE = 64
NORM_EPS = 1e-6

LANES = 128
SUBLANES = 8
VMEM_LIMIT = 56 * 1024 * 1024

PROJ_ROWS = 1024
ATTN_Q = 256
ATTN_K = 512
SELECT_Q = 2048
SELECT_SUB = 128
SSM_CHUNK = 8
SSM_HALVES = 2
SSM_BT = 8
SSM_UNROLL = 4


def _dot(a, b):
    return jnp.dot(a, b, preferred_element_type=F32)


def _dot_nt(a, b):
    return lax.dot_general(a, b, (((1,), (1,)), ((), ())), preferred_element_type=F32)


def _dot_hilo(a, b_bf16):
    hi = a.astype(BF16)
    lo = (a - hi.astype(F32)).astype(BF16)
    return _dot(hi, b_bf16) + _dot(lo, b_bf16)


def _sigmoid(x):
    return 1.0 / (1.0 + jnp.exp(-x))


def _silu(x):
    return x * _sigmoid(x)


def _gelu(x):
    return 0.5 * x * (1.0 + jnp.tanh(math.sqrt(2.0 / math.pi) * (x + 0.044715 * (x * x * x))))


def _ada_kernel(c_ref, w_ref, b_ref, o_ref):
    c = c_ref[...]
    o_ref[...] = _dot(_silu(c).astype(BF16), w_ref[...].astype(BF16)) + b_ref[...]


def _ada(c, w, b):
    bsz, d = c.shape
    n = w.shape[1]
    return pl.pallas_call(
        _ada_kernel,
        grid=(n // d,),
        in_specs=[pl.BlockSpec((bsz, d), lambda j: (0, 0)),
                  pl.BlockSpec((d, d), lambda j: (0, j)),
                  pl.BlockSpec((1, d), lambda j: (0, j))],
        out_specs=pl.BlockSpec((bsz, d), lambda j: (0, j)),
        out_shape=jax.ShapeDtypeStruct((bsz, n), F32),
        compiler_params=pltpu.CompilerParams(vmem_limit_bytes=VMEM_LIMIT),
        name="ada",
    )(c, w, b.reshape(1, n))


_C_Q = 0
_C_KC = 512
_C_KS = 640
_C_KW = 768
_C_VC = 896
_C_VS = 1024
_C_VW = 1152
_C_G3 = 1280
_C_Z = 1408
_C_U = 1920
_C_ZS = 2176
_C_END = 2432


def _modulated_norm(x, gain, scale, shift):
    ms = jnp.mean(x * x, axis=-1, keepdims=True)
    return x * lax.rsqrt(ms + NORM_EPS) * gain * (1.0 + scale) + shift


def _proj_kernel(x_ref, gain_ref, scale_ref, shift_ref, cos_ref, sa_ref, sb_ref, w_ref,
                 q_ref, kc_ref, ks0_ref, ks1_ref, kw_ref, vc_ref, vs0_ref, vs1_ref, vw0_ref, vw1_ref,
                 g3_ref, sz_ref, u_ref, szs_ref, stage_ref, *, per_seq):
    x = x_ref[...]
    hb = _modulated_norm(x, gain_ref[...], scale_ref[0], shift_ref[0]).astype(BF16)

    def proj(a, b):
        return _dot(hb, w_ref[:, a:b])

    cos = cos_ref[...]
    sa = sa_ref[...]
    sb = sb_ref[...]

    def rope(t):
        return t * cos + pltpu.roll(t, LANES - 32, 1) * sa + pltpu.roll(t, 32, 1) * sb

    narrow = proj(_C_Q, _C_Z)
    part = lambda a: narrow[:, a:a + LANES]
    for v in range(ATTN_WIDTH // LANES):
        q_ref[:, v * LANES:(v + 1) * LANES] = rope(part(_C_Q + v * LANES)).astype(BF16)
    kw_ref[...] = rope(part(_C_KW)).astype(BF16)
    tm = x.shape[0]
    for out_ref, val in ((kc_ref, rope(part(_C_KC))), (vc_ref, part(_C_VC))):
        stage_ref[...] = val
        for j in range(CMP_STRIDE):
            out_ref[:, j * LANES:(j + 1) * LANES] = stage_ref[pl.ds(j, tm // CMP_STRIDE, stride=CMP_STRIDE), :]
    lane = lax.broadcasted_iota(jnp.int32, (tm, LANES), 1)
    pos = (pl.program_id(0) % per_seq) * tm + lax.broadcasted_iota(jnp.int32, (tm, LANES), 0)
    blk = pos >> int(math.log2(SEL_BLOCK))
    first = lane < HEAD_DIM
    ks = rope(part(_C_KS))
    ks0_ref[...] = jnp.where(first, ks, jnp.where(lane - HEAD_DIM == blk, 1.0, 0.0)).astype(BF16)
    ks1_ref[...] = jnp.where(first, jnp.where(lane == blk, 1.0, 0.0), ks).astype(BF16)
    ones = jnp.ones((tm, LANES), BF16)
    zero = jnp.zeros((tm, LANES), F32)
    for v_val, refs in ((part(_C_VS), (vs0_ref, vs1_ref)), (part(_C_VW), (vw0_ref, vw1_ref))):
        refs[0][:, :LANES] = jnp.where(first, v_val, zero).astype(BF16)
        refs[1][:, :LANES] = jnp.where(first, zero, v_val).astype(BF16)
        refs[0][:, LANES:] = ones
        refs[1][:, LANES:] = ones
    g3_ref[...] = _sigmoid(part(_C_G3))
    sz_ref[...] = _silu(proj(_C_Z, _C_U)).astype(BF16)
    u = proj(_C_U, _C_ZS)
    for hf in range(SSM_HALVES):
        stage_ref[...] = u[:, hf * LANES:(hf + 1) * LANES]
        for j in range(SSM_CHUNK):
            u_ref[hf, :, j * LANES:(j + 1) * LANES] = stage_ref[
                pl.ds(j, tm // SSM_CHUNK, stride=SSM_CHUNK), :].astype(BF16)
    szs_ref[...] = _silu(proj(_C_ZS, _C_END)).astype(BF16)


def _proj(x2, gain, scale, shift, cos_t, sa_t, sb_t, w, seq):
    t, d = x2.shape
    tm = PROJ_ROWS
    per_seq = seq // tm
    row = lambda n: pl.BlockSpec((tm, n), lambda i: (i, 0))
    mod = pl.BlockSpec((1, 1, d), lambda i: (i // per_seq, 0, 0))
    tab = pl.BlockSpec((tm, LANES), lambda i: (i % per_seq, 0))
    cmp_w, ssm_w = CMP_STRIDE * LANES, SSM_CHUNK * LANES
    out_shape = [
        jax.ShapeDtypeStruct((t, ATTN_WIDTH), BF16),
        jax.ShapeDtypeStruct((t // CMP_STRIDE, cmp_w), F32),
        jax.ShapeDtypeStruct((t, LANES), BF16),
        jax.ShapeDtypeStruct((t, LANES), BF16),
        jax.ShapeDtypeStruct((t, LANES), BF16),
        jax.ShapeDtypeStruct((t // CMP_STRIDE, cmp_w), F32),
        jax.ShapeDtypeStruct((t, 2 * LANES), BF16),
        jax.ShapeDtypeStruct((t, 2 * LANES), BF16),
        jax.ShapeDtypeStruct((t, 2 * LANES), BF16),
        jax.ShapeDtypeStruct((t, 2 * LANES), BF16),
        jax.ShapeDtypeStruct((t, LANES), F32),
        jax.ShapeDtypeStruct((t, ATTN_WIDTH), BF16),
        jax.ShapeDtypeStruct((SSM_HALVES, t // SSM_CHUNK, ssm_w), BF16),
        jax.ShapeDtypeStruct((t, SSM_WIDTH), BF16),
    ]
    cmp_rows = pl.BlockSpec((tm // CMP_STRIDE, cmp_w), lambda i: (i, 0))
    out_specs = [row(ATTN_WIDTH), cmp_rows, row(LANES), row(LANES), row(LANES), cmp_rows] + [row(2 * LANES)] * 4 + [row(LANES)] + [
                 row(ATTN_WIDTH),
                 pl.BlockSpec((SSM_HALVES, tm // SSM_CHUNK, ssm_w), lambda i: (0, i, 0)),
                 row(SSM_WIDTH)]
    return pl.pallas_call(
        functools.partial(_proj_kernel, per_seq=per_seq),
        grid=(t // tm,),
        in_specs=[row(d), pl.BlockSpec((1, d), lambda i: (0, 0)), mod, mod, tab, tab, tab,
                  pl.BlockSpec((d, _C_END), lambda i: (0, 0))],
        out_specs=out_specs,
        out_shape=out_shape,
        scratch_shapes=[pltpu.VMEM((tm, LANES), F32)],
        compiler_params=pltpu.CompilerParams(vmem_limit_bytes=VMEM_LIMIT),
        name="proj",
    )(x2, gain, scale, shift, cos_t, sa_t, sb_t, w)


def _compress_blocks(r_ref, pea, peb, wa, wb, b1, w2):
    r = r_ref[0]
    first = _dot((r + pea[...]).astype(BF16), wa[...])
    second = _dot((r + peb[...]).astype(BF16), wb[...])
    nxt = pltpu.roll(second, second.shape[0] - 1, 0)
    hid = _gelu(first + nxt + b1[...])
    return _dot(hid.astype(BF16), w2[...])


def _compress_params(pe, w1, b1, w2):
    half = CMP_BLOCK // 2
    eye = jnp.eye(N_GROUPS, dtype=F32)
    w1r = w1.reshape(CMP_BLOCK, HEAD_DIM, CMP_HIDDEN)

    def expand(wl):
        return jnp.einsum('ldj,gh->lgdhj', wl, eye).reshape(half * KV_WIDTH, N_GROUPS * CMP_HIDDEN)

    def pe_lanes(p):
        return jnp.broadcast_to(p[:, None, :], (half, N_GROUPS, HEAD_DIM)).reshape(1, half * KV_WIDTH)

    w2e = jnp.einsum('jd,gh->gjhd', w2, eye).reshape(N_GROUPS * CMP_HIDDEN, KV_WIDTH)
    return (pe_lanes(pe[:half]), pe_lanes(pe[half:]),
            expand(w1r[:half]).astype(BF16), expand(w1r[half:]).astype(BF16),
            jnp.tile(b1, N_GROUPS).reshape(1, N_GROUPS * CMP_HIDDEN), w2e.astype(BF16))


def _select_kernel(q_ref, rk_ref, rv_ref, pea_k, peb_k, wa_k, wb_k, b1_k, w2_k, pea_v, peb_v, wa_v, wb_v, b1_v,
                   w2_v, g3_ref, ovlt_ref, egc_ref, mask_ref, ocmp_ref, gate_ref, *, seq):
    ts = SELECT_SUB
    hg = HEADS_PER_GROUP
    rows = hg * ts
    n_sel_blocks = seq // SEL_BLOCK
    sel_shift = int(math.log2(SEL_BLOCK))
    base = pl.program_id(1) * SELECT_Q
    lane = lax.broadcasted_iota(jnp.int32, (ts, LANES), 1)
    row = lax.broadcasted_iota(jnp.int32, (ts, LANES), 0)
    jidx = lax.broadcasted_iota(jnp.int32, (n_sel_blocks, ts), 0)
    qcol = lax.broadcasted_iota(jnp.int32, (n_sel_blocks, ts), 1)
    jloc = lax.broadcasted_iota(jnp.int32, (SUBLANES, ts), 0)
    groups = range(N_GROUPS)
    in_group = [(lane >= HEAD_DIM * g) & (lane < HEAD_DIM * (g + 1)) for g in groups]
    first_half = lane < HEAD_DIM
    zero = jnp.zeros((ts, LANES), BF16)
    kcb = _compress_blocks(rk_ref, pea_k, peb_k, wa_k, wb_k, b1_k, w2_k).astype(BF16)
    vcb = _compress_blocks(rv_ref, pea_v, peb_v, wa_v, wb_v, b1_v, w2_v).astype(BF16)
    subs = range(SELECT_Q // ts)
    units = [(s, g) for s in subs for g in groups]
    sc, p_cmp, o_cmp, imp, maskq = {}, {}, {}, {}, {}

    def scores(u):
        s, g = u
        qh = [q_ref[s * ts:(s + 1) * ts, h * LANES:(h + 1) * LANES] for h in range(hg)]
        qg = jnp.concatenate([jnp.where(in_group[g], x, zero) for x in qh], axis=0)
        sc[u] = _dot_nt(qg, kcb).reshape(hg, ts, LANES)

    def softmax(u):
        s, g = u
        cvalid = (lane * CMP_STRIDE + (CMP_BLOCK - 1)) <= base + s * ts + row
        sv = jnp.where(cvalid[None], sc[u], MASK_VALUE)
        m = jnp.max(sv, axis=-1, keepdims=True)
        e = jnp.where(cvalid[None], jnp.exp(sv - m), 0.0)
        den = jnp.sum(e, axis=-1, keepdims=True)
        p_cmp[u] = e * (1.0 / jnp.where(den > 0.0, den, 1.0))

    def outputs(u):
        o_cmp[u] = _dot(p_cmp[u].reshape(rows, LANES).astype(BF16), vcb)
        psum = p_cmp[u][0]
        for h in range(1, hg):
            psum = psum + p_cmp[u][h]
        p_hi = psum.astype(BF16)
        p_lo = (psum - p_hi.astype(F32)).astype(BF16)
        both = _dot_nt(ovlt_ref[...], jnp.concatenate([p_hi, p_lo], axis=0))
        imp[u] = (both[:, :ts] + both[:, ts:])[0:n_sel_blocks]

    def rank(u):
        s, g = u
        t_lane = base + s * ts + qcol
        forced = (jidx == 0) | (jidx == (t_lane >> sel_shift))
        future = jidx * SEL_BLOCK > t_lane
        imp_g = jnp.where(forced, FORCE_SCORE, jnp.where(future, -1.0, imp[u]))
        tiles = [imp_g[k * SUBLANES:(k + 1) * SUBLANES] for k in range(n_sel_blocks // SUBLANES)]
        cnts = [jnp.zeros((SUBLANES, ts), jnp.int32) for _ in tiles]
        for i in range(n_sel_blocks):
            other = jnp.broadcast_to(imp_g[i:i + 1, :], (SUBLANES, ts))
            ki, ri = divmod(i, SUBLANES)
            for k, tile_k in enumerate(tiles):
                if k < ki:
                    ahead = other > tile_k
                elif k > ki:
                    ahead = other >= tile_k
                else:
                    ahead = (other > tile_k) | ((other == tile_k) & (jloc > ri))
                cnts[k] = cnts[k] + jnp.where(ahead, 1, 0)
        cnt = jnp.concatenate(cnts, axis=0)
        mask_t = jnp.where(cnt < min(N_SEL, n_sel_blocks), 0.0, MASK_VALUE)
        lo_rows = HEAD_DIM * (1 - g)
        parts = [mask_t, jnp.zeros((LANES - n_sel_blocks - lo_rows, ts), F32)]
        if lo_rows:
            parts = [jnp.zeros((lo_rows, ts), F32)] + parts
        maskq[u] = jnp.concatenate(parts, axis=0).T

    def store(s):
        rs = slice(s * ts, (s + 1) * ts)
        mask_ref[rs, :] = (maskq[s, 0] + maskq[s, 1]).astype(BF16)
        gexp = _dot_hilo(g3_ref[rs, :], egc_ref[...])
        gate_ref[rs, :] = gexp[:, ATTN_WIDTH:]
        for h in range(hg):
            hr = slice(h * ts, (h + 1) * ts)
            both = jnp.where(first_half, o_cmp[s, 0][hr], o_cmp[s, 1][hr])
            ocmp_ref[rs, h * LANES:(h + 1) * LANES] = gexp[:, h * LANES:(h + 1) * LANES] * both

    stages = (scores, softmax, outputs, rank)
    for step in range(len(units) + len(stages) - 1):
        for k, stage in enumerate(stages):
            if 0 <= step - k < len(units):
                s, g = units[step - k]
                stage((s, g))
                if stage is rank and g == N_GROUPS - 1:
                    store(s)


def _select(q, rk, rv, pk, pv, g3, ovlt, egc, bsz, seq):
    tsel = SELECT_Q
    nq = seq // tsel
    assert nq == 1, "the compression MLP runs once per select step, i.e. once per sequence"
    qrow = lambda n: pl.BlockSpec((tsel, n), lambda b, i: (b * nq + i, 0))
    cmp_spec = pl.BlockSpec((1,) + rk.shape[1:], lambda b, i: (b, 0, 0))
    full = lambda a: pl.BlockSpec(a.shape, lambda b, i: (0,) * a.ndim)
    return pl.pallas_call(
        functools.partial(_select_kernel, seq=seq),
        grid=(bsz, nq),
        in_specs=[qrow(ATTN_WIDTH), cmp_spec, cmp_spec] + [full(a) for a in pk] + [full(a) for a in pv]
                 + [qrow(LANES), full(ovlt), full(egc)],
        out_specs=[qrow(LANES), qrow(ATTN_WIDTH), qrow(2 * ATTN_WIDTH)],
        out_shape=[jax.ShapeDtypeStruct((bsz * seq, LANES), BF16),
                   jax.ShapeDtypeStruct((bsz * seq, ATTN_WIDTH), F32),
                   jax.ShapeDtypeStruct((bsz * seq, 2 * ATTN_WIDTH), F32)],
        compiler_params=pltpu.CompilerParams(vmem_limit_bytes=VMEM_LIMIT),
        name="select",
    )(q, rk, rv, *pk, *pv, g3, ovlt, egc)


def _attn_kernel(q_ref, mask_ref, ocmp_ref, ks0_ref, ks1_ref, vs0_ref, vs1_ref, kw_ref, vw0_ref, vw1_ref,
                 gate_ref, sz_ref, wbias_ref, dbias_ref, o_ref, slc_ref, win_ref, *, seq):
    tq, tk = ATTN_Q, ATTN_K
    hg = HEADS_PER_GROUP
    rows = hg * tq
    q0 = pl.program_id(1) * tq
    lane = lax.broadcasted_iota(jnp.int32, (tq, LANES), 1)

    span = WINDOW + tq
    start = pl.multiple_of(jnp.maximum(q0 - WINDOW, 0), tq)
    wbias = wbias_ref[0]
    width = ATTN_WIDTH
    ks_refs, vs_refs, vw_refs = (ks0_ref, ks1_ref), (vs0_ref, vs1_ref), (vw0_ref, vw1_ref)

    groups = range(N_GROUPS)
    in_group = [(lane >= HEAD_DIM * g) & (lane < HEAD_DIM * (g + 1)) for g in groups]
    zero = jnp.zeros((tq, LANES), BF16)
    q_heads = [q_ref[:, h * LANES:(h + 1) * LANES] for h in range(hg)]
    qg = [jnp.concatenate([jnp.where(in_group[g], qh, zero) for qh in q_heads], axis=0) for g in groups]
    maskq = mask_ref[...]
    qsel = [jnp.concatenate([jnp.where(in_group[g], qh, maskq) for qh in q_heads], axis=0) for g in groups]

    n_tiles = q0 // tk + 1
    for n in range(1, seq // tk + 1):
        @pl.when(n_tiles == n)
        def _(n=n):
            klen = n * tk
            s2 = [_dot_nt(qsel[g], ks_refs[g][0:klen, :]) for g in groups]
            s3 = [_dot_nt(qg[g], kw_ref[pl.ds(start, span), :]).reshape(hg, tq, span) for g in groups]
            for g in groups:
                last = (s2[g][:, klen - tk:].reshape(hg, tq, tk) + dbias_ref[0][None]).reshape(rows, tk)
                sg = last if n == 1 else jnp.concatenate([s2[g][:, :klen - tk], last], axis=1)
                m2 = jnp.max(sg, axis=-1, keepdims=True)
                p2 = jnp.exp(sg - m2).astype(BF16)
                slc_ref[g] = _dot(p2, vs_refs[g][0:klen, :])
            for g in groups:
                sw = s3[g] + wbias[None]
                m3 = jnp.max(sw, axis=-1, keepdims=True)
                p3 = jnp.exp(sw - m3).astype(BF16).reshape(rows, span)
                win_ref[g] = _dot(p3, vw_refs[g][pl.ds(start, span), :])

    o_slc, o_win = [], []
    for g in groups:
        o_slc.append(slc_ref[g, :, :LANES] * (1.0 / slc_ref[g, :, LANES:]))
        o_win.append(win_ref[g, :, :LANES] * (1.0 / win_ref[g, :, LANES:]))

    first_half = lane < HEAD_DIM
    for h in range(hg):
        rws = slice(h * tq, (h + 1) * tq)
        cols = slice(h * LANES, (h + 1) * LANES)
        acc = ocmp_ref[:, cols]
        for j, branch in enumerate((o_slc, o_win)):
            both = jnp.where(first_half, branch[0][rws], branch[1][rws])
            acc = acc + gate_ref[:, j * width + h * LANES: j * width + (h + 1) * LANES] * both
        o_ref[:, cols] = (acc * sz_ref[:, cols].astype(F32)).astype(BF16)


def _attn(q, maskq, ocmp, ks0, ks1, vs0, vs1, kw, vw0, vw1, gates, sz, wbias, dbias, bsz, seq):
    tq, tk = ATTN_Q, ATTN_K
    nq = seq // tq
    qrow = lambda n: pl.BlockSpec((tq, n), lambda b, i: (b * nq + i, 0))
    per_seq = pl.BlockSpec((seq, LANES), lambda b, i: (b, 0))
    val_seq = pl.BlockSpec((seq, 2 * LANES), lambda b, i: (b, 0))
    diag = pl.BlockSpec((1, tq, tk), lambda b, i: (i % (tk // tq), 0, 0))
    wspec = pl.BlockSpec((1,) + wbias.shape[1:], lambda b, i: (jnp.minimum(i, wbias.shape[0] - 1), 0, 0))
    return pl.pallas_call(
        functools.partial(_attn_kernel, seq=seq),
        grid=(bsz, nq),
        in_specs=[qrow(ATTN_WIDTH), qrow(LANES), qrow(ATTN_WIDTH), per_seq, per_seq, val_seq, val_seq, per_seq,
                  val_seq, val_seq]
                 + [qrow(2 * ATTN_WIDTH), qrow(ATTN_WIDTH), wspec, diag],
        out_specs=qrow(ATTN_WIDTH),
        out_shape=jax.ShapeDtypeStruct((bsz * seq, ATTN_WIDTH), BF16),
        scratch_shapes=[pltpu.VMEM((N_GROUPS, HEADS_PER_GROUP * tq, 2 * LANES), F32)] * 2,
        compiler_params=pltpu.CompilerParams(vmem_limit_bytes=VMEM_LIMIT),
        name="attn",
    )(q, maskq, ocmp, ks0, ks1, vs0, vs1, kw, vw0, vw1, gates, sz, wbias, dbias)


def _ssm_kernel(u_ref, mi_ref, ws_ref, wo_ref, al_ref, y_ref, sx_ref, *, n_chunks):
    u = u_ref[0]
    n_tiles = sx_ref.shape[0]
    nt = n_tiles // 2
    tile = lambda j: slice(j * LANES, (j + 1) * LANES)
    bt = SSM_BT
    pitch = sx_ref.shape[1] // bt
    seq_rows = lambda b: pl.ds(b * pitch, n_chunks)
    for j in range(n_tiles):
        inj = _dot(u, ws_ref[0, :, tile(j)])
        for b in range(bt):
            sx_ref[j, seq_rows(b), :] = inj[b * n_chunks:(b + 1) * n_chunks]
    a_re = [al_ref[0, 0:1, tile(j)] for j in range(nt)]
    a_im = [al_ref[0, 1:2, tile(j)] for j in range(nt)]

    def body(k, carry):
        rows = pl.ds(k, bt, stride=pitch)
        new = []
        for j in range(nt):
            x_re, x_im = carry[j], carry[nt + j]
            s_re = sx_ref[j, rows, :]
            s_im = sx_ref[nt + j, rows, :]
            sx_ref[j, rows, :] = x_re
            sx_ref[nt + j, rows, :] = x_im
            new.append((a_re[j] * x_re - a_im[j] * x_im + s_re, a_re[j] * x_im + a_im[j] * x_re + s_im))
        return tuple([n[0] for n in new] + [n[1] for n in new])

    zero = jnp.zeros((bt, LANES), F32)
    lax.fori_loop(0, n_chunks, body, (zero,) * n_tiles, unroll=SSM_UNROLL)
    width = u.shape[1]
    ctile = 2 * LANES
    y_cols = [_dot(u[:, :(c + 1) * ctile], mi_ref[0, :(c + 1) * ctile, c * ctile:(c + 1) * ctile])
              for c in range(width // ctile)]
    xs = jnp.concatenate(
        [jnp.concatenate([sx_ref[j, seq_rows(b), :] for b in range(bt)], axis=0).astype(BF16)
         for j in range(n_tiles)], axis=1)
    y = jnp.concatenate(y_cols, axis=1) + _dot(xs, wo_ref[0])
    for j in range(SSM_CHUNK):
        y_ref[0, pl.ds(j, y.shape[0], stride=SSM_CHUNK), :] = y[:, tile(j)]


def _ssm(u3, mi, ws, wo, al, n_chunks):
    halves, rows, width = u3.shape
    r = SSM_BT * n_chunks
    wspec = lambda a: pl.BlockSpec((1,) + a.shape[1:], lambda hf, i: (hf, 0, 0))
    return pl.pallas_call(
        functools.partial(_ssm_kernel, n_chunks=n_chunks),
        grid=(halves, rows // r),
        in_specs=[pl.BlockSpec((1, r, width), lambda hf, i: (hf, i, 0)),
                  wspec(mi), wspec(ws), wspec(wo), wspec(al)],
        out_specs=pl.BlockSpec((1, r * SSM_CHUNK, LANES), lambda hf, i: (hf, i, 0)),
        out_shape=jax.ShapeDtypeStruct((halves, rows * SSM_CHUNK, LANES), F32),
        scratch_shapes=[pltpu.VMEM((ws.shape[2] // LANES, SSM_BT * (n_chunks + SUBLANES), LANES), F32)],
        compiler_params=pltpu.CompilerParams(vmem_limit_bytes=VMEM_LIMIT),
        name="ssm",
    )(u3, mi, ws, wo, al)


def _ssm_params(lam_re, lam_im, log_dt, b_re, b_im, c_re, c_im, d_skip):
    L = SSM_CHUNK
    G, P, C = SSM_GROUPS, SSM_STATE, SSM_GROUP
    gh = G // SSM_HALVES
    dt = jnp.exp(log_dt)[:, None]
    lr, li = lam_re, lam_im
    mag = jnp.exp(lr * dt)
    ab_re = mag * jnp.cos(li * dt)
    ab_im = mag * jnp.sin(li * dt)
    nr, ni = ab_re - 1.0, ab_im
    den = lr * lr + li * li
    cr = ((nr * lr + ni * li) / den)[..., None]
    ci = ((ni * lr - nr * li) / den)[..., None]
    bb_re = cr * b_re - ci * b_im
    bb_im = cr * b_im + ci * b_re
    pr, pi = [jnp.ones_like(ab_re)], [jnp.zeros_like(ab_im)]
    for _ in range(L):
        pr.append(pr[-1] * ab_re - pi[-1] * ab_im)
        pi.append(pr[-2] * ab_im + pi[-1] * ab_re)
    pw_re = jnp.stack(pr)
    pw_im = jnp.stack(pi)
    ca_re = jnp.einsum('gcp,dgp->dgcp', c_re, pw_re[:L]) - jnp.einsum('gcp,dgp->dgcp', c_im, pw_im[:L])
    ca_im = jnp.einsum('gcp,dgp->dgcp', c_re, pw_im[:L]) + jnp.einsum('gcp,dgp->dgcp', c_im, pw_re[:L])
    kern = jnp.einsum('dgop,gpi->dgoi', ca_re, bb_re) - jnp.einsum('dgop,gpi->dgoi', ca_im, bb_im)
    lag = np.arange(L)[None, :] - np.arange(L)[:, None]
    lag_or_zero_block = np.where(lag >= 0, lag, L)
    e_re, e_im = pw_re[:L][::-1], pw_im[:L][::-1]
    ws_re = jnp.einsum('sgp,gpi->sgip', e_re, bb_re) - jnp.einsum('sgp,gpi->sgip', e_im, bb_im)
    ws_im = jnp.einsum('sgp,gpi->sgip', e_re, bb_im) + jnp.einsum('sgp,gpi->sgip', e_im, bb_re)
    o_re, o_im = pw_re[1:L + 1], pw_im[1:L + 1]
    wo_re = jnp.einsum('gop,tgp->tgop', c_re, o_re) - jnp.einsum('gop,tgp->tgop', c_im, o_im)
    wo_im = -(jnp.einsum('gop,tgp->tgop', c_re, o_im) + jnp.einsum('gop,tgp->tgop', c_im, o_re))

    def group_diag(a, rows_per_group, cols_per_group):
        tiled = jnp.tile(a, (1, gh))
        rg = (lax.broadcasted_iota(jnp.int32, tiled.shape, 0) // rows_per_group) % gh
        cg = lax.broadcasted_iota(jnp.int32, tiled.shape, 1) // cols_per_group
        return jnp.where(rg == cg, tiled, 0.0)

    mi_h, ws_h, wo_h, al_h = [], [], [], []
    for hf in range(SSM_HALVES):
        gs = slice(hf * gh, (hf + 1) * gh)
        k_gi_o = kern[:, gs].transpose(0, 1, 3, 2).reshape(L * gh * C, C)
        blocks = group_diag(k_gi_o, C, C).reshape(L, gh * C, gh * C)
        blocks = blocks.at[0].add(jnp.diag(d_skip[hf * gh * C:(hf + 1) * gh * C]))
        blocks = jnp.concatenate([blocks, jnp.zeros((1, gh * C, gh * C), F32)], axis=0)
        mi_h.append(blocks[lag_or_zero_block].transpose(0, 2, 1, 3).reshape(L * gh * C, L * gh * C))
        wsr = group_diag(ws_re[:, gs].reshape(L * gh * C, P), C, P)
        wsi = group_diag(ws_im[:, gs].reshape(L * gh * C, P), C, P)
        ws_h.append(jnp.concatenate([wsr, wsi], axis=1))
        wor = group_diag(wo_re[:, gs].reshape(L * gh * C, P), C, P).T
        woi = group_diag(wo_im[:, gs].reshape(L * gh * C, P), C, P).T
        wo_h.append(jnp.concatenate([wor, woi], axis=0))
        al_h.append(jnp.stack([pw_re[L, gs].reshape(gh * P), pw_im[L, gs].reshape(gh * P)]))
    return (jnp.stack(mi_h).astype(BF16), jnp.stack(ws_h).astype(BF16), jnp.stack(wo_h).astype(BF16),
            jnp.stack(al_h))


def _tail_kernel(x_ref, oa_ref, y_ref, szs_ref, gain_ref, scale_ref, shift_ref, gate_ref, wm_ref, wglu_ref,
                 wpa_ref, wps_ref, wout_ref, fg_ref, o_ref):
    d = x_ref.shape[1]
    half = x_ref.shape[0] // 2
    halves = (slice(0, half), slice(half, 2 * half))
    staged = {}

    def merge_gates(r):
        x = x_ref[r, :]
        hb = _modulated_norm(x, gain_ref[...], scale_ref[0], shift_ref[0]).astype(BF16)
        staged[r.start] = (x, _sigmoid(_dot(hb, wm_ref[...])))

    def merge_and_project(r):
        x, mg = staged[r.start]
        y = jnp.concatenate([y_ref[0, r, :], y_ref[1, r, :]], axis=-1)
        yy = _dot(_gelu(y).astype(BF16), wglu_ref[...])
        o_ssm = yy[:, :SSM_WIDTH] * _sigmoid(yy[:, SSM_WIDTH:]) * szs_ref[r, :].astype(F32)
        merged = (mg[:, :d] * _dot(oa_ref[r, :], wpa_ref[...])
                  + mg[:, d:] * _dot(o_ssm.astype(BF16), wps_ref[...]))
        xo = x + gate_ref[0] * _dot(merged.astype(BF16), wout_ref[...])
        ms = jnp.mean(xo * xo, axis=-1, keepdims=True)
        o_ref[r, :] = xo * lax.rsqrt(ms + NORM_EPS) * fg_ref[...]

    merge_gates(halves[0])
    merge_gates(halves[1])
    merge_and_project(halves[0])
    merge_and_project(halves[1])


def _tail(x2, oa, y3, szs, gain, scale, shift, gate, wm, wglu, wpa, wps, wout, fgain, seq):
    t, d = x2.shape
    tm = PROJ_ROWS
    per_seq = seq // tm
    row = lambda n: pl.BlockSpec((tm, n), lambda i: (i, 0))
    full = lambda a: pl.BlockSpec(a.shape, lambda i: (0,) * a.ndim)
    mod = pl.BlockSpec((1, 1, d), lambda i: (i // per_seq, 0, 0))
    return pl.pallas_call(
        _tail_kernel,
        grid=(t // tm,),
        in_specs=[row(d), row(ATTN_WIDTH), pl.BlockSpec((SSM_HALVES, tm, LANES), lambda i: (0, i, 0)),
                  row(SSM_WIDTH), full(gain), mod, mod, mod,
                  full(wm), full(wglu), full(wpa), full(wps), full(wout), full(fgain)],
        out_specs=row(d),
        out_shape=jax.ShapeDtypeStruct((t, d), F32),
        compiler_params=pltpu.CompilerParams(vmem_limit_bytes=VMEM_LIMIT),
        name="tail",
    )(x2, oa, y3, szs, gain, scale, shift, gate, wm, wglu, wpa, wps, wout, fgain)


def _rope_tables(seq):
    half = HEAD_DIM // 2
    inv_freq = ROPE_THETA ** (-jnp.arange(half, dtype=F32) / half)
    ang = jnp.arange(seq, dtype=F32)[:, None] * inv_freq[None, :]
    cos, sin = jnp.cos(ang), jnp.sin(ang)
    zero = jnp.zeros_like(sin)
    reps = LANES // HEAD_DIM
    cos_t = jnp.tile(jnp.concatenate([cos, cos], axis=1), (1, reps))
    sa_t = jnp.tile(jnp.concatenate([-sin, zero], axis=1), (1, reps))
    sb_t = jnp.tile(jnp.concatenate([zero, sin], axis=1), (1, reps))
    return cos_t, sa_t, sb_t


def _heads_to_lanes(w, axis):
    shape = w.shape
    split = shape[:axis] + (N_GROUPS, HEADS_PER_GROUP, HEAD_DIM) + shape[axis + 1:]
    return jnp.swapaxes(w.reshape(split), axis, axis + 1).reshape(shape)


def _proj_weight(w_in):
    d = w_in.shape[0]
    o_q, o_kv, o_g, o_z, o_u, o_zs, o_m = 0, 512, 1280, 1304, 1816, 2072, 2328
    kv = lambda j: w_in[:, o_kv + j * KV_WIDTH: o_kv + (j + 1) * KV_WIDTH]
    gates = jnp.pad(w_in[:, o_g:o_z], ((0, 0), (0, LANES - 3 * N_HEADS)))
    cols = [_heads_to_lanes(w_in[:, o_q:o_kv], 1) * (HEAD_DIM ** -0.5),
            kv(0), kv(2), kv(4), kv(1), kv(3), kv(5), gates,
            _heads_to_lanes(w_in[:, o_z:o_u], 1), w_in[:, o_u:o_zs], w_in[:, o_zs:o_m]]
    w = jnp.concatenate(cols, axis=1)
    assert w.shape == (d, _C_END)
    return w.astype(BF16), w_in[:, o_m:].astype(BF16)


def _attn_constants(seq):
    n_cmp = LANES
    cs = np.arange(n_cmp) * CMP_STRIDE
    ss = np.arange(LANES) * SEL_BLOCK
    ovl = (np.minimum(cs[:, None] + CMP_BLOCK, ss[None, :] + SEL_BLOCK) > np.maximum(cs[:, None], ss[None, :]))
    ovl = ovl & (np.arange(LANES)[None, :] < seq // SEL_BLOCK) & (cs[:, None] + CMP_BLOCK <= seq)
    tq, tk = ATTN_Q, ATTN_K
    v = np.arange(tk // tq)[:, None, None]
    r = np.arange(tq)[None, :, None]
    cc = np.arange(tk)[None, None, :]
    dbias = np.where(cc <= v * tq + r, 0.0, MASK_VALUE).astype(np.float32)
    eg = np.zeros((LANES, 3 * ATTN_WIDTH), np.float32)
    for g in range(N_GROUPS):
        for h in range(HEADS_PER_GROUP):
            for j in range(3):
                base = j * ATTN_WIDTH + h * LANES + g * HEAD_DIM
                eg[(g * HEADS_PER_GROUP + h) * 3 + j, base:base + HEAD_DIM] = 1.0
    span = WINDOW + tq
    q0 = (np.arange(WINDOW // tq + 1) * tq)[:, None, None]
    kp = np.maximum(q0 - WINDOW, 0) + np.arange(span)[None, None, :]
    tt = q0 + np.arange(tq)[None, :, None]
    wbias = np.where((kp <= tt) & (kp > tt - WINDOW), 0.0, MASK_VALUE).astype(np.float32)
    return (jnp.asarray(ovl.T, BF16), jnp.asarray(eg, BF16), jnp.asarray(wbias, F32), jnp.asarray(dbias, F32))


def kernel(x, c, w_ada, b_ada, norm_gain, w_in, pe_cmp_k, w_cmp_k1, b_cmp_k1, w_cmp_k2, pe_cmp_v, w_cmp_v1,
           b_cmp_v1, w_cmp_v2, lam_re, lam_im, log_dt, b_re, b_im, c_re, c_im, d_skip, w_glu, w_proj_attn,
           w_proj_ssm, w_out, final_gain):
    bsz, seq, d = x.shape
    depth = w_in.shape[0]
    assert depth == 1, "the tail kernel fuses the final norm into the (single) layer"
    assert bsz % SSM_BT == 0 and seq % SSM_CHUNK == 0
    assert seq % ATTN_K == 0 and seq % SELECT_Q == 0 and seq % PROJ_ROWS == 0 and seq >= WINDOW + ATTN_Q
    assert (seq - CMP_BLOCK) // CMP_STRIDE + 1 <= LANES and seq // CMP_STRIDE == LANES
    t = bsz * seq
    cos_t, sa_t, sb_t = _rope_tables(seq)
    ovlt, eg, wbias, dbias = _attn_constants(seq)
    n_chunks = seq // SSM_CHUNK

    x2 = x.reshape(t, d)
    for l in range(depth):
        mod = _ada(c, w_ada[l], b_ada[l])
        shift, mscale, gate = [m.reshape(bsz, 1, d) for m in jnp.split(mod, 3, axis=-1)]
        gain = norm_gain[l].reshape(1, d)
        w_proj, w_merge = _proj_weight(w_in[l])
        (q, kc_r, ks0, ks1, kw, vc_r, vs0, vs1, vw0, vw1, g3, sz, u, szs) = _proj(
            x2, gain, mscale, shift, cos_t, sa_t, sb_t, w_proj, seq)

        nch = seq // CMP_STRIDE
        maskq, ocmp, gates = _select(
            q, kc_r.reshape(bsz, nch, CMP_STRIDE * LANES), vc_r.reshape(bsz, nch, CMP_STRIDE * LANES),
            _compress_params(pe_cmp_k[l], w_cmp_k1[l], b_cmp_k1[l], w_cmp_k2[l]),
            _compress_params(pe_cmp_v[l], w_cmp_v1[l], b_cmp_v1[l], w_cmp_v2[l]), g3, ovlt, eg, bsz, seq)
        o_attn = _attn(q, maskq, ocmp, ks0, ks1, vs0, vs1, kw, vw0, vw1, gates, sz, wbias, dbias, bsz, seq)

        mi, ws, wo, al = _ssm_params(lam_re[l], lam_im[l], log_dt[l], b_re[l], b_im[l], c_re[l], c_im[l],
                                     d_skip[l])
        y3 = _ssm(u, mi, ws, wo, al, n_chunks)

        x2 = _tail(x2, o_attn, y3, szs, gain, mscale, shift, gate, w_merge, w_glu[l].astype(BF16),
                   _heads_to_lanes(w_proj_attn[l], 0).astype(BF16), w_proj_ssm[l].astype(BF16),
                   w_out[l].astype(BF16),
                   final_gain.reshape(1, d) if l == depth - 1 else jnp.ones((1, d), F32), seq)
    return x2.reshape(bsz, seq, d)
```
